```python
import jax, jax.numpy as jnp
from jax import lax
import numpy as np

D_MODEL = 2048
BATCH = 2
SEQ = 4096
DEPTH = 4
DEC_BATCH = 32
DEC_SEQ = 1
PAST_LEN = 16384
PAGE_SIZE = 128

N_HEADS = 16
N_KV_HEADS = 4
HEAD_DIM = 64
GROUP = N_HEADS // N_KV_HEADS
ATTN_WIDTH = N_HEADS * HEAD_DIM
KV_WIDTH = N_KV_HEADS * HEAD_DIM
WINDOW = 128
BLOCK = WINDOW
ROPE_DIM = HEAD_DIM // 4
ROPE_THETA = 500000.0
POOL_WINDOWS = (2, 4, 8, 16)
N_POOL_GROUPS = len(POOL_WINDOWS)
POOL_WIDTH = D_MODEL - ATTN_WIDTH
POOL_GROUP_DIM = POOL_WIDTH // N_POOL_GROUPS
POOL_STATE = max(POOL_WINDOWS) - 1
MIX_WIDTH = ATTN_WIDTH + POOL_WIDTH
IN_WIDTH = ATTN_WIDTH + 2 * KV_WIDTH + POOL_WIDTH
D_FF = 5632
N_EXPERTS = 8
TOP_K = 2
D_FF_EXPERT = D_MODEL // 2
N_DENSE = (DEPTH + 1) // 2
N_MOE = DEPTH // 2
EPS = 1e-5
NEG_INF = -1e30

kernel_name = "hymba_swa_sink_pool_moe_step"


def rmsnorm(x, g):
    xf = x.astype(jnp.float32)
    xf = xf * lax.rsqrt(jnp.mean(xf * xf, axis=-1, keepdims=True) + EPS)
    return (xf * g.astype(jnp.float32)).astype(x.dtype)


def rope(x, pos):
    half = ROPE_DIM // 2
    inv = ROPE_THETA ** (-jnp.arange(0, ROPE_DIM, 2, dtype=jnp.float32) / ROPE_DIM)
    ang = pos[:, None] * inv[None, :]
    cos = jnp.cos(ang)[None, :, None, :]
    sin = jnp.sin(ang)[None, :, None, :]
    xr = x[..., :ROPE_DIM].astype(jnp.float32)
    x1, x2 = xr[..., :half], xr[..., half:]
    rot = jnp.concatenate([x1 * cos - x2 * sin, x2 * cos + x1 * sin], axis=-1).astype(x.dtype)
    return jnp.concatenate([rot, x[..., ROPE_DIM:]], axis=-1)


def in_proj(h, w_in):
    b, s, _ = h.shape
    z = h @ w_in
    q = z[..., :ATTN_WIDTH].reshape(b, s, N_HEADS, HEAD_DIM)
    k = z[..., ATTN_WIDTH:ATTN_WIDTH + KV_WIDTH].reshape(b, s, N_KV_HEADS, HEAD_DIM)
    v = z[..., ATTN_WIDTH + KV_WIDTH:ATTN_WIDTH + 2 * KV_WIDTH].reshape(b, s, N_KV_HEADS, HEAD_DIM)
    u = z[..., ATTN_WIDTH + 2 * KV_WIDTH:]
    return q, k, v, u


def sink_attend(scores, mask, sinks, v, eq):
    s = jnp.where(mask, scores, NEG_INF)
    sk = sinks.astype(jnp.float32).reshape(N_KV_HEADS, GROUP, 1, 1)
    m = jnp.maximum(jnp.max(s, axis=-1, keepdims=True), sk)
    p = jnp.exp(s - m)
    denom = jnp.sum(p, axis=-1, keepdims=True) + jnp.exp(sk - m)
    return jnp.einsum(eq, (p / denom).astype(v.dtype), v)


def attn_prompt(q, k, v, sinks):
    b, s = q.shape[0], q.shape[1]
    nb = s // BLOCK
    qb = q.reshape(b, nb, BLOCK, N_KV_HEADS, GROUP, HEAD_DIM)
    kb = k.reshape(b, nb, BLOCK, N_KV_HEADS, HEAD_DIM)
    vb = v.reshape(b, nb, BLOCK, N_KV_HEADS, HEAD_DIM)
    kprev = jnp.concatenate([jnp.zeros_like(kb[:, :1]), kb[:, :-1]], axis=1)
    vprev = jnp.concatenate([jnp.zeros_like(vb[:, :1]), vb[:, :-1]], axis=1)
    keys = jnp.concatenate([kprev, kb], axis=2)
    vals = jnp.concatenate([vprev, vb], axis=2)
    scores = jnp.einsum('bnqkgd,bnjkd->bnkgqj', qb, keys).astype(jnp.float32) * (HEAD_DIM ** -0.5)
    blk = jnp.arange(nb)[:, None]
    qpos = blk * BLOCK + jnp.arange(BLOCK)[None, :]
    kpos = (blk - 1) * BLOCK + jnp.arange(2 * BLOCK)[None, :]
    diff = qpos[:, :, None] - kpos[:, None, :]
    mask = (diff >= 0) & (diff < WINDOW) & (kpos[:, None, :] >= 0)
    out = sink_attend(scores, mask[None, :, None, None], sinks, vals, 'bnkgqj,bnjkd->bnqkgd')
    return out.reshape(b, s, ATTN_WIDTH)


def attn_sample(q, k, v, k_buf, v_buf, sinks):
    b, s = q.shape[0], q.shape[1]
    qg = q.reshape(b, s, N_KV_HEADS, GROUP, HEAD_DIM)
    kext = jnp.concatenate([k_buf, k], axis=1)
    vext = jnp.concatenate([v_buf, v], axis=1)
    scores = jnp.einsum('bqkgd,bjkd->bkgqj', qg, kext).astype(jnp.float32) * (HEAD_DIM ** -0.5)
    qpos = PAST_LEN + jnp.arange(s)
    kpos = PAST_LEN - WINDOW + jnp.arange(WINDOW + s)
    diff = qpos[:, None] - kpos[None, :]
    mask = (diff >= 0) & (diff < WINDOW)
    out = sink_attend(scores, mask[None, None, None], sinks, vext, 'bkgqj,bjkd->bqkgd')
    return out.reshape(b, s, ATTN_WIDTH), kext[:, -WINDOW:], vext[:, -WINDOW:]


def pool_mix(u_ext, pos0, n_out, w_pool, scale):
    b, l, _ = u_ext.shape
    uf = u_ext.astype(jnp.float32)
    cs = jnp.concatenate([jnp.zeros((b, 1, POOL_WIDTH), jnp.float32), jnp.cumsum(uf, axis=1)], axis=1)
    rows = jnp.arange(l - n_out, l)
    end = rows + 1
    pos = pos0 + rows
    means = []
    for g, w in enumerate(POOL_WINDOWS):
        c = cs[..., g * POOL_GROUP_DIM:(g + 1) * POOL_GROUP_DIM]
        lo = jnp.maximum(end - w, 0)
        wsum = jnp.take(c, end, axis=1) - jnp.take(c, lo, axis=1)
        cnt = jnp.minimum(w, pos + 1).astype(jnp.float32)
        means.append(wsum / cnt[None, :, None])
    d = jnp.concatenate(means, axis=-1) - uf[:, l - n_out:]
    d = d.astype(u_ext.dtype).reshape(b, n_out, N_POOL_GROUPS, POOL_GROUP_DIM)
    y = jnp.einsum('bsgc,gcd->bsgd', d, w_pool).reshape(b, n_out, POOL_WIDTH)
    return y * scale


def dense_ffn(x, wg, wu, wd):
    return (jax.nn.silu(x @ wg) * (x @ wu)) @ wd


def moe_ffn(x, w_router, b_router, wg, wu, wd):
    b, s, d = x.shape
    t = x.reshape(b * s, d)
    logits = (t @ w_router).astype(jnp.float32) + b_router.astype(jnp.float32)
    topv, topi = lax.top_k(logits, TOP_K)
    gates = jax.nn.softmax(topv, axis=-1)
    combine = jnp.sum(jax.nn.one_hot(topi, N_EXPERTS, dtype=jnp.float32) * gates[..., None], axis=1)
    hg = jnp.einsum('td,edf->tef', t, wg)
    hu = jnp.einsum('td,edf->tef', t, wu)
    hh = jax.nn.silu(hg) * hu * combine[..., None].astype(x.dtype)
    y = jnp.einsum('tef,efd->td', hh, wd)
    return y.reshape(b, s, d)


def setup_inputs(seed: int = 0) -> dict:
    key = jax.random.key(seed)
    ks = jax.random.split(key, 24)
    f32 = jnp.float32
    nrm = lambda k, shape, sc: jax.random.normal(k, shape, f32) * sc
    return {
        "x_prompt": nrm(ks[0], (BATCH, SEQ, D_MODEL), 1.0),
        "x_sample": nrm(ks[1], (DEC_BATCH, DEC_SEQ, D_MODEL), 1.0),
        "cache_k": nrm(ks[2], (DEPTH, DEC_BATCH, WINDOW, N_KV_HEADS, HEAD_DIM), 1.0),
        "cache_v": nrm(ks[3], (DEPTH, DEC_BATCH, WINDOW, N_KV_HEADS, HEAD_DIM), 1.0),
        "state_pool": nrm(ks[4], (DEPTH, DEC_BATCH, POOL_STATE, POOL_WIDTH), 1.0),
        "norm_mix": 1.0 + nrm(ks[5], (DEPTH, D_MODEL), 0.1),
        "w_in": nrm(ks[6], (DEPTH, D_MODEL, IN_WIDTH), D_MODEL ** -0.5),
        "attn_sinks": nrm(ks[7], (DEPTH, N_HEADS), 0.5),
        "w_pool": nrm(ks[8], (DEPTH, N_POOL_GROUPS, POOL_GROUP_DIM, POOL_GROUP_DIM), POOL_GROUP_DIM ** -0.5),
        "pool_scale": 1.0 + nrm(ks[9], (DEPTH, POOL_WIDTH), 0.1),
        "w_out": nrm(ks[10], (DEPTH, MIX_WIDTH, D_MODEL), MIX_WIDTH ** -0.5),
        "norm_ffn": 1.0 + nrm(ks[11], (DEPTH, D_MODEL), 0.1),
        "w_gate_dense": nrm(ks[12], (N_DENSE, D_MODEL, D_FF), D_MODEL ** -0.5),
        "w_up_dense": nrm(ks[13], (N_DENSE, D_MODEL, D_FF), D_MODEL ** -0.5),
        "w_down_dense": nrm(ks[14], (N_DENSE, D_FF, D_MODEL), D_FF ** -0.5),
        "w_router": nrm(ks[15], (N_MOE, D_MODEL, N_EXPERTS), D_MODEL ** -0.5),
        "b_router": nrm(ks[16], (N_MOE, N_EXPERTS), 0.01),
        "w_gate_exp": nrm(ks[17], (N_MOE, N_EXPERTS, D_MODEL, D_FF_EXPERT), D_MODEL ** -0.5),
        "w_up_exp": nrm(ks[18], (N_MOE, N_EXPERTS, D_MODEL, D_FF_EXPERT), D_MODEL ** -0.5),
        "w_down_exp": nrm(ks[19], (N_MOE, N_EXPERTS, D_FF_EXPERT, D_MODEL), D_FF_EXPERT ** -0.5),
        "norm_final": 1.0 + nrm(ks[20], (D_MODEL,), 0.1),
    }


def reference(x_prompt, x_sample, cache_k, cache_v, state_pool, norm_mix, w_in, attn_sinks,
              w_pool, pool_scale, w_out, norm_ffn, w_gate_dense, w_up_dense, w_down_dense,
              w_router, b_router, w_gate_exp, w_up_exp, w_down_exp, norm_final):
    xp, xs = x_prompt, x_sample
    s_p, s_s = xp.shape[1], xs.shape[1]
    pos_p = jnp.arange(s_p, dtype=jnp.float32)
    pos_s = PAST_LEN + jnp.arange(s_s, dtype=jnp.float32)
    nk_p, nv_p, nu_p, nk_s, nv_s, nu_s = [], [], [], [], [], []
    for l in range(DEPTH):
        hp = rmsnorm(xp, norm_mix[l])
        hs = rmsnorm(xs, norm_mix[l])
        qp, kp, vp, up = in_proj(hp, w_in[l])
        qs, ks_, vs, us = in_proj(hs, w_in[l])
        qp, kp = rope(qp, pos_p), rope(kp, pos_p)
        qs, ks_ = rope(qs, pos_s), rope(ks_, pos_s)
        ap = attn_prompt(qp, kp, vp, attn_sinks[l])
        a_s, kbuf, vbuf = attn_sample(qs, ks_, vs, cache_k[l], cache_v[l], attn_sinks[l])
        pp = pool_mix(up, 0, s_p, w_pool[l], pool_scale[l])
        u_ext = jnp.concatenate([state_pool[l], us], axis=1)
        ps = pool_mix(u_ext, PAST_LEN - POOL_STATE, s_s, w_pool[l], pool_scale[l])
        xp = xp + jnp.concatenate([ap, pp], axis=-1) @ w_out[l]
        xs = xs + jnp.concatenate([a_s, ps], axis=-1) @ w_out[l]
        nk_p.append(kp[:, -WINDOW:])
        nv_p.append(vp[:, -WINDOW:])
        nu_p.append(up[:, -POOL_STATE:])
        nk_s.append(kbuf)
        nv_s.append(vbuf)
        nu_s.append(u_ext[:, -POOL_STATE:])
        hp = rmsnorm(xp, norm_ffn[l])
        hs = rmsnorm(xs, norm_ffn[l])
        i = l // 2
        if l % 2 == 0:
            xp = xp + dense_ffn(hp, w_gate_dense[i], w_up_dense[i], w_down_dense[i])
            xs = xs + dense_ffn(hs, w_gate_dense[i], w_up_dense[i], w_down_dense[i])
        else:
            xp = xp + moe_ffn(hp, w_router[i], b_router[i], w_gate_exp[i], w_up_exp[i], w_down_exp[i])
            xs = xs + moe_ffn(hs, w_router[i], b_router[i], w_gate_exp[i], w_up_exp[i], w_down_exp[i])
    y_prompt = rmsnorm(xp, norm_final)
    y_sample = rmsnorm(xs, norm_final)
    return (y_prompt, y_sample, jnp.stack(nk_p), jnp.stack(nv_p), jnp.stack(nu_p),
            jnp.stack(nk_s), jnp.stack(nv_s), jnp.stack(nu_s))
```

```python
import functools

import jax
import jax.numpy as jnp
from jax import lax
from jax.experimental import pallas as pl
from jax.experimental.pallas import tpu as pltpu

F32 = jnp.float32
BF16 = jnp.bfloat16

D_MODEL = 2048
BATCH = 2
SEQ = 4096
DEPTH = 4
DEC_BATCH = 32
PAST_LEN = 16384
N_HEADS = 16
N_KV_HEADS = 4
HEAD_DIM = 64
GROUP = N_HEADS // N_KV_HEADS
ATTN_WIDTH = N_HEADS * HEAD_DIM
KV_WIDTH = N_KV_HEADS * HEAD_DIM
WINDOW = 128
ROPE_DIM = HEAD_DIM // 4
ROPE_THETA = 500000.0
POOL_WINDOWS = (2, 4, 8, 16)
POOL_WIDTH = D_MODEL - ATTN_WIDTH
POOL_GROUP_DIM = POOL_WIDTH // len(POOL_WINDOWS)
POOL_STATE = max(POOL_WINDOWS) - 1
IN_WIDTH = ATTN_WIDTH + 2 * KV_WIDTH + POOL_WIDTH
D_FF = 5632
N_EXPERTS = 8
D_FF_EXPERT = D_MODEL // 2
EPS = 1e-5
NEG_INF = -1e30

LANES = 128
BF16_ROWS = 16
VMEM_LIMIT = 56 * 1024 * 1024
DENSE_SLABS = 4
ROUTER_PAD = LANES


def _params(n_axes=1):
    return pltpu.CompilerParams(dimension_semantics=("arbitrary",) * n_axes,
                                vmem_limit_bytes=VMEM_LIMIT)


def _rms(x, g):
    ms = jnp.mean(x * x, axis=-1, keepdims=True)
    return (x * lax.rsqrt(ms + EPS)) * g


def _stash_piece(step, w_ref, wbf, n_pieces):
    rows = w_ref.shape[0]

    @pl.when(step < n_pieces)
    def _():
        r0 = pl.multiple_of(step * rows, rows)
        wbf[pl.ds(r0, rows), :] = w_ref[...].astype(BF16)


def _piece_rows(total_rows, n_pieces):
    rows, rem = divmod(total_rows, n_pieces)
    assert rem == 0 and rows % BF16_ROWS == 0, (total_rows, n_pieces)
    return rows


def _inproj_kernel(x_ref, g_ref, w_ref, ra_ref, rb_ref, rc_ref, q_ref, k_ref, v_ref, u_ref, *rest,
                   n_pieces, with_tails):
    if with_tails:
        kvt_ref, ut_ref, wbf = rest
    else:
        (wbf,) = rest
    step = pl.program_id(0)
    _stash_piece(step, w_ref, wbf, n_pieces)

    @pl.when(step >= n_pieces)
    def _():
        tm = x_ref.shape[0]
        h = _rms(x_ref[...], g_ref[...])
        z = jnp.dot(h.astype(BF16), wbf[...], preferred_element_type=F32)
        ra, rb, rc = ra_ref[...], rb_ref[...], rc_ref[...]
        n_rot = (ATTN_WIDTH + KV_WIDTH) // LANES
        rot = []
        for c in range(n_rot):
            zc = z[:, c * LANES:(c + 1) * LANES]
            r = zc * ra + pltpu.roll(zc, ROPE_DIM // 2, 1) * rb + pltpu.roll(zc, LANES - ROPE_DIM // 2, 1) * rc
            if c >= ATTN_WIDTH // LANES:
                r = r * (HEAD_DIM ** 0.5)
            rot.append(r)
        nq = ATTN_WIDTH // LANES
        for c in range(nq):
            q_ref[:, c * LANES:(c + 1) * LANES] = rot[c].astype(q_ref.dtype)
        for c in range(nq, n_rot):
            k_ref[:, (c - nq) * LANES:(c - nq + 1) * LANES] = rot[c].astype(k_ref.dtype)
        v = z[:, ATTN_WIDTH + KV_WIDTH:ATTN_WIDTH + 2 * KV_WIDTH]
        u = z[:, ATTN_WIDTH + 2 * KV_WIDTH:]
        v_ref[...] = v.astype(v_ref.dtype)
        u_ref[...] = u.astype(u_ref.dtype)
        if with_tails:
            for c in range(nq, n_rot):
                kvt_ref[:, (c - nq) * LANES:(c - nq + 1) * LANES] = rot[c][tm - WINDOW:, :]
            kvt_ref[:, KV_WIDTH:] = v[tm - WINDOW:, :]
            ut_ref[...] = u[tm - BF16_ROWS:, :]


def _inproj(x, g, w_in, layer, rope, *, tm, tiles_per_seq, out_dtype, with_tails):
    m = x.shape[0]
    nm = m // tm
    n_pieces = 8
    rows = _piece_rows(D_MODEL, n_pieces)
    n_seq = nm // tiles_per_seq
    tile = lambda i: jnp.maximum(i - n_pieces, 0)
    row_spec = lambda width: pl.BlockSpec((tm, width), lambda i: (tile(i), 0))
    rope_spec = pl.BlockSpec((tm, LANES), lambda i: (tile(i) % tiles_per_seq, 0))
    out_shape = [jax.ShapeDtypeStruct((m, ATTN_WIDTH), out_dtype),
                 jax.ShapeDtypeStruct((m, KV_WIDTH), out_dtype),
                 jax.ShapeDtypeStruct((m, KV_WIDTH), out_dtype),
                 jax.ShapeDtypeStruct((m, POOL_WIDTH), out_dtype)]
    out_specs = [row_spec(ATTN_WIDTH), row_spec(KV_WIDTH), row_spec(KV_WIDTH), row_spec(POOL_WIDTH)]
    if with_tails:
        out_shape += [jax.ShapeDtypeStruct((n_seq * WINDOW, 2 * KV_WIDTH), F32),
                      jax.ShapeDtypeStruct((n_seq * BF16_ROWS, POOL_WIDTH), F32)]
        out_specs += [pl.BlockSpec((WINDOW, 2 * KV_WIDTH), lambda i: (tile(i) // tiles_per_seq, 0)),
                      pl.BlockSpec((BF16_ROWS, POOL_WIDTH), lambda i: (tile(i) // tiles_per_seq, 0))]
    return pl.pallas_call(
        functools.partial(_inproj_kernel, n_pieces=n_pieces, with_tails=with_tails),
        grid=(n_pieces + nm,),
        in_specs=[row_spec(D_MODEL),
                  pl.BlockSpec((1, D_MODEL), lambda i: (0, 0)),
                  pl.BlockSpec((None, rows, IN_WIDTH), lambda i: (layer, jnp.minimum(i, n_pieces - 1), 0)),
                  rope_spec, rope_spec, rope_spec],
        out_specs=out_specs,
        out_shape=out_shape,
        scratch_shapes=[pltpu.VMEM((D_MODEL, IN_WIDTH), BF16)],
        compiler_params=_params(),
        name="inproj",
    )(x, g, w_in, *rope)


def _sink_softmax(s, sink):
    mx = jnp.maximum(jnp.max(s, axis=1, keepdims=True), sink)
    p = jnp.exp(s - mx)
    den = jnp.sum(p, axis=1, keepdims=True) + jnp.exp(sink - mx)
    return p / den


def _pool_group(d, wpb_g, scale_g):
    return jnp.dot(d.astype(BF16), wpb_g, preferred_element_type=F32) * scale_g


def _mixer_kernel(sink_ref, q_ref, kc_ref, kp_ref, vc_ref, vp_ref, uc_ref, up_ref, wp_ref, sc_ref,
                  o_ref, wpb, ext):
    b = pl.program_id(0)
    n = pl.program_id(1)

    @pl.when((b == 0) & (n == 0))
    def _():
        wpb[...] = wp_ref[...].astype(BF16)

    blk = WINDOW
    half = LANES // 2
    kk = jnp.concatenate([kp_ref[...], kc_ref[...]], axis=0).astype(F32)
    vv = jnp.concatenate([vp_ref[...], vc_ref[...]], axis=0).astype(F32)
    lane = lax.broadcasted_iota(jnp.int32, (2 * blk, LANES), 1)
    low = lane < half
    row = lax.broadcasted_iota(jnp.int32, (blk, 4 * blk), 0)
    key = lax.broadcasted_iota(jnp.int32, (blk, 4 * blk), 1) & (2 * blk - 1)
    valid = (key > row) & (key <= row + blk) & ((n > 0) | (key >= blk))

    for h in range(N_KV_HEADS):
        g0 = (h // 2) * LANES
        kg, vg = kk[:, g0:g0 + LANES], vv[:, g0:g0 + LANES]
        if h % 2 == 0:
            klo, vlo = jnp.where(low, kg, 0.0), jnp.where(low, vg, 0.0)
            khi, vhi = pltpu.roll(klo, half, 1), pltpu.roll(vlo, half, 1)
        else:
            khi, vhi = jnp.where(low, 0.0, kg), jnp.where(low, 0.0, vg)
            klo, vlo = pltpu.roll(khi, half, 1), pltpu.roll(vhi, half, 1)
        kcat = jnp.concatenate([klo, khi], axis=0).astype(BF16)
        vcat = jnp.concatenate([vlo, vhi], axis=0).astype(BF16)
        for pair in range(h * GROUP // 2, (h + 1) * GROUP // 2):
            qp = q_ref[:, pair * LANES:(pair + 1) * LANES]
            s = lax.dot_general(qp, kcat, (((1,), (1,)), ((), ())), preferred_element_type=F32)
            s = jnp.where(valid, s, NEG_INF)
            ps = [_sink_softmax(s[:, t * 2 * blk:(t + 1) * 2 * blk], sink_ref[2 * pair + t]).astype(BF16)
                  for t in range(2)]
            o = jnp.dot(jnp.concatenate(ps, axis=1), vcat, preferred_element_type=F32)
            o_ref[:, pair * LANES:(pair + 1) * LANES] = o.astype(o_ref.dtype)

    pad = up_ref.shape[0]
    ext[0:pad, :] = jnp.where(n > 0, up_ref[...].astype(F32), 0.0)
    ext[pad:pad + blk, :] = uc_ref[...].astype(F32)
    pos = (n * blk + lax.broadcasted_iota(jnp.int32, (blk, 1), 0)).astype(F32)
    for g, w in enumerate(POOL_WINDOWS):
        c0, c1 = g * POOL_GROUP_DIM, (g + 1) * POOL_GROUP_DIM
        tok = ext[pad:pad + blk, c0:c1]
        wsum = tok
        for j in range(1, w):
            wsum = wsum + ext[pad - j:pad - j + blk, c0:c1]
        d = wsum / jnp.minimum(float(w), pos + 1.0) - tok
        y = _pool_group(d, wpb[g], sc_ref[:, c0:c1])
        o_ref[:, ATTN_WIDTH + c0:ATTN_WIDTH + c1] = y.astype(o_ref.dtype)


def _mixer(q, k, v, u, sinks, w_pool, scale, layer):
    nb = SEQ // WINDOW
    pad = BF16_ROWS
    cur = lambda b, n: (b * nb + n, 0)
    prev = lambda b, n: (b * nb + jnp.maximum(n - 1, 0), 0)
    prev_u = lambda b, n: (b * nb * (WINDOW // pad) + jnp.maximum(n * (WINDOW // pad) - 1, 0), 0)
    return pl.pallas_call(
        _mixer_kernel,
        grid=(BATCH, nb),
        in_specs=[pl.BlockSpec(memory_space=pltpu.SMEM),
                  pl.BlockSpec((WINDOW, ATTN_WIDTH), cur),
                  pl.BlockSpec((WINDOW, KV_WIDTH), cur), pl.BlockSpec((WINDOW, KV_WIDTH), prev),
                  pl.BlockSpec((WINDOW, KV_WIDTH), cur), pl.BlockSpec((WINDOW, KV_WIDTH), prev),
                  pl.BlockSpec((WINDOW, POOL_WIDTH), cur), pl.BlockSpec((pad, POOL_WIDTH), prev_u),
                  pl.BlockSpec((None, len(POOL_WINDOWS), POOL_GROUP_DIM, POOL_GROUP_DIM),
                               lambda b, n: (layer, 0, 0, 0)),
                  pl.BlockSpec((1, POOL_WIDTH), lambda b, n: (0, 0))],
        out_specs=pl.BlockSpec((WINDOW, D_MODEL), cur),
        out_shape=jax.ShapeDtypeStruct((BATCH * SEQ, D_MODEL), BF16),
        scratch_shapes=[pltpu.VMEM((len(POOL_WINDOWS), POOL_GROUP_DIM, POOL_GROUP_DIM), BF16),
                        pltpu.VMEM((pad + WINDOW, POOL_WIDTH), F32)],
        compiler_params=_params(2),
        name="mixer",
    )(sinks, q, k, k, v, v, u, u, w_pool, scale)


def _sample_mixer_kernel(sink_ref, q_ref, kn_ref, vn_ref, un_ref, ck_ref, cv_ref, st_ref, wp_ref, sc_ref,
                         a_ref, nk_ref, nv_ref, nu_ref, po_ref, dscr):
    b = pl.program_id(0)
    nk_ref[0:WINDOW - 1, :] = ck_ref[1:WINDOW, :]
    nk_ref[WINDOW - 1:WINDOW, :] = kn_ref[...]
    nv_ref[0:WINDOW - 1, :] = cv_ref[1:WINDOW, :]
    nv_ref[WINDOW - 1:WINDOW, :] = vn_ref[...]
    keys = nk_ref[...].astype(BF16)
    vals = nv_ref[...].astype(BF16)

    q = q_ref[...]
    qe = jnp.concatenate([q] * N_KV_HEADS, axis=1)
    own = (lax.broadcasted_iota(jnp.int32, qe.shape, 0) // GROUP
           == lax.broadcasted_iota(jnp.int32, qe.shape, 1) // HEAD_DIM)
    qe = jnp.where(own, qe, 0.0).astype(BF16)
    s = lax.dot_general(qe, keys, (((1,), (1,)), ((), ())), preferred_element_type=F32)
    p = _sink_softmax(s, sink_ref[...]).astype(BF16)
    r = jnp.dot(p, vals, preferred_element_type=F32)
    r = jnp.where(own, r, 0.0)
    o = r[:, 0:HEAD_DIM]
    for h in range(1, N_KV_HEADS):
        o = o + r[:, h * HEAD_DIM:(h + 1) * HEAD_DIM]
    a_ref[...] = o

    st = st_ref[...]
    un = un_ref[...]
    nu_ref[0:POOL_STATE - 1, :] = st[1:POOL_STATE, :]
    nu_ref[POOL_STATE - 1:POOL_STATE, :] = un
    for g, w in enumerate(POOL_WINDOWS):
        c0, c1 = g * POOL_GROUP_DIM, (g + 1) * POOL_GROUP_DIM
        tok = un[:, c0:c1]
        wsum = tok + jnp.sum(st[POOL_STATE - (w - 1):POOL_STATE, c0:c1], axis=0, keepdims=True)
        dscr[pl.ds(b, 1), c0:c1] = wsum / float(min(w, PAST_LEN + 1)) - tok

    @pl.when(b == pl.num_programs(0) - 1)
    def _():
        for g in range(len(POOL_WINDOWS)):
            c0, c1 = g * POOL_GROUP_DIM, (g + 1) * POOL_GROUP_DIM
            po_ref[:, c0:c1] = _pool_group(dscr[:, c0:c1], wp_ref[g].astype(BF16), sc_ref[:, c0:c1])


def _sample_mixer(q, k, v, u, cache_k, cache_v, state, sinks, w_pool, scale, layer):
    nb = DEC_BATCH
    per_b = lambda *shape: pl.BlockSpec((None,) + shape, lambda b: (b,) + (0,) * len(shape))
    per_lb = lambda *shape: pl.BlockSpec((None, None) + shape, lambda b: (layer, b) + (0,) * len(shape))
    return pl.pallas_call(
        _sample_mixer_kernel,
        grid=(nb,),
        in_specs=[pl.BlockSpec((N_HEADS, 1), lambda b: (0, 0)),
                  per_b(N_HEADS, HEAD_DIM), per_b(1, KV_WIDTH), per_b(1, KV_WIDTH), per_b(1, POOL_WIDTH),
                  per_lb(WINDOW, KV_WIDTH), per_lb(WINDOW, KV_WIDTH), per_lb(POOL_STATE, POOL_WIDTH),
                  pl.BlockSpec((None, len(POOL_WINDOWS), POOL_GROUP_DIM, POOL_GROUP_DIM),
                               lambda b: (layer, 0, 0, 0)),
                  pl.BlockSpec((1, POOL_WIDTH), lambda b: (0, 0))],
        out_specs=[per_b(N_HEADS, HEAD_DIM), per_b(WINDOW, KV_WIDTH), per_b(WINDOW, KV_WIDTH),
                   per_b(POOL_STATE, POOL_WIDTH), pl.BlockSpec((nb, POOL_WIDTH), lambda b: (0, 0))],
        out_shape=[jax.ShapeDtypeStruct((nb, N_HEADS, HEAD_DIM), F32),
                   jax.ShapeDtypeStruct((nb, WINDOW, KV_WIDTH), F32),
                   jax.ShapeDtypeStruct((nb, WINDOW, KV_WIDTH), F32),
                   jax.ShapeDtypeStruct((nb, POOL_STATE, POOL_WIDTH), F32),
                   jax.ShapeDtypeStruct((nb, POOL_WIDTH), F32)],
        scratch_shapes=[pltpu.VMEM((nb, POOL_WIDTH), F32)],
        compiler_params=_params(),
        name="sample_mixer",
    )(sinks.reshape(N_HEADS, 1), q.reshape(nb, N_HEADS, HEAD_DIM), k.reshape(nb, 1, KV_WIDTH),
      v.reshape(nb, 1, KV_WIDTH), u.reshape(nb, 1, POOL_WIDTH), cache_k, cache_v, state, w_pool, scale)


def _top2_gates(logits):
    lane = lax.broadcasted_iota(jnp.int32, logits.shape, 1)
    lg = jnp.where(lane < N_EXPERTS, logits, -jnp.inf)
    m1 = jnp.max(lg, axis=1, keepdims=True)
    i1 = jnp.min(jnp.where(lg == m1, lane, ROUTER_PAD), axis=1, keepdims=True)
    lg2 = jnp.where(lane == i1, -jnp.inf, lg)
    m2 = jnp.max(lg2, axis=1, keepdims=True)
    i2 = jnp.min(jnp.where(lg2 == m2, lane, ROUTER_PAD), axis=1, keepdims=True)
    e2 = jnp.exp(m2 - m1)
    den = 1.0 + e2
    return jnp.where(lane == i1, 1.0 / den, 0.0) + jnp.where(lane == i2, e2 / den, 0.0)


def _outproj_kernel(mix_ref, x_ref, w_ref, g_ref, *rest, n_pieces, with_router):
    if with_router:
        wr_ref, br_ref, xo_ref, hn_ref, cmb_ref, wbf = rest
    else:
        xo_ref, hn_ref, wbf = rest
    step = pl.program_id(0)
    _stash_piece(step, w_ref, wbf, n_pieces)

    @pl.when(step >= n_pieces)
    def _():
        y = jnp.dot(mix_ref[...].astype(BF16), wbf[...], preferred_element_type=F32)
        xo = x_ref[...] + y
        xo_ref[...] = xo
        hn = _rms(xo, g_ref[...]).astype(BF16)
        hn_ref[...] = hn
        if with_router:
            logits = jnp.dot(hn, wr_ref[...].astype(BF16), preferred_element_type=F32) + br_ref[...]
            cmb_ref[...] = _top2_gates(logits)


def _outproj(mix, x, w_out, g, layer, router, *, tm):
    m = x.shape[0]
    nm = m // tm
    n_pieces = 8
    rows = _piece_rows(D_MODEL, n_pieces)
    tile = lambda i: jnp.maximum(i - n_pieces, 0)
    row_spec = lambda width: pl.BlockSpec((tm, width), lambda i: (tile(i), 0))
    const = lambda r, c: pl.BlockSpec((r, c), lambda i: (0, 0))
    in_specs = [row_spec(D_MODEL), row_spec(D_MODEL),
                pl.BlockSpec((None, rows, D_MODEL), lambda i: (layer, jnp.minimum(i, n_pieces - 1), 0)),
                const(1, D_MODEL)]
    args = [mix, x, w_out, g]
    out_shape = [jax.ShapeDtypeStruct((m, D_MODEL), F32), jax.ShapeDtypeStruct((m, D_MODEL), BF16)]
    out_specs = [row_spec(D_MODEL), row_spec(D_MODEL)]
    if router is not None:
        in_specs += [const(D_MODEL, ROUTER_PAD), const(1, ROUTER_PAD)]
        args += list(router)
        out_shape.append(jax.ShapeDtypeStruct((m, ROUTER_PAD), F32))
        out_specs.append(row_spec(ROUTER_PAD))
    return pl.pallas_call(
        functools.partial(_outproj_kernel, n_pieces=n_pieces, with_router=router is not None),
        grid=(n_pieces + nm,),
        in_specs=in_specs,
        out_specs=out_specs,
        out_shape=out_shape,
        scratch_shapes=[pltpu.VMEM((D_MODEL, D_MODEL), BF16)],
        compiler_params=_params(),
        name="outproj",
    )(*args)


def _ffn_kernel(hn_ref, acc_ref, *rest, pieces, expert, final_norm):
    rest = list(rest)
    cmb_ref = rest.pop(0) if expert is not None else None
    gf_ref = rest.pop(0) if final_norm else None
    wg_ref, wu_ref, wd_ref, o_ref, wgb, wub, wdb = rest
    step = pl.program_id(0)
    _stash_piece(step, wg_ref, wgb, pieces[0])
    _stash_piece(step, wu_ref, wub, pieces[0])
    _stash_piece(step, wd_ref, wdb, pieces[1])

    @pl.when(step >= max(pieces))
    def _():
        hn = hn_ref[...]
        gate = jnp.dot(hn, wgb[...], preferred_element_type=F32)
        up = jnp.dot(hn, wub[...], preferred_element_type=F32)
        h = (gate * (1.0 / (1.0 + jnp.exp(-gate)))) * up
        if expert is not None:
            h = h * cmb_ref[:, expert:expert + 1]
        out = acc_ref[...] + jnp.dot(h.astype(BF16), wdb[...], preferred_element_type=F32)
        if final_norm:
            out = _rms(out, gf_ref[...])
        o_ref[...] = out


def _ffn_slab(hn, acc, wg, wu, wd, lead, slab, width, *, tm, cmb=None, expert=None, g_final=None):
    m = hn.shape[0]
    nm = m // tm
    pieces = (16, width // LANES)
    n_pro = max(pieces)
    rows_in = _piece_rows(D_MODEL, pieces[0])
    rows_dn = _piece_rows(width, pieces[1])
    tile = lambda i: jnp.maximum(i - n_pro, 0)
    row_spec = lambda w_: pl.BlockSpec((tm, w_), lambda i: (tile(i), 0))
    nlead = (None,) * len(lead)
    in_specs = [row_spec(D_MODEL), row_spec(D_MODEL)]
    args = [hn, acc]
    if expert is not None:
        in_specs.append(row_spec(ROUTER_PAD))
        args.append(cmb)
    if g_final is not None:
        in_specs.append(pl.BlockSpec((1, D_MODEL), lambda i: (0, 0)))
        args.append(g_final)
    up_spec = pl.BlockSpec(nlead + (rows_in, width), lambda i: lead + (jnp.minimum(i, pieces[0] - 1), slab))
    in_specs += [up_spec, up_spec,
                 pl.BlockSpec(nlead + (rows_dn, D_MODEL),
                              lambda i: lead + (slab * pieces[1] + jnp.minimum(i, pieces[1] - 1), 0))]
    args += [wg, wu, wd]
    return pl.pallas_call(
        functools.partial(_ffn_kernel, pieces=pieces, expert=expert, final_norm=g_final is not None),
        grid=(n_pro + nm,),
        in_specs=in_specs,
        out_specs=row_spec(D_MODEL),
        out_shape=jax.ShapeDtypeStruct((m, D_MODEL), F32),
        scratch_shapes=[pltpu.VMEM((D_MODEL, width), BF16), pltpu.VMEM((D_MODEL, width), BF16),
                        pltpu.VMEM((width, D_MODEL), BF16)],
        compiler_params=_params(),
        name="ffn_slab",
    )(*args)


def _rope_tables(pos):
    half = ROPE_DIM // 2
    inv = ROPE_THETA ** (-jnp.arange(0, ROPE_DIM, 2, dtype=F32) / ROPE_DIM)
    ang = pos[:, None] * inv[None, :]
    cos, sin = jnp.cos(ang), jnp.sin(ang)
    n = pos.shape[0]
    rest = HEAD_DIM - ROPE_DIM
    a = jnp.concatenate([cos, cos, jnp.ones((n, rest), F32)], axis=1)
    b = jnp.concatenate([jnp.zeros((n, half), F32), sin, jnp.zeros((n, rest), F32)], axis=1)
    c = jnp.concatenate([-sin, jnp.zeros((n, half + rest), F32)], axis=1)
    reps = LANES // HEAD_DIM
    scale = HEAD_DIM ** -0.5
    return tuple(jnp.tile(t, (1, reps)) * scale for t in (a, b, c))


def kernel(x_prompt, x_sample, cache_k, cache_v, state_pool, norm_mix, w_in, attn_sinks, w_pool, pool_scale,
           w_out, norm_ffn, w_gate_dense, w_up_dense, w_down_dense, w_router, b_router, w_gate_exp, w_up_exp,
           w_down_exp, norm_final):
    tm_p = 512
    tm_s = DEC_BATCH
    xp = x_prompt.reshape(BATCH * SEQ, D_MODEL)
    xs = x_sample.reshape(DEC_BATCH, D_MODEL)
    rope_p = _rope_tables(jnp.arange(SEQ, dtype=F32))
    rope_s = _rope_tables(jnp.full((tm_s,), PAST_LEN, dtype=F32))
    ck = cache_k.reshape(DEPTH, DEC_BATCH, WINDOW, KV_WIDTH)
    cv = cache_v.reshape(DEPTH, DEC_BATCH, WINDOW, KV_WIDTH)
    g_final = norm_final.reshape(1, D_MODEL)
    slab_w = D_FF // DENSE_SLABS

    nk_p, nv_p, nu_p, nk_s, nv_s, nu_s = [], [], [], [], [], []
    for l in range(DEPTH):
        g_mix = norm_mix[l].reshape(1, D_MODEL)
        g_ffn = norm_ffn[l].reshape(1, D_MODEL)
        scale = pool_scale[l].reshape(1, POOL_WIDTH)
        moe = l % 2 == 1
        i = l // 2
        router = None
        if moe:
            router = (jnp.pad(w_router[i], ((0, 0), (0, ROUTER_PAD - N_EXPERTS))),
                      jnp.pad(b_router[i], (0, ROUTER_PAD - N_EXPERTS)).reshape(1, ROUTER_PAD))

        q, k, v, u, kvt, ut = _inproj(xp, g_mix, w_in, l, rope_p, tm=tm_p, tiles_per_seq=SEQ // tm_p,
                                      out_dtype=BF16, with_tails=True)
        mix = _mixer(q, k, v, u, attn_sinks[l], w_pool, scale, l)
        outs = _outproj(mix, xp, w_out, g_ffn, l, router, tm=tm_p)
        nk_p.append(kvt[:, :KV_WIDTH].reshape(BATCH, WINDOW, N_KV_HEADS, HEAD_DIM))
        nv_p.append(kvt[:, KV_WIDTH:].reshape(BATCH, WINDOW, N_KV_HEADS, HEAD_DIM))
        nu_p.append(ut.reshape(BATCH, BF16_ROWS, POOL_WIDTH)[:, BF16_ROWS - POOL_STATE:])

        qs, ks, vs, us = _inproj(xs, g_mix, w_in, l, rope_s, tm=tm_s, tiles_per_seq=1,
                                 out_dtype=F32, with_tails=False)
        a_s, nk, nv, nu, ps = _sample_mixer(qs, ks, vs, us, ck, cv, state_pool, attn_sinks[l], w_pool, scale, l)
        mix_s = jnp.concatenate([a_s.reshape(DEC_BATCH, ATTN_WIDTH), ps], axis=1)
        outs_s = _outproj(mix_s, xs, w_out, g_ffn, l, router, tm=tm_s)
        nk_s.append(nk.reshape(DEC_BATCH, WINDOW, N_KV_HEADS, HEAD_DIM))
        nv_s.append(nv.reshape(DEC_BATCH, WINDOW, N_KV_HEADS, HEAD_DIM))
        nu_s.append(nu)

        last = l == DEPTH - 1
        for (xo, hn, *cmb), tm in ((outs, tm_p), (outs_s, tm_s)):
            acc = xo
            if moe:
                for e in range(N_EXPERTS):
                    acc = _ffn_slab(hn, acc, w_gate_exp, w_up_exp, w_down_exp, (i, e), 0, D_FF_EXPERT, tm=tm,
                                    cmb=cmb[0], expert=e,
                                    g_final=g_final if last and e == N_EXPERTS - 1 else None)
            else:
                for s in range(DENSE_SLABS):
                    acc = _ffn_slab(hn, acc, w_gate_dense, w_up_dense, w_down_dense, (i,), s, slab_w, tm=tm,
                                    g_final=g_final if last and s == DENSE_SLABS - 1 else None)
            if tm == tm_p:
                xp = acc
            else:
                xs = acc

    y_prompt = xp.reshape(BATCH, SEQ, D_MODEL)
    y_sample = xs.reshape(DEC_BATCH, 1, D_MODEL)
    return (y_prompt, y_sample, jnp.stack(nk_p), jnp.stack(nv_p), jnp.stack(nu_p),
            jnp.stack(nk_s), jnp.stack(nv_s), jnp.stack(nu_s))
```

```python
import functools

import jax
import jax.numpy as jnp
from jax import lax
from jax.experimental import pallas as pl
from jax.experimental.pallas import tpu as pltpu

F32 = jnp.float32
BF16 = jnp.bfloat16

D_MODEL = 2048
BATCH = 2
SEQ = 4096
DEPTH = 4
DEC_BATCH = 32
PAST_LEN = 16384
N_HEADS = 16
N_KV_HEADS = 4
HEAD_DIM = 64
GROUP = N_HEADS // N_KV_HEADS
ATTN_WIDTH = N_HEADS * HEAD_DIM
KV_WIDTH = N_KV_HEADS * HEAD_DIM
WINDOW = 128
ROPE_DIM = HEAD_DIM // 4
ROPE_THETA = 500000.0
POOL_WINDOWS = (2, 4, 8, 16)
POOL_WIDTH = D_MODEL - ATTN_WIDTH
POOL_GROUP_DIM = POOL_WIDTH // len(POOL_WINDOWS)
POOL_STATE = max(POOL_WINDOWS) - 1
IN_WIDTH = ATTN_WIDTH + 2 * KV_WIDTH + POOL_WIDTH
D_FF = 5632
N_EXPERTS = 8
D_FF_EXPERT = D_MODEL // 2
EPS = 1e-5
NEG_INF = -1e30

LANES = 128
BF16_ROWS = 16
VMEM_LIMIT = 56 * 1024 * 1024
DENSE_SLABS = 4
ROUTER_PAD = LANES


def _params(n_axes=1):
    return pltpu.CompilerParams(dimension_semantics=("arbitrary",) * n_axes,
                                vmem_limit_bytes=VMEM_LIMIT)


def _rms(x, g):
    ms = jnp.mean(x * x, axis=-1, keepdims=True)
    return (x * lax.rsqrt(ms + EPS)) * g


def _rope_mix(zc, ra, rb, rc):
    return zc * ra + pltpu.roll(zc, ROPE_DIM // 2, 1) * rb + pltpu.roll(zc, LANES - ROPE_DIM // 2, 1) * rc


def _split_bf16(a):
    hi = a.astype(BF16)
    return hi, (a - hi.astype(F32)).astype(BF16)


def _dot3(a, b, dims=(((1,), (0,)), ((), ()))):
    m = a.shape[0]
    ah, al = _split_bf16(a)
    bh, bl = _split_bf16(b)
    t = lax.dot_general(jnp.concatenate([ah, al], axis=0), bh, dims, preferred_element_type=F32)
    return t[:m] + t[m:] + lax.dot_general(ah, bl, dims, preferred_element_type=F32)


def _stash_piece(step, w_ref, wbf, n_pieces):
    rows = w_ref.shape[0]

    @pl.when(step < n_pieces)
    def _():
        r0 = pl.multiple_of(step * rows, rows)
        wbf[pl.ds(r0, rows), :] = w_ref[...].astype(BF16)


def _piece_rows(total_rows, n_pieces):
    rows, rem = divmod(total_rows, n_pieces)
    assert rem == 0 and rows % BF16_ROWS == 0, (total_rows, n_pieces)
    return rows


def _inproj_kernel(x_ref, g_ref, w_ref, ra_ref, rb_ref, rc_ref, q_ref, k_ref, v_ref, u_ref, kvt_ref, ut_ref,
                   wbf, *, n_pieces):
    step = pl.program_id(0)
    _stash_piece(step, w_ref, wbf, n_pieces)

    @pl.when(step >= n_pieces)
    def _():
        tm = x_ref.shape[0]
        h = _rms(x_ref[...], g_ref[...])
        z = jnp.dot(h.astype(BF16), wbf[...], preferred_element_type=F32)
        ra, rb, rc = ra_ref[...], rb_ref[...], rc_ref[...]
        n_rot = (ATTN_WIDTH + KV_WIDTH) // LANES
        rot = []
        for c in range(n_rot):
            r = _rope_mix(z[:, c * LANES:(c + 1) * LANES], ra, rb, rc)
            if c >= ATTN_WIDTH // LANES:
                r = r * (HEAD_DIM ** 0.5)
            rot.append(r)
        nq = ATTN_WIDTH // LANES
        for c in range(nq):
            q_ref[:, c * LANES:(c + 1) * LANES] = rot[c].astype(q_ref.dtype)
        for c in range(nq, n_rot):
            k_ref[:, (c - nq) * LANES:(c - nq + 1) * LANES] = rot[c].astype(k_ref.dtype)
        v = z[:, ATTN_WIDTH + KV_WIDTH:ATTN_WIDTH + 2 * KV_WIDTH]
        u = z[:, ATTN_WIDTH + 2 * KV_WIDTH:]
        v_ref[...] = v.astype(v_ref.dtype)
        u_ref[...] = u.astype(u_ref.dtype)
        for c in range(nq, n_rot):
            kvt_ref[:, (c - nq) * LANES:(c - nq + 1) * LANES] = rot[c][tm - WINDOW:, :]
        kvt_ref[:, KV_WIDTH:] = v[tm - WINDOW:, :]
        ut_ref[...] = u[tm - BF16_ROWS:, :]


def _inproj(x, g, w_in, layer, rope, *, tm):
    m = x.shape[0]
    nm = m // tm
    n_pieces = 8
    rows = _piece_rows(D_MODEL, n_pieces)
    tiles_per_seq = SEQ // tm
    tile = lambda i: jnp.maximum(i - n_pieces, 0)
    row_spec = lambda width: pl.BlockSpec((tm, width), lambda i: (tile(i), 0))
    rope_spec = pl.BlockSpec((tm, LANES), lambda i: (tile(i) % tiles_per_seq, 0))
    out_shape = [jax.ShapeDtypeStruct((m, ATTN_WIDTH), BF16),
                 jax.ShapeDtypeStruct((m, KV_WIDTH), BF16),
                 jax.ShapeDtypeStruct((m, KV_WIDTH), BF16),
                 jax.ShapeDtypeStruct((m, POOL_WIDTH), BF16),
                 jax.ShapeDtypeStruct((BATCH * WINDOW, 2 * KV_WIDTH), F32),
                 jax.ShapeDtypeStruct((BATCH * BF16_ROWS, POOL_WIDTH), F32)]
    out_specs = [row_spec(ATTN_WIDTH), row_spec(KV_WIDTH), row_spec(KV_WIDTH), row_spec(POOL_WIDTH),
                 pl.BlockSpec((WINDOW, 2 * KV_WIDTH), lambda i: (tile(i) // tiles_per_seq, 0)),
                 pl.BlockSpec((BF16_ROWS, POOL_WIDTH), lambda i: (tile(i) // tiles_per_seq, 0))]
    return pl.pallas_call(
        functools.partial(_inproj_kernel, n_pieces=n_pieces),
        grid=(n_pieces + nm,),
        in_specs=[row_spec(D_MODEL),
                  pl.BlockSpec((1, D_MODEL), lambda i: (0, 0)),
                  pl.BlockSpec((None, rows, IN_WIDTH), lambda i: (layer, jnp.minimum(i, n_pieces - 1), 0)),
                  rope_spec, rope_spec, rope_spec],
        out_specs=out_specs,
        out_shape=out_shape,
        scratch_shapes=[pltpu.VMEM((D_MODEL, IN_WIDTH), BF16)],
        compiler_params=_params(),
        name="inproj",
    )(x, g, w_in, *rope)


def _sink_softmax(s, sink, axis):
    mx = jnp.maximum(jnp.max(s, axis=axis, keepdims=True), sink)
    p = jnp.exp(s - mx)
    den = jnp.sum(p, axis=axis, keepdims=True) + jnp.exp(sink - mx)
    return p * (1.0 / den)


def _pool_group(d, wpb_g, scale_g):
    return jnp.dot(d.astype(BF16), wpb_g, preferred_element_type=F32) * scale_g


def _mixer_kernel(sink_ref, q_ref, kc_ref, kp_ref, vc_ref, vp_ref, uc_ref, up_ref, wp_ref, sc_ref,
                  o_ref, wpb, band):
    b = pl.program_id(0)
    n = pl.program_id(1)
    blk = WINDOW

    @pl.when((b == 0) & (n == 0))
    def _():
        wpb[...] = wp_ref[...].astype(BF16)
        tok = lax.broadcasted_iota(jnp.int32, (blk, 2 * blk), 0) + blk
        src = lax.broadcasted_iota(jnp.int32, (blk, 2 * blk), 1)
        for g, w in enumerate(POOL_WINDOWS):
            band[g] = jnp.where((src <= tok) & (src > tok - w), 1.0, 0.0).astype(BF16)

    half = LANES // 2
    kk = jnp.concatenate([kp_ref[...], kc_ref[...]], axis=0).astype(F32)
    vv = jnp.concatenate([vp_ref[...], vc_ref[...]], axis=0).astype(F32)
    lane = lax.broadcasted_iota(jnp.int32, (2 * blk, LANES), 1)
    low = lane < half
    key = lax.broadcasted_iota(jnp.int32, (4 * blk, blk), 0) & (2 * blk - 1)
    qry = lax.broadcasted_iota(jnp.int32, (4 * blk, blk), 1)
    valid = (key > qry) & (key <= qry + blk) & ((n > 0) | (key >= blk))

    for h in range(N_KV_HEADS):
        g0 = (h // 2) * LANES
        kg, vg = kk[:, g0:g0 + LANES], vv[:, g0:g0 + LANES]
        if h % 2 == 0:
            klo, vlo = jnp.where(low, kg, 0.0), jnp.where(low, vg, 0.0)
            khi, vhi = pltpu.roll(klo, half, 1), pltpu.roll(vlo, half, 1)
        else:
            khi, vhi = jnp.where(low, 0.0, kg), jnp.where(low, 0.0, vg)
            klo, vlo = pltpu.roll(khi, half, 1), pltpu.roll(vhi, half, 1)
        kcat = jnp.concatenate([klo, khi], axis=0).astype(BF16)
        vcat = jnp.concatenate([vlo, vhi], axis=0).astype(BF16)
        for pair in range(h * GROUP // 2, (h + 1) * GROUP // 2):
            qp = q_ref[:, pair * LANES:(pair + 1) * LANES]
            st = lax.dot_general(kcat, qp, (((1,), (1,)), ((), ())), preferred_element_type=F32)
            st = jnp.where(valid, st, NEG_INF)
            pt = [_sink_softmax(st[t * 2 * blk:(t + 1) * 2 * blk], sink_ref[2 * pair + t], 0).astype(BF16)
                  for t in range(2)]
            o = lax.dot_general(jnp.concatenate(pt, axis=0), vcat, (((0,), (0,)), ((), ())),
                                preferred_element_type=F32)
            o_ref[:, pair * LANES:(pair + 1) * LANES] = o.astype(o_ref.dtype)

    ext = jnp.concatenate([jnp.where(n > 0, up_ref[...], jnp.zeros_like(up_ref)), uc_ref[...]], axis=0)
    pos = (n * blk + lax.broadcasted_iota(jnp.int32, (blk, 1), 0)).astype(F32)
    for g, w in enumerate(POOL_WINDOWS):
        c0, c1 = g * POOL_GROUP_DIM, (g + 1) * POOL_GROUP_DIM
        wsum = jnp.dot(band[g], ext[:, c0:c1], preferred_element_type=F32)
        d = wsum * (1.0 / jnp.minimum(float(w), pos + 1.0)) - uc_ref[:, c0:c1].astype(F32)
        y = _pool_group(d, wpb[g], sc_ref[:, c0:c1])
        o_ref[:, ATTN_WIDTH + c0:ATTN_WIDTH + c1] = y.astype(o_ref.dtype)


def _mixer(q, k, v, u, sinks, w_pool, scale, layer):
    assert u.dtype == BF16
    nb = SEQ // WINDOW
    cur = lambda b, n: (b * nb + n, 0)
    prev = lambda b, n: (b * nb + jnp.maximum(n - 1, 0), 0)
    n_grp = len(POOL_WINDOWS)
    return pl.pallas_call(
        _mixer_kernel,
        grid=(BATCH, nb),
        in_specs=[pl.BlockSpec(memory_space=pltpu.SMEM),
                  pl.BlockSpec((WINDOW, ATTN_WIDTH), cur),
                  pl.BlockSpec((WINDOW, KV_WIDTH), cur), pl.BlockSpec((WINDOW, KV_WIDTH), prev),
                  pl.BlockSpec((WINDOW, KV_WIDTH), cur), pl.BlockSpec((WINDOW, KV_WIDTH), prev),
                  pl.BlockSpec((WINDOW, POOL_WIDTH), cur), pl.BlockSpec((WINDOW, POOL_WIDTH), prev),
                  pl.BlockSpec((None, n_grp, POOL_GROUP_DIM, POOL_GROUP_DIM), lambda b, n: (layer, 0, 0, 0)),
                  pl.BlockSpec((1, POOL_WIDTH), lambda b, n: (0, 0))],
        out_specs=pl.BlockSpec((WINDOW, D_MODEL), cur),
        out_shape=jax.ShapeDtypeStruct((BATCH * SEQ, D_MODEL), BF16),
        scratch_shapes=[pltpu.VMEM((n_grp, POOL_GROUP_DIM, POOL_GROUP_DIM), BF16),
                        pltpu.VMEM((n_grp, WINDOW, 2 * WINDOW), BF16)],
        compiler_params=_params(2),
        name="mixer",
    )(sinks, q, k, k, v, v, u, u, w_pool, scale)


def _sample_mixer_kernel(sink_ref, q_ref, kn_ref, vn_ref, un_ref, ck_ref, cv_ref, st_ref, wp_ref, sc_ref,
                         a_ref, nk_ref, nv_ref, nu_ref, po_ref, dscr):
    b = pl.program_id(0)
    nk_ref[0:WINDOW - 1, :] = ck_ref[1:WINDOW, :]
    nk_ref[WINDOW - 1:WINDOW, :] = kn_ref[...]
    nv_ref[0:WINDOW - 1, :] = cv_ref[1:WINDOW, :]
    nv_ref[WINDOW - 1:WINDOW, :] = vn_ref[...]
    keys = nk_ref[...]
    vals = nv_ref[...]

    q = q_ref[...]
    qe = jnp.concatenate([q] * N_KV_HEADS, axis=1)
    own = (lax.broadcasted_iota(jnp.int32, qe.shape, 0) // GROUP
           == lax.broadcasted_iota(jnp.int32, qe.shape, 1) // HEAD_DIM)
    qe = jnp.where(own, qe, 0.0)
    s = _dot3(qe, keys, (((1,), (1,)), ((), ())))
    r = _dot3(_sink_softmax(s, sink_ref[...], 1), vals)
    r = jnp.where(own, r, 0.0)
    o = r[:, 0:HEAD_DIM]
    for h in range(1, N_KV_HEADS):
        o = o + r[:, h * HEAD_DIM:(h + 1) * HEAD_DIM]
    a_ref[...] = o

    st = st_ref[...]
    un = un_ref[...]
    nu_ref[0:POOL_STATE - 1, :] = st[1:POOL_STATE, :]
    nu_ref[POOL_STATE - 1:POOL_STATE, :] = un
    for g, w in enumerate(POOL_WINDOWS):
        c0, c1 = g * POOL_GROUP_DIM, (g + 1) * POOL_GROUP_DIM
        tok = un[:, c0:c1]
        wsum = tok + jnp.sum(st[POOL_STATE - (w - 1):POOL_STATE, c0:c1], axis=0, keepdims=True)
        dscr[pl.ds(b, 1), c0:c1] = wsum / float(min(w, PAST_LEN + 1)) - tok

    @pl.when(b == pl.num_programs(0) - 1)
    def _():
        for g in range(len(POOL_WINDOWS)):
            c0, c1 = g * POOL_GROUP_DIM, (g + 1) * POOL_GROUP_DIM
            po_ref[:, c0:c1] = _dot3(dscr[:, c0:c1], wp_ref[g]) * sc_ref[:, c0:c1]


def _sample_mixer(q, k, v, u, cache_k, cache_v, state, sinks, w_pool, scale, layer):
    nb = DEC_BATCH
    per_b = lambda *shape: pl.BlockSpec((None,) + shape, lambda b: (b,) + (0,) * len(shape))
    per_lb = lambda *shape: pl.BlockSpec((None, None) + shape, lambda b: (layer, b) + (0,) * len(shape))
    return pl.pallas_call(
        _sample_mixer_kernel,
        grid=(nb,),
        in_specs=[pl.BlockSpec((N_HEADS, 1), lambda b: (0, 0)),
                  per_b(N_HEADS, HEAD_DIM), per_b(1, KV_WIDTH), per_b(1, KV_WIDTH), per_b(1, POOL_WIDTH),
                  per_lb(WINDOW, KV_WIDTH), per_lb(WINDOW, KV_WIDTH), per_lb(POOL_STATE, POOL_WIDTH),
                  pl.BlockSpec((None, len(POOL_WINDOWS), POOL_GROUP_DIM, POOL_GROUP_DIM),
                               lambda b: (layer, 0, 0, 0)),
                  pl.BlockSpec((1, POOL_WIDTH), lambda b: (0, 0))],
        out_specs=[per_b(N_HEADS, HEAD_DIM), per_b(WINDOW, KV_WIDTH), per_b(WINDOW, KV_WIDTH),
                   per_b(POOL_STATE, POOL_WIDTH), pl.BlockSpec((nb, POOL_WIDTH), lambda b: (0, 0))],
        out_shape=[jax.ShapeDtypeStruct((nb, N_HEADS, HEAD_DIM), F32),
                   jax.ShapeDtypeStruct((nb, WINDOW, KV_WIDTH), F32),
                   jax.ShapeDtypeStruct((nb, WINDOW, KV_WIDTH), F32),
                   jax.ShapeDtypeStruct((nb, POOL_STATE, POOL_WIDTH), F32),
                   jax.ShapeDtypeStruct((nb, POOL_WIDTH), F32)],
        scratch_shapes=[pltpu.VMEM((nb, POOL_WIDTH), F32)],
        compiler_params=_params(),
        name="sample_mixer",
    )(sinks.reshape(N_HEADS, 1), q.reshape(nb, N_HEADS, HEAD_DIM), k.reshape(nb, 1, KV_WIDTH),
      v.reshape(nb, 1, KV_WIDTH), u.reshape(nb, 1, POOL_WIDTH), cache_k, cache_v, state, w_pool, scale)


SAMPLE_TN = 512


def _sample_inproj_kernel(x_ref, g_ref, w_ref, ra_ref, rb_ref, rc_ref, z_ref):
    j = pl.program_id(0)
    z = _dot3(_rms(x_ref[...], g_ref[...]), w_ref[...])
    ra, rb, rc = ra_ref[...], rb_ref[...], rc_ref[...]
    per_tile = SAMPLE_TN // LANES
    for c in range(per_tile):
        zc = z[:, c * LANES:(c + 1) * LANES]
        grp = j * per_tile + c
        r = _rope_mix(zc, ra, rb, rc)
        r = jnp.where(grp >= ATTN_WIDTH // LANES, r * (HEAD_DIM ** 0.5), r)
        z_ref[:, c * LANES:(c + 1) * LANES] = jnp.where(grp < (ATTN_WIDTH + KV_WIDTH) // LANES, r, zc)


def _sample_inproj(x, g, w_in, layer, rope):
    m = x.shape[0]
    const = lambda r, c: pl.BlockSpec((r, c), lambda j: (0, 0))
    return pl.pallas_call(
        _sample_inproj_kernel,
        grid=(IN_WIDTH // SAMPLE_TN,),
        in_specs=[const(m, D_MODEL), const(1, D_MODEL),
                  pl.BlockSpec((None, D_MODEL, SAMPLE_TN), lambda j: (layer, 0, j)),
                  const(m, LANES), const(m, LANES), const(m, LANES)],
        out_specs=pl.BlockSpec((m, SAMPLE_TN), lambda j: (0, j)),
        out_shape=jax.ShapeDtypeStruct((m, IN_WIDTH), F32),
        compiler_params=_params(),
        name="sample_inproj",
    )(x, g, w_in, *rope)


def _sample_outproj_kernel(mix_ref, x_ref, w_ref, o_ref):
    o_ref[...] = x_ref[...] + _dot3(mix_ref[...], w_ref[...])


def _sample_outproj(mix, x, w_out, layer):
    m = x.shape[0]
    col = pl.BlockSpec((m, SAMPLE_TN), lambda j: (0, j))
    return pl.pallas_call(
        _sample_outproj_kernel,
        grid=(D_MODEL // SAMPLE_TN,),
        in_specs=[pl.BlockSpec((m, D_MODEL), lambda j: (0, 0)), col,
                  pl.BlockSpec((None, D_MODEL, SAMPLE_TN), lambda j: (layer, 0, j))],
        out_specs=col,
        out_shape=jax.ShapeDtypeStruct((m, D_MODEL), F32),
        compiler_params=_params(),
        name="sample_outproj",
    )(mix, x, w_out)


def _sample_ffn_kernel(x_ref, g_ref, *rest, moe, final_norm):
    rest = list(rest)
    wr_ref, br_ref = (rest.pop(0), rest.pop(0)) if moe else (None, None)
    gf_ref = rest.pop(0) if final_norm else None
    wg_ref, wu_ref, wd_ref, o_ref, acc, hn_scr, cmb = rest
    e, f = pl.program_id(0), pl.program_id(1)

    @pl.when((e == 0) & (f == 0))
    def _():
        hn = _rms(x_ref[...], g_ref[...])
        hn_scr[...] = hn
        acc[...] = jnp.zeros_like(acc)
        if moe:
            cmb[...] = _top2_gates(_dot3(hn, wr_ref[...]) + br_ref[...])

    hn = hn_scr[...]
    gate = _dot3(hn, wg_ref[...])
    up = _dot3(hn, wu_ref[...])
    h = (gate * (1.0 / (1.0 + jnp.exp(-gate)))) * up
    if moe:
        lane = lax.broadcasted_iota(jnp.int32, cmb.shape, 1)
        h = h * jnp.sum(jnp.where(lane == e, cmb[...], 0.0), axis=1, keepdims=True)
    acc[...] += _dot3(h, wd_ref[...])

    @pl.when((e == pl.num_programs(0) - 1) & (f == pl.num_programs(1) - 1))
    def _():
        out = x_ref[...] + acc[...]
        if final_norm:
            out = _rms(out, gf_ref[...])
        o_ref[...] = out


def _sample_ffn(x, g, wg, wu, wd, idx, *, router=None, g_final=None):
    m = x.shape[0]
    moe = router is not None
    n_e = N_EXPERTS if moe else 1
    width = wg.shape[-1]
    const = lambda r, c: pl.BlockSpec((r, c), lambda e, f: (0, 0))
    in_specs = [const(m, D_MODEL), const(1, D_MODEL)]
    args = [x, g]
    if moe:
        in_specs += [const(D_MODEL, ROUTER_PAD), const(1, ROUTER_PAD)]
        args += list(router)
        up_spec = pl.BlockSpec((None, None, D_MODEL, SAMPLE_TN), lambda e, f: (idx, e, 0, f))
        dn_spec = pl.BlockSpec((None, None, SAMPLE_TN, D_MODEL), lambda e, f: (idx, e, f, 0))
    else:
        up_spec = pl.BlockSpec((None, D_MODEL, SAMPLE_TN), lambda e, f: (idx, 0, f))
        dn_spec = pl.BlockSpec((None, SAMPLE_TN, D_MODEL), lambda e, f: (idx, f, 0))
    if g_final is not None:
        in_specs.append(const(1, D_MODEL))
        args.append(g_final)
    return pl.pallas_call(
        functools.partial(_sample_ffn_kernel, moe=moe, final_norm=g_final is not None),
        grid=(n_e, width // SAMPLE_TN),
        in_specs=in_specs + [up_spec, up_spec, dn_spec],
        out_specs=const(m, D_MODEL),
        out_shape=jax.ShapeDtypeStruct((m, D_MODEL), F32),
        scratch_shapes=[pltpu.VMEM((m, D_MODEL), F32), pltpu.VMEM((m, D_MODEL), F32),
                        pltpu.VMEM((m, ROUTER_PAD), F32)],
        compiler_params=_params(2),
        name="sample_ffn",
    )(*args, wg, wu, wd)


def _top2_gates(logits):
    lane = lax.broadcasted_iota(jnp.int32, logits.shape, 1)
    lg = jnp.where(lane < N_EXPERTS, logits, -jnp.inf)
    m1 = jnp.max(lg, axis=1, keepdims=True)
    i1 = jnp.min(jnp.where(lg == m1, lane, ROUTER_PAD), axis=1, keepdims=True)
    lg2 = jnp.where(lane == i1, -jnp.inf, lg)
    m2 = jnp.max(lg2, axis=1, keepdims=True)
    i2 = jnp.min(jnp.where(lg2 == m2, lane, ROUTER_PAD), axis=1, keepdims=True)
    e2 = jnp.exp(m2 - m1)
    den = 1.0 + e2
    return jnp.where(lane == i1, 1.0 / den, 0.0) + jnp.where(lane == i2, e2 / den, 0.0)


def _outproj_kernel(mix_ref, x_ref, w_ref, g_ref, *rest, n_pieces, with_router):
    if with_router:
        wr_ref, br_ref, xo_ref, hn_ref, cmb_ref, wbf = rest
    else:
        xo_ref, hn_ref, wbf = rest
    step = pl.program_id(0)
    _stash_piece(step, w_ref, wbf, n_pieces)

    @pl.when(step >= n_pieces)
    def _():
        y = jnp.dot(mix_ref[...], wbf[...], preferred_element_type=F32)
        xo = x_ref[...] + y
        xo_ref[...] = xo
        hn = _rms(xo, g_ref[...]).astype(BF16)
        hn_ref[...] = hn
        if with_router:
            logits = jnp.dot(hn, wr_ref[...].astype(BF16), preferred_element_type=F32) + br_ref[...]
            cmb_ref[...] = _top2_gates(logits)


def _outproj(mix, x, w_out, g, layer, router, *, tm):
    m = x.shape[0]
    nm = m // tm
    n_pieces = 8
    rows = _piece_rows(D_MODEL, n_pieces)
    tile = lambda i: jnp.maximum(i - n_pieces, 0)
    row_spec = lambda width: pl.BlockSpec((tm, width), lambda i: (tile(i), 0))
    const = lambda r, c: pl.BlockSpec((r, c), lambda i: (0, 0))
    in_specs = [row_spec(D_MODEL), row_spec(D_MODEL),
                pl.BlockSpec((None, rows, D_MODEL), lambda i: (layer, jnp.minimum(i, n_pieces - 1), 0)),
                const(1, D_MODEL)]
    args = [mix, x, w_out, g]
    out_shape = [jax.ShapeDtypeStruct((m, D_MODEL), F32), jax.ShapeDtypeStruct((m, D_MODEL), BF16)]
    out_specs = [row_spec(D_MODEL), row_spec(D_MODEL)]
    if router is not None:
        in_specs += [const(D_MODEL, ROUTER_PAD), const(1, ROUTER_PAD)]
        args += list(router)
        out_shape.append(jax.ShapeDtypeStruct((m, ROUTER_PAD), F32))
        out_specs.append(row_spec(ROUTER_PAD))
    return pl.pallas_call(
        functools.partial(_outproj_kernel, n_pieces=n_pieces, with_router=router is not None),
        grid=(n_pieces + nm,),
        in_specs=in_specs,
        out_specs=out_specs,
        out_shape=out_shape,
        scratch_shapes=[pltpu.VMEM((D_MODEL, D_MODEL), BF16)],
        compiler_params=_params(),
        name="outproj",
    )(*args)


def _ffn_kernel(hn_ref, acc_ref, *rest, pieces, expert, final_norm):
    rest = list(rest)
    cmb_ref = rest.pop(0) if expert is not None else None
    gf_ref = rest.pop(0) if final_norm else None
    wg_ref, wu_ref, wd_ref, o_ref, wgb, wub, wdb = rest
    step = pl.program_id(0)
    _stash_piece(step, wg_ref, wgb, pieces[0])
    _stash_piece(step, wu_ref, wub, pieces[0])
    _stash_piece(step, wd_ref, wdb, pieces[1])

    @pl.when(step >= max(pieces))
    def _():
        hn = hn_ref[...]
        gate = jnp.dot(hn, wgb[...], preferred_element_type=F32)
        up = jnp.dot(hn, wub[...], preferred_element_type=F32)
        h = (gate * (1.0 / (1.0 + jnp.exp(-gate)))) * up
        if expert is not None:
            h = h * cmb_ref[:, expert:expert + 1]
        out = acc_ref[...] + jnp.dot(h.astype(BF16), wdb[...], preferred_element_type=F32)
        if final_norm:
            out = _rms(out, gf_ref[...])
        o_ref[...] = out


def _ffn_slab(hn, acc, wg, wu, wd, lead, slab, width, *, tm, cmb=None, expert=None, g_final=None):
    m = hn.shape[0]
    nm = m // tm
    pieces = (16, width // LANES)
    n_pro = max(pieces)
    rows_in = _piece_rows(D_MODEL, pieces[0])
    rows_dn = _piece_rows(width, pieces[1])
    tile = lambda i: jnp.maximum(i - n_pro, 0)
    row_spec = lambda w_: pl.BlockSpec((tm, w_), lambda i: (tile(i), 0))
    nlead = (None,) * len(lead)
    in_specs = [row_spec(D_MODEL), row_spec(D_MODEL)]
    args = [hn, acc]
    if expert is not None:
        in_specs.append(row_spec(ROUTER_PAD))
        args.append(cmb)
    if g_final is not None:
        in_specs.append(pl.BlockSpec((1, D_MODEL), lambda i: (0, 0)))
        args.append(g_final)
    up_spec = pl.BlockSpec(nlead + (rows_in, width), lambda i: lead + (jnp.minimum(i, pieces[0] - 1), slab))
    in_specs += [up_spec, up_spec,
                 pl.BlockSpec(nlead + (rows_dn, D_MODEL),
                              lambda i: lead + (slab * pieces[1] + jnp.minimum(i, pieces[1] - 1), 0))]
    args += [wg, wu, wd]
    return pl.pallas_call(
        functools.partial(_ffn_kernel, pieces=pieces, expert=expert, final_norm=g_final is not None),
        grid=(n_pro + nm,),
        in_specs=in_specs,
        out_specs=row_spec(D_MODEL),
        out_shape=jax.ShapeDtypeStruct((m, D_MODEL), F32),
        scratch_shapes=[pltpu.VMEM((D_MODEL, width), BF16), pltpu.VMEM((D_MODEL, width), BF16),
                        pltpu.VMEM((width, D_MODEL), BF16)],
        compiler_params=_params(),
        name="ffn_slab",
    )(*args)


def _rope_tables(pos):
    half = ROPE_DIM // 2
    inv = ROPE_THETA ** (-jnp.arange(0, ROPE_DIM, 2, dtype=F32) / ROPE_DIM)
    ang = pos[:, None] * inv[None, :]
    cos, sin = jnp.cos(ang), jnp.sin(ang)
    n = pos.shape[0]
    rest = HEAD_DIM - ROPE_DIM
    a = jnp.concatenate([cos, cos, jnp.ones((n, rest), F32)], axis=1)
    b = jnp.concatenate([jnp.zeros((n, half), F32), sin, jnp.zeros((n, rest), F32)], axis=1)
    c = jnp.concatenate([-sin, jnp.zeros((n, half + rest), F32)], axis=1)
    reps = LANES // HEAD_DIM
    scale = HEAD_DIM ** -0.5
    return tuple(jnp.tile(t, (1, reps)) * scale for t in (a, b, c))


def kernel(x_prompt, x_sample, cache_k, cache_v, state_pool, norm_mix, w_in, attn_sinks, w_pool, pool_scale,
           w_out, norm_ffn, w_gate_dense, w_up_dense, w_down_dense, w_router, b_router, w_gate_exp, w_up_exp,
           w_down_exp, norm_final):
    tm = 512
    xp = x_prompt.reshape(BATCH * SEQ, D_MODEL)
    xs = x_sample.reshape(DEC_BATCH, D_MODEL)
    rope_p = _rope_tables(jnp.arange(SEQ, dtype=F32))
    rope_s = _rope_tables(jnp.full((DEC_BATCH,), PAST_LEN, dtype=F32))
    ck = cache_k.reshape(DEPTH, DEC_BATCH, WINDOW, KV_WIDTH)
    cv = cache_v.reshape(DEPTH, DEC_BATCH, WINDOW, KV_WIDTH)
    g_final = norm_final.reshape(1, D_MODEL)
    slab_w = D_FF // DENSE_SLABS
    heads = (N_KV_HEADS, HEAD_DIM)

    nk_p, nv_p, nu_p, nk_s, nv_s, nu_s = [], [], [], [], [], []
    for l in range(DEPTH):
        g_mix = norm_mix[l].reshape(1, D_MODEL)
        g_ffn = norm_ffn[l].reshape(1, D_MODEL)
        scale = pool_scale[l].reshape(1, POOL_WIDTH)
        moe = l % 2 == 1
        i = l // 2
        last = l == DEPTH - 1
        router = None
        if moe:
            router = (jnp.pad(w_router[i], ((0, 0), (0, ROUTER_PAD - N_EXPERTS))),
                      jnp.pad(b_router[i], (0, ROUTER_PAD - N_EXPERTS)).reshape(1, ROUTER_PAD))

        q, k, v, u, kvt, ut = _inproj(xp, g_mix, w_in, l, rope_p, tm=tm)
        mix = _mixer(q, k, v, u, attn_sinks[l], w_pool, scale, l)
        xp, hn, *cmb = _outproj(mix, xp, w_out, g_ffn, l, router, tm=tm)
        nk_p.append(kvt[:, :KV_WIDTH].reshape(BATCH, WINDOW, *heads))
        nv_p.append(kvt[:, KV_WIDTH:].reshape(BATCH, WINDOW, *heads))
        nu_p.append(ut.reshape(BATCH, BF16_ROWS, POOL_WIDTH)[:, BF16_ROWS - POOL_STATE:])
        if moe:
            for e in range(N_EXPERTS):
                xp = _ffn_slab(hn, xp, w_gate_exp, w_up_exp, w_down_exp, (i, e), 0, D_FF_EXPERT, tm=tm,
                               cmb=cmb[0], expert=e, g_final=g_final if last and e == N_EXPERTS - 1 else None)
        else:
            for s in range(DENSE_SLABS):
                xp = _ffn_slab(hn, xp, w_gate_dense, w_up_dense, w_down_dense, (i,), s, slab_w, tm=tm,
                               g_final=g_final if last and s == DENSE_SLABS - 1 else None)

        z = _sample_inproj(xs, g_mix, w_in, l, rope_s)
        a_s, nk, nv, nu, ps = _sample_mixer(
            z[:, :ATTN_WIDTH], z[:, ATTN_WIDTH:ATTN_WIDTH + KV_WIDTH],
            z[:, ATTN_WIDTH + KV_WIDTH:ATTN_WIDTH + 2 * KV_WIDTH], z[:, ATTN_WIDTH + 2 * KV_WIDTH:],
            ck, cv, state_pool, attn_sinks[l], w_pool, scale, l)
        mix_s = jnp.concatenate([a_s.reshape(DEC_BATCH, ATTN_WIDTH), ps], axis=1)
        xs = _sample_outproj(mix_s, xs, w_out, l)
        if moe:
            xs = _sample_ffn(xs, g_ffn, w_gate_exp, w_up_exp, w_down_exp, i, router=router,
                             g_final=g_final if last else None)
        else:
            xs = _sample_ffn(xs, g_ffn, w_gate_dense, w_up_dense, w_down_dense, i,
                             g_final=g_final if last else None)
        nk_s.append(nk.reshape(DEC_BATCH, WINDOW, *heads))
        nv_s.append(nv.reshape(DEC_BATCH, WINDOW, *heads))
        nu_s.append(nu)

    y_prompt = xp.reshape(BATCH, SEQ, D_MODEL)
    y_sample = xs.reshape(DEC_BATCH, 1, D_MODEL)
    return (y_prompt, y_sample, jnp.stack(nk_p), jnp.stack(nv_p), jnp.stack(nu_p),
            jnp.stack(nk_s), jnp.stack(nv_s), jnp.stack(nu_s))
```

```python
import functools

import jax
import jax.numpy as jnp
from jax import lax
from jax.experimental import pallas as pl
from jax.experimental.pallas import tpu as pltpu

F32 = jnp.float32
BF16 = jnp.bfloat16

D_MODEL = 2048
BATCH = 2
SEQ = 4096
DEPTH = 4
DEC_BATCH = 32
PAST_LEN = 16384
N_HEADS = 16
N_KV_HEADS = 4
HEAD_DIM = 64
GROUP = N_HEADS // N_KV_HEADS
ATTN_WIDTH = N_HEADS * HEAD_DIM
KV_WIDTH = N_KV_HEADS * HEAD_DIM
WINDOW = 128
ROPE_DIM = HEAD_DIM // 4
ROPE_THETA = 500000.0
POOL_WINDOWS = (2, 4, 8, 16)
POOL_WIDTH = D_MODEL - ATTN_WIDTH
POOL_GROUP_DIM = POOL_WIDTH // len(POOL_WINDOWS)
POOL_STATE = max(POOL_WINDOWS) - 1
IN_WIDTH = ATTN_WIDTH + 2 * KV_WIDTH + POOL_WIDTH
D_FF = 5632
N_EXPERTS = 8
D_FF_EXPERT = D_MODEL // 2
EPS = 1e-5
NEG_INF = -1e30

LANES = 128
BF16_ROWS = 16
VMEM_LIMIT = 56 * 1024 * 1024
DENSE_SLABS = 4
ROUTER_PAD = LANES


def _params(n_axes=1):
    return pltpu.CompilerParams(dimension_semantics=("arbitrary",) * n_axes,
                                vmem_limit_bytes=VMEM_LIMIT)


def _rms(x, g):
    ms = jnp.mean(x * x, axis=-1, keepdims=True)
    return (x * lax.rsqrt(ms + EPS)) * g


def _rope_mix(zc, ra, rb, rc):
    return zc * ra + pltpu.roll(zc, ROPE_DIM // 2, 1) * rb + pltpu.roll(zc, LANES - ROPE_DIM // 2, 1) * rc


def _split_bf16(a):
    hi = a.astype(BF16)
    return hi, (a - hi.astype(F32)).astype(BF16)


def _dot3(a, b, dims=(((1,), (0,)), ((), ()))):
    m = a.shape[0]
    ah, al = _split_bf16(a)
    bh, bl = _split_bf16(b)
    t = lax.dot_general(jnp.concatenate([ah, al], axis=0), bh, dims, preferred_element_type=F32)
    return t[:m] + t[m:] + lax.dot_general(ah, bl, dims, preferred_element_type=F32)


def _stash_piece(step, w_ref, wbf, n_pieces):
    rows = w_ref.shape[0]

    @pl.when(step < n_pieces)
    def _():
        r0 = pl.multiple_of(step * rows, rows)
        wbf[pl.ds(r0, rows), :] = w_ref[...].astype(BF16)


def _piece_rows(total_rows, n_pieces):
    rows, rem = divmod(total_rows, n_pieces)
    assert rem == 0 and rows % BF16_ROWS == 0, (total_rows, n_pieces)
    return rows


def _inproj_kernel(x_ref, g_ref, w_ref, ra_ref, rb_ref, rc_ref, q_ref, k_ref, v_ref, u_ref, kvt_ref, ut_ref,
                   wbf, *, n_pieces):
    step = pl.program_id(0)
    _stash_piece(step, w_ref, wbf, n_pieces)

    @pl.when(step >= n_pieces)
    def _():
        tm = x_ref.shape[0]
        h = _rms(x_ref[...], g_ref[...])
        z = jnp.dot(h.astype(BF16), wbf[...], preferred_element_type=F32)
        ra, rb, rc = ra_ref[...], rb_ref[...], rc_ref[...]
        n_rot = (ATTN_WIDTH + KV_WIDTH) // LANES
        rot = []
        for c in range(n_rot):
            r = _rope_mix(z[:, c * LANES:(c + 1) * LANES], ra, rb, rc)
            if c >= ATTN_WIDTH // LANES:
                r = r * (HEAD_DIM ** 0.5)
            rot.append(r)
        nq = ATTN_WIDTH // LANES
        for c in range(nq):
            q_ref[:, c * LANES:(c + 1) * LANES] = rot[c].astype(q_ref.dtype)
        for c in range(nq, n_rot):
            k_ref[:, (c - nq) * LANES:(c - nq + 1) * LANES] = rot[c].astype(k_ref.dtype)
        v = z[:, ATTN_WIDTH + KV_WIDTH:ATTN_WIDTH + 2 * KV_WIDTH]
        u = z[:, ATTN_WIDTH + 2 * KV_WIDTH:]
        v_ref[...] = v.astype(v_ref.dtype)
        u_ref[...] = u.astype(u_ref.dtype)
        for c in range(nq, n_rot):
            kvt_ref[:, (c - nq) * LANES:(c - nq + 1) * LANES] = rot[c][tm - WINDOW:, :]
        kvt_ref[:, KV_WIDTH:] = v[tm - WINDOW:, :]
        ut_ref[...] = u[tm - BF16_ROWS:, :]


def _inproj(x, g, w_in, layer, rope, *, tm):
    m = x.shape[0]
    nm = m // tm
    n_pieces = 8
    rows = _piece_rows(D_MODEL, n_pieces)
    tiles_per_seq = SEQ // tm
    tile = lambda i: jnp.maximum(i - n_pieces, 0)
    row_spec = lambda width: pl.BlockSpec((tm, width), lambda i: (tile(i), 0))
    rope_spec = pl.BlockSpec((tm, LANES), lambda i: (tile(i) % tiles_per_seq, 0))
    out_shape = [jax.ShapeDtypeStruct((m, ATTN_WIDTH), BF16),
                 jax.ShapeDtypeStruct((m, KV_WIDTH), BF16),
                 jax.ShapeDtypeStruct((m, KV_WIDTH), BF16),
                 jax.ShapeDtypeStruct((m, POOL_WIDTH), BF16),
                 jax.ShapeDtypeStruct((BATCH * WINDOW, 2 * KV_WIDTH), F32),
                 jax.ShapeDtypeStruct((BATCH * BF16_ROWS, POOL_WIDTH), F32)]
    out_specs = [row_spec(ATTN_WIDTH), row_spec(KV_WIDTH), row_spec(KV_WIDTH), row_spec(POOL_WIDTH),
                 pl.BlockSpec((WINDOW, 2 * KV_WIDTH), lambda i: (tile(i) // tiles_per_seq, 0)),
                 pl.BlockSpec((BF16_ROWS, POOL_WIDTH), lambda i: (tile(i) // tiles_per_seq, 0))]
    return pl.pallas_call(
        functools.partial(_inproj_kernel, n_pieces=n_pieces),
        grid=(n_pieces + nm,),
        in_specs=[row_spec(D_MODEL),
                  pl.BlockSpec((1, D_MODEL), lambda i: (0, 0)),
                  pl.BlockSpec((None, rows, IN_WIDTH), lambda i: (layer, jnp.minimum(i, n_pieces - 1), 0)),
                  rope_spec, rope_spec, rope_spec],
        out_specs=out_specs,
        out_shape=out_shape,
        scratch_shapes=[pltpu.VMEM((D_MODEL, IN_WIDTH), BF16)],
        compiler_params=_params(),
        name="inproj",
    )(x, g, w_in, *rope)


def _sink_softmax(s, sink, axis):
    mx = jnp.maximum(jnp.max(s, axis=axis, keepdims=True), sink)
    p = jnp.exp(s - mx)
    den = jnp.sum(p, axis=axis, keepdims=True) + jnp.exp(sink - mx)
    return p * (1.0 / den)


def _pool_group(d, wpb_g, scale_g):
    return jnp.dot(d.astype(BF16), wpb_g, preferred_element_type=F32) * scale_g


def _mixer_kernel(sink_ref, q_ref, kc_ref, kp_ref, vc_ref, vp_ref, uc_ref, up_ref, wp_ref, sc_ref,
                  o_ref, wpb, band):
    b = pl.program_id(0)
    n = pl.program_id(1)
    blk = WINDOW

    @pl.when((b == 0) & (n == 0))
    def _():
        wpb[...] = wp_ref[...].astype(BF16)
        tok = lax.broadcasted_iota(jnp.int32, (blk, 2 * blk), 0) + blk
        src = lax.broadcasted_iota(jnp.int32, (blk, 2 * blk), 1)
        for g, w in enumerate(POOL_WINDOWS):
            band[g] = jnp.where((src <= tok) & (src > tok - w), 1.0, 0.0).astype(BF16)

    half = LANES // 2
    kk = jnp.concatenate([kp_ref[...], kc_ref[...]], axis=0).astype(F32)
    vv = jnp.concatenate([vp_ref[...], vc_ref[...]], axis=0).astype(F32)
    lane = lax.broadcasted_iota(jnp.int32, (2 * blk, LANES), 1)
    low = lane < half
    key = lax.broadcasted_iota(jnp.int32, (4 * blk, blk), 0) & (2 * blk - 1)
    qry = lax.broadcasted_iota(jnp.int32, (4 * blk, blk), 1)
    valid = (key > qry) & (key <= qry + blk) & ((n > 0) | (key >= blk))

    for h in range(N_KV_HEADS):
        g0 = (h // 2) * LANES
        kg, vg = kk[:, g0:g0 + LANES], vv[:, g0:g0 + LANES]
        if h % 2 == 0:
            klo, vlo = jnp.where(low, kg, 0.0), jnp.where(low, vg, 0.0)
            khi, vhi = pltpu.roll(klo, half, 1), pltpu.roll(vlo, half, 1)
        else:
            khi, vhi = jnp.where(low, 0.0, kg), jnp.where(low, 0.0, vg)
            klo, vlo = pltpu.roll(khi, half, 1), pltpu.roll(vhi, half, 1)
        kcat = jnp.concatenate([klo, khi], axis=0).astype(BF16)
        vcat = jnp.concatenate([vlo, vhi], axis=0).astype(BF16)
        for pair in range(h * GROUP // 2, (h + 1) * GROUP // 2):
            qp = q_ref[:, pair * LANES:(pair + 1) * LANES]
            st = lax.dot_general(kcat, qp, (((1,), (1,)), ((), ())), preferred_element_type=F32)
            st = jnp.where(valid, st, NEG_INF)
            pt = [_sink_softmax(st[t * 2 * blk:(t + 1) * 2 * blk], sink_ref[2 * pair + t], 0).astype(BF16)
                  for t in range(2)]
            o = lax.dot_general(jnp.concatenate(pt, axis=0), vcat, (((0,), (0,)), ((), ())),
                                preferred_element_type=F32)
            o_ref[:, pair * LANES:(pair + 1) * LANES] = o.astype(o_ref.dtype)

    ext = jnp.concatenate([jnp.where(n > 0, up_ref[...], jnp.zeros_like(up_ref)), uc_ref[...]], axis=0)
    pos = (n * blk + lax.broadcasted_iota(jnp.int32, (blk, 1), 0)).astype(F32)
    for g, w in enumerate(POOL_WINDOWS):
        c0, c1 = g * POOL_GROUP_DIM, (g + 1) * POOL_GROUP_DIM
        wsum = jnp.dot(band[g], ext[:, c0:c1], preferred_element_type=F32)
        d = wsum * (1.0 / jnp.minimum(float(w), pos + 1.0)) - uc_ref[:, c0:c1].astype(F32)
        y = _pool_group(d, wpb[g], sc_ref[:, c0:c1])
        o_ref[:, ATTN_WIDTH + c0:ATTN_WIDTH + c1] = y.astype(o_ref.dtype)


def _mixer(q, k, v, u, sinks, w_pool, scale, layer):
    assert u.dtype == BF16
    nb = SEQ // WINDOW
    cur = lambda b, n: (b * nb + n, 0)
    prev = lambda b, n: (b * nb + jnp.maximum(n - 1, 0), 0)
    n_grp = len(POOL_WINDOWS)
    return pl.pallas_call(
        _mixer_kernel,
        grid=(BATCH, nb),
        in_specs=[pl.BlockSpec(memory_space=pltpu.SMEM),
                  pl.BlockSpec((WINDOW, ATTN_WIDTH), cur),
                  pl.BlockSpec((WINDOW, KV_WIDTH), cur), pl.BlockSpec((WINDOW, KV_WIDTH), prev),
                  pl.BlockSpec((WINDOW, KV_WIDTH), cur), pl.BlockSpec((WINDOW, KV_WIDTH), prev),
                  pl.BlockSpec((WINDOW, POOL_WIDTH), cur), pl.BlockSpec((WINDOW, POOL_WIDTH), prev),
                  pl.BlockSpec((None, n_grp, POOL_GROUP_DIM, POOL_GROUP_DIM), lambda b, n: (layer, 0, 0, 0)),
                  pl.BlockSpec((1, POOL_WIDTH), lambda b, n: (0, 0))],
        out_specs=pl.BlockSpec((WINDOW, D_MODEL), cur),
        out_shape=jax.ShapeDtypeStruct((BATCH * SEQ, D_MODEL), BF16),
        scratch_shapes=[pltpu.VMEM((n_grp, POOL_GROUP_DIM, POOL_GROUP_DIM), BF16),
                        pltpu.VMEM((n_grp, WINDOW, 2 * WINDOW), BF16)],
        compiler_params=_params(2),
        name="mixer",
    )(sinks, q, k, k, v, v, u, u, w_pool, scale)


def _sample_mixer_kernel(sink_ref, q_ref, kn_ref, vn_ref, un_ref, ck_ref, cv_ref, st_ref, wp_ref, sc_ref,
                         a_ref, nk_ref, nv_ref, nu_ref, po_ref, dscr):
    b = pl.program_id(0)
    nk_ref[0:WINDOW - 1, :] = ck_ref[1:WINDOW, :]
    nk_ref[WINDOW - 1:WINDOW, :] = kn_ref[...]
    nv_ref[0:WINDOW - 1, :] = cv_ref[1:WINDOW, :]
    nv_ref[WINDOW - 1:WINDOW, :] = vn_ref[...]
    keys = nk_ref[...]
    vals = nv_ref[...]

    q = q_ref[...]
    qe = jnp.concatenate([q] * N_KV_HEADS, axis=1)
    own = (lax.broadcasted_iota(jnp.int32, qe.shape, 0) // GROUP
           == lax.broadcasted_iota(jnp.int32, qe.shape, 1) // HEAD_DIM)
    qe = jnp.where(own, qe, 0.0)
    s = _dot3(qe, keys, (((1,), (1,)), ((), ())))
    r = _dot3(_sink_softmax(s, sink_ref[...], 1), vals)
    r = jnp.where(own, r, 0.0)
    o = r[:, 0:HEAD_DIM]
    for h in range(1, N_KV_HEADS):
        o = o + r[:, h * HEAD_DIM:(h + 1) * HEAD_DIM]
    a_ref[...] = o

    st = st_ref[...]
    un = un_ref[...]
    nu_ref[0:POOL_STATE - 1, :] = st[1:POOL_STATE, :]
    nu_ref[POOL_STATE - 1:POOL_STATE, :] = un
    for g, w in enumerate(POOL_WINDOWS):
        c0, c1 = g * POOL_GROUP_DIM, (g + 1) * POOL_GROUP_DIM
        tok = un[:, c0:c1]
        wsum = tok + jnp.sum(st[POOL_STATE - (w - 1):POOL_STATE, c0:c1], axis=0, keepdims=True)
        dscr[pl.ds(b, 1), c0:c1] = wsum / float(min(w, PAST_LEN + 1)) - tok

    @pl.when(b == pl.num_programs(0) - 1)
    def _():
        for g in range(len(POOL_WINDOWS)):
            c0, c1 = g * POOL_GROUP_DIM, (g + 1) * POOL_GROUP_DIM
            po_ref[:, c0:c1] = _dot3(dscr[:, c0:c1], wp_ref[g]) * sc_ref[:, c0:c1]


def _sample_mixer(q, k, v, u, cache_k, cache_v, state, sinks, w_pool, scale, layer):
    nb = DEC_BATCH
    per_b = lambda *shape: pl.BlockSpec((None,) + shape, lambda b: (b,) + (0,) * len(shape))
    per_lb = lambda *shape: pl.BlockSpec((None, None) + shape, lambda b: (layer, b) + (0,) * len(shape))
    return pl.pallas_call(
        _sample_mixer_kernel,
        grid=(nb,),
        in_specs=[pl.BlockSpec((N_HEADS, 1), lambda b: (0, 0)),
                  per_b(N_HEADS, HEAD_DIM), per_b(1, KV_WIDTH), per_b(1, KV_WIDTH), per_b(1, POOL_WIDTH),
                  per_lb(WINDOW, KV_WIDTH), per_lb(WINDOW, KV_WIDTH), per_lb(POOL_STATE, POOL_WIDTH),
                  pl.BlockSpec((None, len(POOL_WINDOWS), POOL_GROUP_DIM, POOL_GROUP_DIM),
                               lambda b: (layer, 0, 0, 0)),
                  pl.BlockSpec((1, POOL_WIDTH), lambda b: (0, 0))],
        out_specs=[per_b(N_HEADS, HEAD_DIM), per_b(WINDOW, KV_WIDTH), per_b(WINDOW, KV_WIDTH),
                   per_b(POOL_STATE, POOL_WIDTH), pl.BlockSpec((nb, POOL_WIDTH), lambda b: (0, 0))],
        out_shape=[jax.ShapeDtypeStruct((nb, N_HEADS, HEAD_DIM), F32),
                   jax.ShapeDtypeStruct((nb, WINDOW, KV_WIDTH), F32),
                   jax.ShapeDtypeStruct((nb, WINDOW, KV_WIDTH), F32),
                   jax.ShapeDtypeStruct((nb, POOL_STATE, POOL_WIDTH), F32),
                   jax.ShapeDtypeStruct((nb, POOL_WIDTH), F32)],
        scratch_shapes=[pltpu.VMEM((nb, POOL_WIDTH), F32)],
        compiler_params=_params(),
        name="sample_mixer",
    )(sinks.reshape(N_HEADS, 1), q.reshape(nb, N_HEADS, HEAD_DIM), k.reshape(nb, 1, KV_WIDTH),
      v.reshape(nb, 1, KV_WIDTH), u.reshape(nb, 1, POOL_WIDTH), cache_k, cache_v, state, w_pool, scale)


SAMPLE_TN = 512


def _sample_inproj_kernel(x_ref, g_ref, w_ref, ra_ref, rb_ref, rc_ref, z_ref):
    j = pl.program_id(0)
    z = _dot3(_rms(x_ref[...], g_ref[...]), w_ref[...])
    ra, rb, rc = ra_ref[...], rb_ref[...], rc_ref[...]
    per_tile = SAMPLE_TN // LANES
    for c in range(per_tile):
        zc = z[:, c * LANES:(c + 1) * LANES]
        grp = j * per_tile + c
        r = _rope_mix(zc, ra, rb, rc)
        r = jnp.where(grp >= ATTN_WIDTH // LANES, r * (HEAD_DIM ** 0.5), r)
        z_ref[:, c * LANES:(c + 1) * LANES] = jnp.where(grp < (ATTN_WIDTH + KV_WIDTH) // LANES, r, zc)


def _sample_inproj(x, g, w_in, layer, rope):
    m = x.shape[0]
    const = lambda r, c: pl.BlockSpec((r, c), lambda j: (0, 0))
    return pl.pallas_call(
        _sample_inproj_kernel,
        grid=(IN_WIDTH // SAMPLE_TN,),
        in_specs=[const(m, D_MODEL), const(1, D_MODEL),
                  pl.BlockSpec((None, D_MODEL, SAMPLE_TN), lambda j: (layer, 0, j)),
                  const(m, LANES), const(m, LANES), const(m, LANES)],
        out_specs=pl.BlockSpec((m, SAMPLE_TN), lambda j: (0, j)),
        out_shape=jax.ShapeDtypeStruct((m, IN_WIDTH), F32),
        compiler_params=_params(),
        name="sample_inproj",
    )(x, g, w_in, *rope)


def _sample_outproj_kernel(mix_ref, x_ref, w_ref, o_ref):
    o_ref[...] = x_ref[...] + _dot3(mix_ref[...], w_ref[...])


def _sample_outproj(mix, x, w_out, layer):
    m = x.shape[0]
    col = pl.BlockSpec((m, SAMPLE_TN), lambda j: (0, j))
    return pl.pallas_call(
        _sample_outproj_kernel,
        grid=(D_MODEL // SAMPLE_TN,),
        in_specs=[pl.BlockSpec((m, D_MODEL), lambda j: (0, 0)), col,
                  pl.BlockSpec((None, D_MODEL, SAMPLE_TN), lambda j: (layer, 0, j))],
        out_specs=col,
        out_shape=jax.ShapeDtypeStruct((m, D_MODEL), F32),
        compiler_params=_params(),
        name="sample_outproj",
    )(mix, x, w_out)


def _sample_ffn_kernel(x_ref, g_ref, *rest, moe, final_norm):
    rest = list(rest)
    wr_ref, br_ref = (rest.pop(0), rest.pop(0)) if moe else (None, None)
    gf_ref = rest.pop(0) if final_norm else None
    wg_ref, wu_ref, wd_ref, o_ref, acc, hn_scr, cmb = rest
    e, f = pl.program_id(0), pl.program_id(1)

    @pl.when((e == 0) & (f == 0))
    def _():
        hn = _rms(x_ref[...], g_ref[...])
        hn_scr[...] = hn
        acc[...] = jnp.zeros_like(acc)
        if moe:
            cmb[...] = _top2_gates(_dot3(hn, wr_ref[...]) + br_ref[...])

    hn = hn_scr[...]
    gate = _dot3(hn, wg_ref[...])
    up = _dot3(hn, wu_ref[...])
    h = (gate * (1.0 / (1.0 + jnp.exp(-gate)))) * up
    if moe:
        lane = lax.broadcasted_iota(jnp.int32, cmb.shape, 1)
        h = h * jnp.sum(jnp.where(lane == e, cmb[...], 0.0), axis=1, keepdims=True)
    acc[...] += _dot3(h, wd_ref[...])

    @pl.when((e == pl.num_programs(0) - 1) & (f == pl.num_programs(1) - 1))
    def _():
        out = x_ref[...] + acc[...]
        if final_norm:
            out = _rms(out, gf_ref[...])
        o_ref[...] = out


def _sample_ffn(x, g, wg, wu, wd, idx, *, router=None, g_final=None):
    m = x.shape[0]
    moe = router is not None
    n_e = N_EXPERTS if moe else 1
    width = wg.shape[-1]
    const = lambda r, c: pl.BlockSpec((r, c), lambda e, f: (0, 0))
    in_specs = [const(m, D_MODEL), const(1, D_MODEL)]
    args = [x, g]
    if moe:
        in_specs += [const(D_MODEL, ROUTER_PAD), const(1, ROUTER_PAD)]
        args += list(router)
        up_spec = pl.BlockSpec((None, None, D_MODEL, SAMPLE_TN), lambda e, f: (idx, e, 0, f))
        dn_spec = pl.BlockSpec((None, None, SAMPLE_TN, D_MODEL), lambda e, f: (idx, e, f, 0))
    else:
        up_spec = pl.BlockSpec((None, D_MODEL, SAMPLE_TN), lambda e, f: (idx, 0, f))
        dn_spec = pl.BlockSpec((None, SAMPLE_TN, D_MODEL), lambda e, f: (idx, f, 0))
    if g_final is not None:
        in_specs.append(const(1, D_MODEL))
        args.append(g_final)
    return pl.pallas_call(
        functools.partial(_sample_ffn_kernel, moe=moe, final_norm=g_final is not None),
        grid=(n_e, width // SAMPLE_TN),
        in_specs=in_specs + [up_spec, up_spec, dn_spec],
        out_specs=const(m, D_MODEL),
        out_shape=jax.ShapeDtypeStruct((m, D_MODEL), F32),
        scratch_shapes=[pltpu.VMEM((m, D_MODEL), F32), pltpu.VMEM((m, D_MODEL), F32),
                        pltpu.VMEM((m, ROUTER_PAD), F32)],
        compiler_params=_params(2),
        name="sample_ffn",
    )(*args, wg, wu, wd)


def _top2(logits):
    lane = lax.broadcasted_iota(jnp.int32, logits.shape, 1)
    lg = jnp.where(lane < N_EXPERTS, logits, -jnp.inf)
    m1 = jnp.max(lg, axis=1, keepdims=True)
    i1 = jnp.min(jnp.where(lg == m1, lane, ROUTER_PAD), axis=1, keepdims=True)
    lg2 = jnp.where(lane == i1, -jnp.inf, lg)
    m2 = jnp.max(lg2, axis=1, keepdims=True)
    i2 = jnp.min(jnp.where(lg2 == m2, lane, ROUTER_PAD), axis=1, keepdims=True)
    e2 = jnp.exp(m2 - m1)
    den = 1.0 + e2
    return i1, i2, 1.0 / den, e2 / den


def _top2_gates(logits):
    i1, i2, g1, g2 = _top2(logits)
    lane = lax.broadcasted_iota(jnp.int32, logits.shape, 1)
    return jnp.where(lane == i1, g1, 0.0) + jnp.where(lane == i2, g2, 0.0)


ROUTE_I1, ROUTE_I2, ROUTE_G1, ROUTE_G2, ROUTE_P1, ROUTE_P2 = range(6)


def _outproj_kernel(mix_ref, x_ref, w_ref, g_ref, *rest, n_pieces, with_router):
    if with_router:
        wr_ref, br_ref, xo_ref, hn_ref, route_ref, cnt_ref, wbf, tri, carry = rest
    else:
        xo_ref, hn_ref, wbf = rest
    step = pl.program_id(0)
    _stash_piece(step, w_ref, wbf, n_pieces)

    if with_router:
        @pl.when(step == 0)
        def _():
            row = lax.broadcasted_iota(jnp.int32, tri.shape, 0)
            col = lax.broadcasted_iota(jnp.int32, tri.shape, 1)
            tri[...] = jnp.where(col < row, 1.0, 0.0).astype(BF16)
            carry[...] = jnp.zeros_like(carry)

    @pl.when(step >= n_pieces)
    def _():
        y = jnp.dot(mix_ref[...], wbf[...], preferred_element_type=F32)
        xo = x_ref[...] + y
        xo_ref[...] = xo
        hn = _rms(xo, g_ref[...])
        if not with_router:
            hn_ref[...] = hn.astype(BF16)
            return
        hn_ref[...] = hn
        logits = jnp.dot(hn.astype(BF16), wr_ref[...].astype(BF16), preferred_element_type=F32) + br_ref[...]
        i1, i2, g1, g2 = _top2(logits)
        lane = lax.broadcasted_iota(jnp.int32, logits.shape, 1)
        picked = jnp.where((lane == i1) | (lane == i2), 1.0, 0.0)
        ahead = jnp.dot(tri[...], picked.astype(BF16), preferred_element_type=F32) + carry[...]
        p1 = jnp.sum(jnp.where(lane == i1, ahead, 0.0), axis=1, keepdims=True)
        p2 = jnp.sum(jnp.where(lane == i2, ahead, 0.0), axis=1, keepdims=True)
        carry[...] += jnp.sum(picked, axis=0, keepdims=True)
        cnt_ref[...] = carry[...]
        rec = jnp.zeros(logits.shape, F32)
        for k, val in ((ROUTE_I1, i1.astype(F32)), (ROUTE_I2, i2.astype(F32)), (ROUTE_G1, g1), (ROUTE_G2, g2),
                       (ROUTE_P1, p1), (ROUTE_P2, p2)):
            rec = jnp.where(lane == k, val, rec)
        route_ref[...] = rec


def _outproj(mix, x, w_out, g, layer, router, *, tm):
    m = x.shape[0]
    nm = m // tm
    n_pieces = 8
    rows = _piece_rows(D_MODEL, n_pieces)
    tile = lambda i: jnp.maximum(i - n_pieces, 0)
    row_spec = lambda width: pl.BlockSpec((tm, width), lambda i: (tile(i), 0))
    const = lambda r, c: pl.BlockSpec((r, c), lambda i: (0, 0))
    in_specs = [row_spec(D_MODEL), row_spec(D_MODEL),
                pl.BlockSpec((None, rows, D_MODEL), lambda i: (layer, jnp.minimum(i, n_pieces - 1), 0)),
                const(1, D_MODEL)]
    args = [mix, x, w_out, g]
    scratch = [pltpu.VMEM((D_MODEL, D_MODEL), BF16)]
    if router is None:
        out_shape = [jax.ShapeDtypeStruct((m, D_MODEL), F32), jax.ShapeDtypeStruct((m, D_MODEL), BF16)]
        out_specs = [row_spec(D_MODEL), row_spec(D_MODEL)]
    else:
        in_specs += [const(D_MODEL, ROUTER_PAD), const(1, ROUTER_PAD)]
        args += list(router)
        out_shape = [jax.ShapeDtypeStruct((m, D_MODEL), F32), jax.ShapeDtypeStruct((m, D_MODEL), F32),
                     jax.ShapeDtypeStruct((m, ROUTER_PAD), F32), jax.ShapeDtypeStruct((1, ROUTER_PAD), F32)]
        out_specs = [row_spec(D_MODEL), row_spec(D_MODEL), row_spec(ROUTER_PAD), const(1, ROUTER_PAD)]
        scratch += [pltpu.VMEM((tm, tm), BF16), pltpu.VMEM((1, ROUTER_PAD), F32)]
    return pl.pallas_call(
        functools.partial(_outproj_kernel, n_pieces=n_pieces, with_router=router is not None),
        grid=(n_pieces + nm,),
        in_specs=in_specs,
        out_specs=out_specs,
        out_shape=out_shape,
        scratch_shapes=scratch,
        compiler_params=_params(),
        name="outproj",
    )(*args)


EXPERT_TM = 512
EXPERT_PIECES = 4
ROUTE_TM = 256
SORTED_ROWS = BATCH * SEQ * 2 + N_EXPERTS * EXPERT_TM
EXPERT_STEPS = N_EXPERTS * EXPERT_PIECES + SORTED_ROWS // EXPERT_TM


def _take(table, idx):
    hot = idx[..., None] == jnp.arange(table.shape[0], dtype=idx.dtype)
    return jnp.sum(jnp.where(hot, table, 0), axis=-1)


def _route_plan(route, counts):
    i32 = jnp.int32
    t, p = EXPERT_TM, EXPERT_PIECES
    cnt = counts[0, :N_EXPERTS].astype(i32)
    ntile = (cnt + t - 1) // t
    tile0 = jnp.cumsum(ntile) - ntile
    experts = route[:, ROUTE_I1:ROUTE_I2 + 1].astype(i32)
    dest = _take(tile0, experts) * t + route[:, ROUTE_P1:ROUTE_P2 + 1].astype(i32)
    step0 = jnp.arange(N_EXPERTS, dtype=i32) * p + tile0
    s = jnp.arange(EXPERT_STEPS, dtype=i32)
    e = jnp.sum((s[:, None] >= step0[None, :]).astype(i32), axis=1) - 1
    k = s - _take(step0, e)
    load = k < p
    comp = jnp.logical_not(load) & (k - p < _take(ntile, e))
    n_tiles = jnp.sum(ntile)
    first = _take(tile0, e)
    spare = jnp.minimum(n_tiles + k - p - _take(ntile, e), SORTED_ROWS // t - 1)
    tile = jnp.where(comp, first + k - p, jnp.where(load, jnp.minimum(first, SORTED_ROWS // t - 1), spare))
    piece = jnp.where(load, k, p - 1)
    kind = jnp.where(load, 0, jnp.where(comp, 1, 2))
    nvalid = jnp.clip(_take(cnt, e) - (tile - first) * t, 0, t)
    dest = dest.reshape(dest.shape[0] // ROUTE_TM, 1, 2 * ROUTE_TM)
    n_all = SORTED_ROWS // t
    partial = jnp.where(ntile > 0, tile0 + ntile - 1, -1)
    unused = n_tiles + jnp.arange(N_EXPERTS, dtype=i32)
    pad_rows = jnp.concatenate([partial, jnp.where(unused < n_all, unused, -1)]) * t
    return dest, pad_rows, (e, piece, tile, kind, nvalid)


def _row_dmas(n, copies_of):
    def start(t, c):
        for cp in copies_of(t):
            cp.start()
        return c

    def wait(t, c):
        for cp in copies_of(t):
            cp.wait()
        return c

    lax.fori_loop(0, n, start, 0, unroll=8)
    lax.fori_loop(0, n, wait, 0, unroll=8)


def _dispatch_kernel(dest_ref, pad_ref, hn_hbm, xg_hbm, zeros, sem):
    step = pl.program_id(0)

    @pl.when(step == 0)
    def _():
        zeros[...] = jnp.zeros_like(zeros)

        def fill(j):
            row = pl.multiple_of(jnp.maximum(pad_ref[j], 0), EXPERT_TM)
            return pltpu.make_async_copy(zeros, xg_hbm.at[pl.ds(row, EXPERT_TM)], sem)

        for j in range(pad_ref.shape[0]):
            pl.when(pad_ref[j] >= 0)(lambda j=j: fill(j).start())
        for j in range(pad_ref.shape[0]):
            pl.when(pad_ref[j] >= 0)(lambda j=j: fill(j).wait())

    base = step * ROUTE_TM

    def copies_of(t):
        src = hn_hbm.at[pl.ds(base + t, 1)]
        return [pltpu.make_async_copy(src, xg_hbm.at[pl.ds(dest_ref[0, 2 * t + k], 1)], sem) for k in range(2)]

    _row_dmas(ROUTE_TM, copies_of)


def _dispatch(hn, dest, pad_rows):
    return pl.pallas_call(
        _dispatch_kernel,
        grid=(hn.shape[0] // ROUTE_TM,),
        in_specs=[pl.BlockSpec((None, 1, 2 * ROUTE_TM), lambda s: (s, 0, 0), memory_space=pltpu.SMEM),
                  pl.BlockSpec(memory_space=pltpu.SMEM),
                  pl.BlockSpec(memory_space=pl.ANY)],
        out_specs=pl.BlockSpec(memory_space=pl.ANY),
        out_shape=jax.ShapeDtypeStruct((SORTED_ROWS, D_MODEL), F32),
        scratch_shapes=[pltpu.VMEM((EXPERT_TM, D_MODEL), F32), pltpu.SemaphoreType.DMA],
        compiler_params=_params(),
        name="dispatch",
    )(dest, pad_rows, hn)


def _expert_kernel(st_e, st_p, st_t, st_kind, st_nv, x_ref, wg_ref, wu_ref, wd_ref, y_ref, wgb, wub, wdb):
    s = pl.program_id(0)
    kind = st_kind[s]

    @pl.when(kind == 0)
    def _():
        for w_ref, wbf in ((wg_ref, wgb), (wu_ref, wub), (wd_ref, wdb)):
            rows = w_ref.shape[0]
            r0 = pl.multiple_of(st_p[s] * rows, rows)
            wbf[pl.ds(r0, rows), :] = w_ref[...].astype(BF16)

    @pl.when(kind == 1)
    def _():
        row = lax.broadcasted_iota(jnp.int32, (x_ref.shape[0], 1), 0)
        x = jnp.where(row < st_nv[s], x_ref[...], 0.0).astype(BF16)
        gate = jnp.dot(x, wgb[...], preferred_element_type=F32)
        up = jnp.dot(x, wub[...], preferred_element_type=F32)
        h = (gate * (1.0 / (1.0 + jnp.exp(-gate)))) * up
        y_ref[...] = jnp.dot(h.astype(BF16), wdb[...], preferred_element_type=F32)

    @pl.when(kind == 2)
    def _():
        y_ref[...] = jnp.zeros_like(y_ref)


def _experts(xg, wg, wu, wd, idx, steps):
    t, p = EXPERT_TM, EXPERT_PIECES
    rows_in = _piece_rows(D_MODEL, p)
    rows_dn = _piece_rows(D_FF_EXPERT, p)
    up_spec = pl.BlockSpec((None, None, rows_in, D_FF_EXPERT), lambda s, e, pc, tl, kd, nv: (idx, e[s], pc[s], 0))
    row_spec = pl.BlockSpec((t, D_MODEL), lambda s, e, pc, tl, kd, nv: (tl[s], 0))
    return pl.pallas_call(
        _expert_kernel,
        grid_spec=pltpu.PrefetchScalarGridSpec(
            num_scalar_prefetch=5,
            grid=(EXPERT_STEPS,),
            in_specs=[row_spec, up_spec, up_spec,
                      pl.BlockSpec((None, None, rows_dn, D_MODEL),
                                   lambda s, e, pc, tl, kd, nv: (idx, e[s], pc[s], 0))],
            out_specs=row_spec,
            scratch_shapes=[pltpu.VMEM((D_MODEL, D_FF_EXPERT), BF16), pltpu.VMEM((D_MODEL, D_FF_EXPERT), BF16),
                            pltpu.VMEM((D_FF_EXPERT, D_MODEL), BF16)]),
        out_shape=jax.ShapeDtypeStruct((SORTED_ROWS, D_MODEL), F32),
        compiler_params=_params(),
        name="experts",
    )(*steps, xg, wg, wu, wd)


def _combine_kernel(dest_ref, x_ref, route_ref, *rest, final_norm):
    if final_norm:
        gf_ref, y_hbm, o_ref, ya, yb, sem = rest
    else:
        y_hbm, o_ref, ya, yb, sem = rest

    def copies_of(t):
        return [pltpu.make_async_copy(y_hbm.at[pl.ds(dest_ref[0, 2 * t + k], 1)], buf.at[pl.ds(t, 1)], sem)
                for k, buf in enumerate((ya, yb))]

    _row_dmas(x_ref.shape[0], copies_of)
    out = x_ref[...] + route_ref[:, ROUTE_G1:ROUTE_G1 + 1] * ya[...] + route_ref[:, ROUTE_G2:ROUTE_G2 + 1] * yb[...]
    if final_norm:
        out = _rms(out, gf_ref[...])
    o_ref[...] = out


def _combine(x, route, y, dest, g_final):
    m = x.shape[0]
    row_spec = lambda width: pl.BlockSpec((ROUTE_TM, width), lambda s: (s, 0))
    in_specs = [pl.BlockSpec((None, 1, 2 * ROUTE_TM), lambda s: (s, 0, 0), memory_space=pltpu.SMEM),
                row_spec(D_MODEL), row_spec(ROUTER_PAD)]
    args = [dest, x, route]
    if g_final is not None:
        in_specs.append(pl.BlockSpec((1, D_MODEL), lambda s: (0, 0)))
        args.append(g_final)
    return pl.pallas_call(
        functools.partial(_combine_kernel, final_norm=g_final is not None),
        grid=(m // ROUTE_TM,),
        in_specs=in_specs + [pl.BlockSpec(memory_space=pl.ANY)],
        out_specs=row_spec(D_MODEL),
        out_shape=jax.ShapeDtypeStruct((m, D_MODEL), F32),
        scratch_shapes=[pltpu.VMEM((ROUTE_TM, D_MODEL), F32), pltpu.VMEM((ROUTE_TM, D_MODEL), F32),
                        pltpu.SemaphoreType.DMA],
        compiler_params=_params(),
        name="combine",
    )(*args, y)


def _ffn_kernel(hn_ref, acc_ref, *rest, pieces, final_norm):
    rest = list(rest)
    gf_ref = rest.pop(0) if final_norm else None
    wg_ref, wu_ref, wd_ref, o_ref, wgb, wub, wdb = rest
    step = pl.program_id(0)
    _stash_piece(step, wg_ref, wgb, pieces[0])
    _stash_piece(step, wu_ref, wub, pieces[0])
    _stash_piece(step, wd_ref, wdb, pieces[1])

    @pl.when(step >= max(pieces))
    def _():
        hn = hn_ref[...]
        gate = jnp.dot(hn, wgb[...], preferred_element_type=F32)
        up = jnp.dot(hn, wub[...], preferred_element_type=F32)
        h = (gate * (1.0 / (1.0 + jnp.exp(-gate)))) * up
        out = acc_ref[...] + jnp.dot(h.astype(BF16), wdb[...], preferred_element_type=F32)
        if final_norm:
            out = _rms(out, gf_ref[...])
        o_ref[...] = out


def _ffn_slab(hn, acc, wg, wu, wd, idx, slab, width, *, tm, g_final=None):
    m = hn.shape[0]
    nm = m // tm
    pieces = (16, width // LANES)
    n_pro = max(pieces)
    rows_in = _piece_rows(D_MODEL, pieces[0])
    rows_dn = _piece_rows(width, pieces[1])
    tile = lambda i: jnp.maximum(i - n_pro, 0)
    row_spec = lambda w_: pl.BlockSpec((tm, w_), lambda i: (tile(i), 0))
    in_specs = [row_spec(D_MODEL), row_spec(D_MODEL)]
    args = [hn, acc]
    if g_final is not None:
        in_specs.append(pl.BlockSpec((1, D_MODEL), lambda i: (0, 0)))
        args.append(g_final)
    up_spec = pl.BlockSpec((None, rows_in, width), lambda i: (idx, jnp.minimum(i, pieces[0] - 1), slab))
    in_specs += [up_spec, up_spec,
                 pl.BlockSpec((None, rows_dn, D_MODEL),
                              lambda i: (idx, slab * pieces[1] + jnp.minimum(i, pieces[1] - 1), 0))]
    args += [wg, wu, wd]
    return pl.pallas_call(
        functools.partial(_ffn_kernel, pieces=pieces, final_norm=g_final is not None),
        grid=(n_pro + nm,),
        in_specs=in_specs,
        out_specs=row_spec(D_MODEL),
        out_shape=jax.ShapeDtypeStruct((m, D_MODEL), F32),
        scratch_shapes=[pltpu.VMEM((D_MODEL, width), BF16), pltpu.VMEM((D_MODEL, width), BF16),
                        pltpu.VMEM((width, D_MODEL), BF16)],
        compiler_params=_params(),
        name="ffn_slab",
    )(*args)


def _rope_tables(pos):
    half = ROPE_DIM // 2
    inv = ROPE_THETA ** (-jnp.arange(0, ROPE_DIM, 2, dtype=F32) / ROPE_DIM)
    ang = pos[:, None] * inv[None, :]
    cos, sin = jnp.cos(ang), jnp.sin(ang)
    n = pos.shape[0]
    rest = HEAD_DIM - ROPE_DIM
    a = jnp.concatenate([cos, cos, jnp.ones((n, rest), F32)], axis=1)
    b = jnp.concatenate([jnp.zeros((n, half), F32), sin, jnp.zeros((n, rest), F32)], axis=1)
    c = jnp.concatenate([-sin, jnp.zeros((n, half + rest), F32)], axis=1)
    reps = LANES // HEAD_DIM
    scale = HEAD_DIM ** -0.5
    return tuple(jnp.tile(t, (1, reps)) * scale for t in (a, b, c))


def kernel(x_prompt, x_sample, cache_k, cache_v, state_pool, norm_mix, w_in, attn_sinks, w_pool, pool_scale,
           w_out, norm_ffn, w_gate_dense, w_up_dense, w_down_dense, w_router, b_router, w_gate_exp, w_up_exp,
           w_down_exp, norm_final):
    tm = 512
    xp = x_prompt.reshape(BATCH * SEQ, D_MODEL)
    xs = x_sample.reshape(DEC_BATCH, D_MODEL)
    rope_p = _rope_tables(jnp.arange(SEQ, dtype=F32))
    rope_s = _rope_tables(jnp.full((DEC_BATCH,), PAST_LEN, dtype=F32))
    ck = cache_k.reshape(DEPTH, DEC_BATCH, WINDOW, KV_WIDTH)
    cv = cache_v.reshape(DEPTH, DEC_BATCH, WINDOW, KV_WIDTH)
    g_final = norm_final.reshape(1, D_MODEL)
    slab_w = D_FF // DENSE_SLABS
    heads = (N_KV_HEADS, HEAD_DIM)

    nk_p, nv_p, nu_p, nk_s, nv_s, nu_s = [], [], [], [], [], []
    for l in range(DEPTH):
        g_mix = norm_mix[l].reshape(1, D_MODEL)
        g_ffn = norm_ffn[l].reshape(1, D_MODEL)
        scale = pool_scale[l].reshape(1, POOL_WIDTH)
        moe = l % 2 == 1
        i = l // 2
        last = l == DEPTH - 1
        router = None
        if moe:
            router = (jnp.pad(w_router[i], ((0, 0), (0, ROUTER_PAD - N_EXPERTS))),
                      jnp.pad(b_router[i], (0, ROUTER_PAD - N_EXPERTS)).reshape(1, ROUTER_PAD))

        q, k, v, u, kvt, ut = _inproj(xp, g_mix, w_in, l, rope_p, tm=tm)
        mix = _mixer(q, k, v, u, attn_sinks[l], w_pool, scale, l)
        xp, hn, *routing = _outproj(mix, xp, w_out, g_ffn, l, router, tm=tm)
        nk_p.append(kvt[:, :KV_WIDTH].reshape(BATCH, WINDOW, *heads))
        nv_p.append(kvt[:, KV_WIDTH:].reshape(BATCH, WINDOW, *heads))
        nu_p.append(ut.reshape(BATCH, BF16_ROWS, POOL_WIDTH)[:, BF16_ROWS - POOL_STATE:])
        if moe:
            route, counts = routing
            dest, pad_rows, steps = _route_plan(route, counts)
            y = _experts(_dispatch(hn, dest, pad_rows), w_gate_exp, w_up_exp, w_down_exp, i, steps)
            xp = _combine(xp, route, y, dest, g_final if last else None)
        else:
            for s in range(DENSE_SLABS):
                xp = _ffn_slab(hn, xp, w_gate_dense, w_up_dense, w_down_dense, i, s, slab_w, tm=tm,
                               g_final=g_final if last and s == DENSE_SLABS - 1 else None)

        z = _sample_inproj(xs, g_mix, w_in, l, rope_s)
        a_s, nk, nv, nu, ps = _sample_mixer(
            z[:, :ATTN_WIDTH], z[:, ATTN_WIDTH:ATTN_WIDTH + KV_WIDTH],
            z[:, ATTN_WIDTH + KV_WIDTH:ATTN_WIDTH + 2 * KV_WIDTH], z[:, ATTN_WIDTH + 2 * KV_WIDTH:],
            ck, cv, state_pool, attn_sinks[l], w_pool, scale, l)
        mix_s = jnp.concatenate([a_s.reshape(DEC_BATCH, ATTN_WIDTH), ps], axis=1)
        xs = _sample_outproj(mix_s, xs, w_out, l)
        if moe:
            xs = _sample_ffn(xs, g_ffn, w_gate_exp, w_up_exp, w_down_exp, i, router=router,
                             g_final=g_final if last else None)
        else:
            xs = _sample_ffn(xs, g_ffn, w_gate_dense, w_up_dense, w_down_dense, i,
                             g_final=g_final if last else None)
        nk_s.append(nk.reshape(DEC_BATCH, WINDOW, *heads))
        nv_s.append(nv.reshape(DEC_BATCH, WINDOW, *heads))
        nu_s.append(nu)

    y_prompt = xp.reshape(BATCH, SEQ, D_MODEL)
    y_sample = xs.reshape(DEC_BATCH, 1, D_MODEL)
    return (y_prompt, y_sample, jnp.stack(nk_p), jnp.stack(nv_p), jnp.stack(nu_p),
            jnp.stack(nk_s), jnp.stack(nv_s), jnp.stack(nu_s))
```

```python
import functools

import jax
import jax.numpy as jnp
from jax import lax
from jax.experimental import pallas as pl
from jax.experimental.pallas import tpu as pltpu

F32 = jnp.float32
BF16 = jnp.bfloat16

D_MODEL = 2048
BATCH = 2
SEQ = 4096
DEPTH = 4
DEC_BATCH = 32
PAST_LEN = 16384
N_HEADS = 16
N_KV_HEADS = 4
HEAD_DIM = 64
GROUP = N_HEADS // N_KV_HEADS
ATTN_WIDTH = N_HEADS * HEAD_DIM
KV_WIDTH = N_KV_HEADS * HEAD_DIM
WINDOW = 128
ROPE_DIM = HEAD_DIM // 4
ROPE_THETA = 500000.0
POOL_WINDOWS = (2, 4, 8, 16)
POOL_WIDTH = D_MODEL - ATTN_WIDTH
POOL_GROUP_DIM = POOL_WIDTH // len(POOL_WINDOWS)
POOL_STATE = max(POOL_WINDOWS) - 1
IN_WIDTH = ATTN_WIDTH + 2 * KV_WIDTH + POOL_WIDTH
D_FF = 5632
N_EXPERTS = 8
D_FF_EXPERT = D_MODEL // 2
EPS = 1e-5
NEG_INF = -1e30

LANES = 128
BF16_ROWS = 16
VMEM_LIMIT = 56 * 1024 * 1024
DENSE_SLABS = 4
ROUTER_PAD = LANES


def _params(n_axes=1):
    return pltpu.CompilerParams(dimension_semantics=("arbitrary",) * n_axes,
                                vmem_limit_bytes=VMEM_LIMIT)


def _rms(x, g):
    ms = jnp.mean(x * x, axis=-1, keepdims=True)
    return (x * lax.rsqrt(ms + EPS)) * g


def _rope_mix(zc, ra, rb, rc):
    return zc * ra + pltpu.roll(zc, ROPE_DIM // 2, 1) * rb + pltpu.roll(zc, LANES - ROPE_DIM // 2, 1) * rc


def _split_bf16(a):
    hi = a.astype(BF16)
    return hi, (a - hi.astype(F32)).astype(BF16)


def _dot3(a, b, dims=(((1,), (0,)), ((), ()))):
    m = a.shape[0]
    ah, al = _split_bf16(a)
    bh, bl = _split_bf16(b)
    t = lax.dot_general(jnp.concatenate([ah, al], axis=0), bh, dims, preferred_element_type=F32)
    return t[:m] + t[m:] + lax.dot_general(ah, bl, dims, preferred_element_type=F32)


def _stash_piece(step, w_ref, wbf, n_pieces):
    rows = w_ref.shape[0]

    @pl.when(step < n_pieces)
    def _():
        r0 = pl.multiple_of(step * rows, rows)
        wbf[pl.ds(r0, rows), :] = w_ref[...].astype(BF16)


def _piece_rows(total_rows, n_pieces):
    rows, rem = divmod(total_rows, n_pieces)
    assert rem == 0 and rows % BF16_ROWS == 0, (total_rows, n_pieces)
    return rows


def _inproj_kernel(x_ref, g_ref, w_ref, ra_ref, rb_ref, rc_ref, q_ref, k_ref, v_ref, u_ref, kvt_ref, ut_ref,
                   wbf, *, n_pieces):
    step = pl.program_id(0)
    _stash_piece(step, w_ref, wbf, n_pieces)

    @pl.when(step >= n_pieces)
    def _():
        tm = x_ref.shape[0]
        h = _rms(x_ref[...], g_ref[...])
        z = jnp.dot(h.astype(BF16), wbf[...], preferred_element_type=F32)
        ra, rb, rc = ra_ref[...], rb_ref[...], rc_ref[...]
        n_rot = (ATTN_WIDTH + KV_WIDTH) // LANES
        rot = []
        for c in range(n_rot):
            r = _rope_mix(z[:, c * LANES:(c + 1) * LANES], ra, rb, rc)
            if c >= ATTN_WIDTH // LANES:
                r = r * (HEAD_DIM ** 0.5)
            rot.append(r)
        nq = ATTN_WIDTH // LANES
        for c in range(nq):
            q_ref[:, c * LANES:(c + 1) * LANES] = rot[c].astype(q_ref.dtype)
        for c in range(nq, n_rot):
            k_ref[:, (c - nq) * LANES:(c - nq + 1) * LANES] = rot[c].astype(k_ref.dtype)
        v = z[:, ATTN_WIDTH + KV_WIDTH:ATTN_WIDTH + 2 * KV_WIDTH]
        u = z[:, ATTN_WIDTH + 2 * KV_WIDTH:]
        v_ref[...] = v.astype(v_ref.dtype)
        u_ref[...] = u.astype(u_ref.dtype)
        for c in range(nq, n_rot):
            kvt_ref[:, (c - nq) * LANES:(c - nq + 1) * LANES] = rot[c][tm - WINDOW:, :]
        kvt_ref[:, KV_WIDTH:] = v[tm - WINDOW:, :]
        ut_ref[...] = u[tm - BF16_ROWS:, :]


def _inproj(x, g, w_in, layer, rope, *, tm):
    m = x.shape[0]
    nm = m // tm
    n_pieces = 8
    rows = _piece_rows(D_MODEL, n_pieces)
    tiles_per_seq = SEQ // tm
    tile = lambda i: jnp.maximum(i - n_pieces, 0)
    row_spec = lambda width: pl.BlockSpec((tm, width), lambda i: (tile(i), 0))
    rope_spec = pl.BlockSpec((tm, LANES), lambda i: (tile(i) % tiles_per_seq, 0))
    out_shape = [jax.ShapeDtypeStruct((m, ATTN_WIDTH), BF16),
                 jax.ShapeDtypeStruct((m, KV_WIDTH), BF16),
                 jax.ShapeDtypeStruct((m, KV_WIDTH), BF16),
                 jax.ShapeDtypeStruct((m, POOL_WIDTH), BF16),
                 jax.ShapeDtypeStruct((BATCH * WINDOW, 2 * KV_WIDTH), F32),
                 jax.ShapeDtypeStruct((BATCH * BF16_ROWS, POOL_WIDTH), F32)]
    out_specs = [row_spec(ATTN_WIDTH), row_spec(KV_WIDTH), row_spec(KV_WIDTH), row_spec(POOL_WIDTH),
                 pl.BlockSpec((WINDOW, 2 * KV_WIDTH), lambda i: (tile(i) // tiles_per_seq, 0)),
                 pl.BlockSpec((BF16_ROWS, POOL_WIDTH), lambda i: (tile(i) // tiles_per_seq, 0))]
    return pl.pallas_call(
        functools.partial(_inproj_kernel, n_pieces=n_pieces),
        grid=(n_pieces + nm,),
        in_specs=[row_spec(D_MODEL),
                  pl.BlockSpec((1, D_MODEL), lambda i: (0, 0)),
                  pl.BlockSpec((None, rows, IN_WIDTH), lambda i: (layer, jnp.minimum(i, n_pieces - 1), 0)),
                  rope_spec, rope_spec, rope_spec],
        out_specs=out_specs,
        out_shape=out_shape,
        scratch_shapes=[pltpu.VMEM((D_MODEL, IN_WIDTH), BF16)],
        compiler_params=_params(),
        name="inproj",
    )(x, g, w_in, *rope)


def _sink_softmax(s, sink, axis):
    mx = jnp.maximum(jnp.max(s, axis=axis, keepdims=True), sink)
    p = jnp.exp(s - mx)
    den = jnp.sum(p, axis=axis, keepdims=True) + jnp.exp(sink - mx)
    return p * (1.0 / den)


def _pool_group(d, wpb_g, scale_g):
    return jnp.dot(d.astype(BF16), wpb_g, preferred_element_type=F32) * scale_g


def _mixer_kernel(sink_ref, q_ref, kc_ref, kp_ref, vc_ref, vp_ref, uc_ref, up_ref, wp_ref, sc_ref,
                  o_ref, wpb, band):
    b = pl.program_id(0)
    n = pl.program_id(1)
    blk = WINDOW

    @pl.when((b == 0) & (n == 0))
    def _():
        wpb[...] = wp_ref[...].astype(BF16)
        tok = lax.broadcasted_iota(jnp.int32, (blk, 2 * blk), 0) + blk
        src = lax.broadcasted_iota(jnp.int32, (blk, 2 * blk), 1)
        for g, w in enumerate(POOL_WINDOWS):
            band[g] = jnp.where((src <= tok) & (src > tok - w), 1.0, 0.0).astype(BF16)

    half = LANES // 2
    kk = jnp.concatenate([kp_ref[...], kc_ref[...]], axis=0).astype(F32)
    vv = jnp.concatenate([vp_ref[...], vc_ref[...]], axis=0).astype(F32)
    lane = lax.broadcasted_iota(jnp.int32, (2 * blk, LANES), 1)
    low = lane < half
    key = lax.broadcasted_iota(jnp.int32, (4 * blk, blk), 0) & (2 * blk - 1)
    qry = lax.broadcasted_iota(jnp.int32, (4 * blk, blk), 1)
    valid = (key > qry) & (key <= qry + blk) & ((n > 0) | (key >= blk))

    for h in range(N_KV_HEADS):
        g0 = (h // 2) * LANES
        kg, vg = kk[:, g0:g0 + LANES], vv[:, g0:g0 + LANES]
        if h % 2 == 0:
            klo, vlo = jnp.where(low, kg, 0.0), jnp.where(low, vg, 0.0)
            khi, vhi = pltpu.roll(klo, half, 1), pltpu.roll(vlo, half, 1)
        else:
            khi, vhi = jnp.where(low, 0.0, kg), jnp.where(low, 0.0, vg)
            klo, vlo = pltpu.roll(khi, half, 1), pltpu.roll(vhi, half, 1)
        kcat = jnp.concatenate([klo, khi], axis=0).astype(BF16)
        vcat = jnp.concatenate([vlo, vhi], axis=0).astype(BF16)
        for pair in range(h * GROUP // 2, (h + 1) * GROUP // 2):
            qp = q_ref[:, pair * LANES:(pair + 1) * LANES]
            st = lax.dot_general(kcat, qp, (((1,), (1,)), ((), ())), preferred_element_type=F32)
            st = jnp.where(valid, st, NEG_INF)
            pt = [_sink_softmax(st[t * 2 * blk:(t + 1) * 2 * blk], sink_ref[2 * pair + t], 0).astype(BF16)
                  for t in range(2)]
            o = lax.dot_general(jnp.concatenate(pt, axis=0), vcat, (((0,), (0,)), ((), ())),
                                preferred_element_type=F32)
            o_ref[:, pair * LANES:(pair + 1) * LANES] = o.astype(o_ref.dtype)

    ext = jnp.concatenate([jnp.where(n > 0, up_ref[...], jnp.zeros_like(up_ref)), uc_ref[...]], axis=0)
    pos = (n * blk + lax.broadcasted_iota(jnp.int32, (blk, 1), 0)).astype(F32)
    for g, w in enumerate(POOL_WINDOWS):
        c0, c1 = g * POOL_GROUP_DIM, (g + 1) * POOL_GROUP_DIM
        wsum = jnp.dot(band[g], ext[:, c0:c1], preferred_element_type=F32)
        d = wsum * (1.0 / jnp.minimum(float(w), pos + 1.0)) - uc_ref[:, c0:c1].astype(F32)
        y = _pool_group(d, wpb[g], sc_ref[:, c0:c1])
        o_ref[:, ATTN_WIDTH + c0:ATTN_WIDTH + c1] = y.astype(o_ref.dtype)


def _mixer(q, k, v, u, sinks, w_pool, scale, layer):
    assert u.dtype == BF16
    nb = SEQ // WINDOW
    cur = lambda b, n: (b * nb + n, 0)
    prev = lambda b, n: (b * nb + jnp.maximum(n - 1, 0), 0)
    n_grp = len(POOL_WINDOWS)
    return pl.pallas_call(
        _mixer_kernel,
        grid=(BATCH, nb),
        in_specs=[pl.BlockSpec(memory_space=pltpu.SMEM),
                  pl.BlockSpec((WINDOW, ATTN_WIDTH), cur),
                  pl.BlockSpec((WINDOW, KV_WIDTH), cur), pl.BlockSpec((WINDOW, KV_WIDTH), prev),
                  pl.BlockSpec((WINDOW, KV_WIDTH), cur), pl.BlockSpec((WINDOW, KV_WIDTH), prev),
                  pl.BlockSpec((WINDOW, POOL_WIDTH), cur), pl.BlockSpec((WINDOW, POOL_WIDTH), prev),
                  pl.BlockSpec((None, n_grp, POOL_GROUP_DIM, POOL_GROUP_DIM), lambda b, n: (layer, 0, 0, 0)),
                  pl.BlockSpec((1, POOL_WIDTH), lambda b, n: (0, 0))],
        out_specs=pl.BlockSpec((WINDOW, D_MODEL), cur),
        out_shape=jax.ShapeDtypeStruct((BATCH * SEQ, D_MODEL), BF16),
        scratch_shapes=[pltpu.VMEM((n_grp, POOL_GROUP_DIM, POOL_GROUP_DIM), BF16),
                        pltpu.VMEM((n_grp, WINDOW, 2 * WINDOW), BF16)],
        compiler_params=_params(2),
        name="mixer",
    )(sinks, q, k, k, v, v, u, u, w_pool, scale)


def _sample_mixer_kernel(sink_ref, q_ref, kn_ref, vn_ref, un_ref, ck_ref, cv_ref, st_ref, wp_ref, sc_ref,
                         a_ref, nk_ref, nv_ref, nu_ref, po_ref, dscr):
    b = pl.program_id(0)
    nk_ref[0:WINDOW - 1, :] = ck_ref[1:WINDOW, :]
    nk_ref[WINDOW - 1:WINDOW, :] = kn_ref[...]
    nv_ref[0:WINDOW - 1, :] = cv_ref[1:WINDOW, :]
    nv_ref[WINDOW - 1:WINDOW, :] = vn_ref[...]
    keys = nk_ref[...]
    vals = nv_ref[...]

    q = q_ref[...]
    qe = jnp.concatenate([q] * N_KV_HEADS, axis=1)
    own = (lax.broadcasted_iota(jnp.int32, qe.shape, 0) // GROUP
           == lax.broadcasted_iota(jnp.int32, qe.shape, 1) // HEAD_DIM)
    qe = jnp.where(own, qe, 0.0)
    s = _dot3(qe, keys, (((1,), (1,)), ((), ())))
    r = _dot3(_sink_softmax(s, sink_ref[...], 1), vals)
    r = jnp.where(own, r, 0.0)
    o = r[:, 0:HEAD_DIM]
    for h in range(1, N_KV_HEADS):
        o = o + r[:, h * HEAD_DIM:(h + 1) * HEAD_DIM]
    a_ref[...] = o

    st = st_ref[...]
    un = un_ref[...]
    nu_ref[0:POOL_STATE - 1, :] = st[1:POOL_STATE, :]
    nu_ref[POOL_STATE - 1:POOL_STATE, :] = un
    for g, w in enumerate(POOL_WINDOWS):
        c0, c1 = g * POOL_GROUP_DIM, (g + 1) * POOL_GROUP_DIM
        tok = un[:, c0:c1]
        wsum = tok + jnp.sum(st[POOL_STATE - (w - 1):POOL_STATE, c0:c1], axis=0, keepdims=True)
        dscr[pl.ds(b, 1), c0:c1] = wsum / float(min(w, PAST_LEN + 1)) - tok

    @pl.when(b == pl.num_programs(0) - 1)
    def _():
        for g in range(len(POOL_WINDOWS)):
            c0, c1 = g * POOL_GROUP_DIM, (g + 1) * POOL_GROUP_DIM
            po_ref[:, c0:c1] = _dot3(dscr[:, c0:c1], wp_ref[g]) * sc_ref[:, c0:c1]


def _sample_mixer(q, k, v, u, cache_k, cache_v, state, sinks, w_pool, scale, layer):
    nb = DEC_BATCH
    per_b = lambda *shape: pl.BlockSpec((None,) + shape, lambda b: (b,) + (0,) * len(shape))
    per_lb = lambda *shape: pl.BlockSpec((None, None) + shape, lambda b: (layer, b) + (0,) * len(shape))
    return pl.pallas_call(
        _sample_mixer_kernel,
        grid=(nb,),
        in_specs=[pl.BlockSpec((N_HEADS, 1), lambda b: (0, 0)),
                  per_b(N_HEADS, HEAD_DIM), per_b(1, KV_WIDTH), per_b(1, KV_WIDTH), per_b(1, POOL_WIDTH),
                  per_lb(WINDOW, KV_WIDTH), per_lb(WINDOW, KV_WIDTH), per_lb(POOL_STATE, POOL_WIDTH),
                  pl.BlockSpec((None, len(POOL_WINDOWS), POOL_GROUP_DIM, POOL_GROUP_DIM),
                               lambda b: (layer, 0, 0, 0)),
                  pl.BlockSpec((1, POOL_WIDTH), lambda b: (0, 0))],
        out_specs=[per_b(N_HEADS, HEAD_DIM), per_b(WINDOW, KV_WIDTH), per_b(WINDOW, KV_WIDTH),
                   per_b(POOL_STATE, POOL_WIDTH), pl.BlockSpec((nb, POOL_WIDTH), lambda b: (0, 0))],
        out_shape=[jax.ShapeDtypeStruct((nb, N_HEADS, HEAD_DIM), F32),
                   jax.ShapeDtypeStruct((nb, WINDOW, KV_WIDTH), F32),
                   jax.ShapeDtypeStruct((nb, WINDOW, KV_WIDTH), F32),
                   jax.ShapeDtypeStruct((nb, POOL_STATE, POOL_WIDTH), F32),
                   jax.ShapeDtypeStruct((nb, POOL_WIDTH), F32)],
        scratch_shapes=[pltpu.VMEM((nb, POOL_WIDTH), F32)],
        compiler_params=_params(),
        name="sample_mixer",
    )(sinks.reshape(N_HEADS, 1), q.reshape(nb, N_HEADS, HEAD_DIM), k.reshape(nb, 1, KV_WIDTH),
      v.reshape(nb, 1, KV_WIDTH), u.reshape(nb, 1, POOL_WIDTH), cache_k, cache_v, state, w_pool, scale)


SAMPLE_TN = 512


def _sample_inproj_kernel(x_ref, g_ref, w_ref, ra_ref, rb_ref, rc_ref, z_ref):
    j = pl.program_id(0)
    z = _dot3(_rms(x_ref[...], g_ref[...]), w_ref[...])
    ra, rb, rc = ra_ref[...], rb_ref[...], rc_ref[...]
    per_tile = SAMPLE_TN // LANES
    for c in range(per_tile):
        zc = z[:, c * LANES:(c + 1) * LANES]
        grp = j * per_tile + c
        r = _rope_mix(zc, ra, rb, rc)
        r = jnp.where(grp >= ATTN_WIDTH // LANES, r * (HEAD_DIM ** 0.5), r)
        z_ref[:, c * LANES:(c + 1) * LANES] = jnp.where(grp < (ATTN_WIDTH + KV_WIDTH) // LANES, r, zc)


def _sample_inproj(x, g, w_in, layer, rope):
    m = x.shape[0]
    const = lambda r, c: pl.BlockSpec((r, c), lambda j: (0, 0))
    return pl.pallas_call(
        _sample_inproj_kernel,
        grid=(IN_WIDTH // SAMPLE_TN,),
        in_specs=[const(m, D_MODEL), const(1, D_MODEL),
                  pl.BlockSpec((None, D_MODEL, SAMPLE_TN), lambda j: (layer, 0, j)),
                  const(m, LANES), const(m, LANES), const(m, LANES)],
        out_specs=pl.BlockSpec((m, SAMPLE_TN), lambda j: (0, j)),
        out_shape=jax.ShapeDtypeStruct((m, IN_WIDTH), F32),
        compiler_params=_params(),
        name="sample_inproj",
    )(x, g, w_in, *rope)


def _sample_outproj_kernel(mix_ref, x_ref, w_ref, o_ref):
    o_ref[...] = x_ref[...] + _dot3(mix_ref[...], w_ref[...])


def _sample_outproj(mix, x, w_out, layer):
    m = x.shape[0]
    col = pl.BlockSpec((m, SAMPLE_TN), lambda j: (0, j))
    return pl.pallas_call(
        _sample_outproj_kernel,
        grid=(D_MODEL // SAMPLE_TN,),
        in_specs=[pl.BlockSpec((m, D_MODEL), lambda j: (0, 0)), col,
                  pl.BlockSpec((None, D_MODEL, SAMPLE_TN), lambda j: (layer, 0, j))],
        out_specs=col,
        out_shape=jax.ShapeDtypeStruct((m, D_MODEL), F32),
        compiler_params=_params(),
        name="sample_outproj",
    )(mix, x, w_out)


def _sample_ffn_kernel(x_ref, g_ref, *rest, moe, final_norm):
    rest = list(rest)
    wr_ref, br_ref = (rest.pop(0), rest.pop(0)) if moe else (None, None)
    gf_ref = rest.pop(0) if final_norm else None
    wg_ref, wu_ref, wd_ref, o_ref, acc, hn_scr, cmb = rest
    e, f = pl.program_id(0), pl.program_id(1)

    @pl.when((e == 0) & (f == 0))
    def _():
        hn = _rms(x_ref[...], g_ref[...])
        hn_scr[...] = hn
        acc[...] = jnp.zeros_like(acc)
        if moe:
            cmb[...] = _top2_gates(_dot3(hn, wr_ref[...]) + br_ref[...])

    hn = hn_scr[...]
    gate = _dot3(hn, wg_ref[...])
    up = _dot3(hn, wu_ref[...])
    h = (gate * (1.0 / (1.0 + jnp.exp(-gate)))) * up
    if moe:
        lane = lax.broadcasted_iota(jnp.int32, cmb.shape, 1)
        h = h * jnp.sum(jnp.where(lane == e, cmb[...], 0.0), axis=1, keepdims=True)
    acc[...] += _dot3(h, wd_ref[...])

    @pl.when((e == pl.num_programs(0) - 1) & (f == pl.num_programs(1) - 1))
    def _():
        out = x_ref[...] + acc[...]
        if final_norm:
            out = _rms(out, gf_ref[...])
        o_ref[...] = out


def _sample_ffn(x, g, wg, wu, wd, idx, *, router=None, g_final=None):
    m = x.shape[0]
    moe = router is not None
    n_e = N_EXPERTS if moe else 1
    width = wg.shape[-1]
    const = lambda r, c: pl.BlockSpec((r, c), lambda e, f: (0, 0))
    in_specs = [const(m, D_MODEL), const(1, D_MODEL)]
    args = [x, g]
    if moe:
        in_specs += [const(D_MODEL, ROUTER_PAD), const(1, ROUTER_PAD)]
        args += list(router)
        up_spec = pl.BlockSpec((None, None, D_MODEL, SAMPLE_TN), lambda e, f: (idx, e, 0, f))
        dn_spec = pl.BlockSpec((None, None, SAMPLE_TN, D_MODEL), lambda e, f: (idx, e, f, 0))
    else:
        up_spec = pl.BlockSpec((None, D_MODEL, SAMPLE_TN), lambda e, f: (idx, 0, f))
        dn_spec = pl.BlockSpec((None, SAMPLE_TN, D_MODEL), lambda e, f: (idx, f, 0))
    if g_final is not None:
        in_specs.append(const(1, D_MODEL))
        args.append(g_final)
    return pl.pallas_call(
        functools.partial(_sample_ffn_kernel, moe=moe, final_norm=g_final is not None),
        grid=(n_e, width // SAMPLE_TN),
        in_specs=in_specs + [up_spec, up_spec, dn_spec],
        out_specs=const(m, D_MODEL),
        out_shape=jax.ShapeDtypeStruct((m, D_MODEL), F32),
        scratch_shapes=[pltpu.VMEM((m, D_MODEL), F32), pltpu.VMEM((m, D_MODEL), F32),
                        pltpu.VMEM((m, ROUTER_PAD), F32)],
        compiler_params=_params(2),
        name="sample_ffn",
    )(*args, wg, wu, wd)


def _top2(logits):
    lane = lax.broadcasted_iota(jnp.int32, logits.shape, 1)
    lg = jnp.where(lane < N_EXPERTS, logits, -jnp.inf)
    m1 = jnp.max(lg, axis=1, keepdims=True)
    i1 = jnp.min(jnp.where(lg == m1, lane, ROUTER_PAD), axis=1, keepdims=True)
    lg2 = jnp.where(lane == i1, -jnp.inf, lg)
    m2 = jnp.max(lg2, axis=1, keepdims=True)
    i2 = jnp.min(jnp.where(lg2 == m2, lane, ROUTER_PAD), axis=1, keepdims=True)
    e2 = jnp.exp(m2 - m1)
    den = 1.0 + e2
    return i1, i2, 1.0 / den, e2 / den


def _top2_gates(logits):
    i1, i2, g1, g2 = _top2(logits)
    lane = lax.broadcasted_iota(jnp.int32, logits.shape, 1)
    return jnp.where(lane == i1, g1, 0.0) + jnp.where(lane == i2, g2, 0.0)


ROUTE_I1, ROUTE_I2, ROUTE_G1, ROUTE_G2, ROUTE_P1, ROUTE_P2 = range(6)


def _outproj_kernel(mix_ref, x_ref, w_ref, g_ref, *rest, n_pieces, with_router):
    if with_router:
        wr_ref, br_ref, xo_ref, hn_ref, route_ref, cnt_ref, wbf, tri, carry = rest
    else:
        xo_ref, hn_ref, wbf = rest
    step = pl.program_id(0)
    _stash_piece(step, w_ref, wbf, n_pieces)

    if with_router:
        @pl.when(step == 0)
        def _():
            row = lax.broadcasted_iota(jnp.int32, tri.shape, 0)
            col = lax.broadcasted_iota(jnp.int32, tri.shape, 1)
            tri[...] = jnp.where(col < row, 1.0, 0.0).astype(BF16)
            carry[...] = jnp.zeros_like(carry)

    @pl.when(step >= n_pieces)
    def _():
        y = jnp.dot(mix_ref[...], wbf[...], preferred_element_type=F32)
        xo = x_ref[...] + y
        xo_ref[...] = xo
        hn = _rms(xo, g_ref[...])
        if not with_router:
            hn_ref[...] = hn.astype(BF16)
            return
        hn_ref[...] = hn
        logits = jnp.dot(hn.astype(BF16), wr_ref[...].astype(BF16), preferred_element_type=F32) + br_ref[...]
        i1, i2, g1, g2 = _top2(logits)
        lane = lax.broadcasted_iota(jnp.int32, logits.shape, 1)
        picked = jnp.where((lane == i1) | (lane == i2), 1.0, 0.0)
        ahead = jnp.dot(tri[...], picked.astype(BF16), preferred_element_type=F32) + carry[...]
        p1 = jnp.sum(jnp.where(lane == i1, ahead, 0.0), axis=1, keepdims=True)
        p2 = jnp.sum(jnp.where(lane == i2, ahead, 0.0), axis=1, keepdims=True)
        carry[...] += jnp.sum(picked, axis=0, keepdims=True)
        cnt_ref[...] = carry[...]
        rec = jnp.zeros(logits.shape, F32)
        for k, val in ((ROUTE_I1, i1.astype(F32)), (ROUTE_I2, i2.astype(F32)), (ROUTE_G1, g1), (ROUTE_G2, g2),
                       (ROUTE_P1, p1), (ROUTE_P2, p2)):
            rec = jnp.where(lane == k, val, rec)
        route_ref[...] = rec


def _outproj(mix, x, w_out, g, layer, router, *, tm):
    m = x.shape[0]
    nm = m // tm
    n_pieces = 8
    rows = _piece_rows(D_MODEL, n_pieces)
    tile = lambda i: jnp.maximum(i - n_pieces, 0)
    row_spec = lambda width: pl.BlockSpec((tm, width), lambda i: (tile(i), 0))
    const = lambda r, c: pl.BlockSpec((r, c), lambda i: (0, 0))
    in_specs = [row_spec(D_MODEL), row_spec(D_MODEL),
                pl.BlockSpec((None, rows, D_MODEL), lambda i: (layer, jnp.minimum(i, n_pieces - 1), 0)),
                const(1, D_MODEL)]
    args = [mix, x, w_out, g]
    scratch = [pltpu.VMEM((D_MODEL, D_MODEL), BF16)]
    if router is None:
        out_shape = [jax.ShapeDtypeStruct((m, D_MODEL), F32), jax.ShapeDtypeStruct((m, D_MODEL), BF16)]
        out_specs = [row_spec(D_MODEL), row_spec(D_MODEL)]
    else:
        in_specs += [const(D_MODEL, ROUTER_PAD), const(1, ROUTER_PAD)]
        args += list(router)
        out_shape = [jax.ShapeDtypeStruct((m, D_MODEL), F32), jax.ShapeDtypeStruct((m, D_MODEL), F32),
                     jax.ShapeDtypeStruct((m, ROUTER_PAD), F32), jax.ShapeDtypeStruct((1, ROUTER_PAD), F32)]
        out_specs = [row_spec(D_MODEL), row_spec(D_MODEL), row_spec(ROUTER_PAD), const(1, ROUTER_PAD)]
        scratch += [pltpu.VMEM((tm, tm), BF16), pltpu.VMEM((1, ROUTER_PAD), F32)]
    return pl.pallas_call(
        functools.partial(_outproj_kernel, n_pieces=n_pieces, with_router=router is not None),
        grid=(n_pieces + nm,),
        in_specs=in_specs,
        out_specs=out_specs,
        out_shape=out_shape,
        scratch_shapes=scratch,
        compiler_params=_params(),
        name="outproj",
    )(*args)


EXPERT_TM = 512
EXPERT_PIECES = 4
ROUTE_TM = 256
SORTED_ROWS = BATCH * SEQ * 2 + N_EXPERTS * EXPERT_TM
EXPERT_STEPS = N_EXPERTS * EXPERT_PIECES + SORTED_ROWS // EXPERT_TM


def _take(table, idx):
    hot = idx[..., None] == jnp.arange(table.shape[0], dtype=idx.dtype)
    return jnp.sum(jnp.where(hot, table, 0), axis=-1)


def _route_plan(route, counts):
    i32 = jnp.int32
    t, p = EXPERT_TM, EXPERT_PIECES
    cnt = counts[0, :N_EXPERTS].astype(i32)
    ntile = (cnt + t - 1) // t
    tile0 = jnp.cumsum(ntile) - ntile
    experts = route[:, ROUTE_I1:ROUTE_I2 + 1].astype(i32)
    dest = _take(tile0, experts) * t + route[:, ROUTE_P1:ROUTE_P2 + 1].astype(i32)
    step0 = jnp.arange(N_EXPERTS, dtype=i32) * p + tile0
    s = jnp.arange(EXPERT_STEPS, dtype=i32)
    e = jnp.sum((s[:, None] >= step0[None, :]).astype(i32), axis=1) - 1
    k = s - _take(step0, e)
    load = k < p
    comp = jnp.logical_not(load) & (k - p < _take(ntile, e))
    n_tiles = jnp.sum(ntile)
    first = _take(tile0, e)
    spare = jnp.minimum(n_tiles + k - p - _take(ntile, e), SORTED_ROWS // t - 1)
    tile = jnp.where(comp, first + k - p, jnp.where(load, jnp.minimum(first, SORTED_ROWS // t - 1), spare))
    piece = jnp.where(load, k, p - 1)
    kind = jnp.where(load, 0, jnp.where(comp, 1, 2))
    nvalid = jnp.clip(_take(cnt, e) - (tile - first) * t, 0, t)
    dest = dest.reshape(dest.shape[0] // ROUTE_TM, 1, 2 * ROUTE_TM)
    n_all = SORTED_ROWS // t
    partial = jnp.where(ntile > 0, tile0 + ntile - 1, -1)
    unused = n_tiles + jnp.arange(N_EXPERTS, dtype=i32)
    pad_rows = jnp.concatenate([partial, jnp.where(unused < n_all, unused, -1)]) * t
    return dest, pad_rows, (e, piece, tile, kind, nvalid)


def _row_dmas(n, copies_of):
    def start(t, c):
        for cp in copies_of(t):
            cp.start()
        return c

    def wait(t, c):
        for cp in copies_of(t):
            cp.wait()
        return c

    lax.fori_loop(0, n, start, 0, unroll=8)
    lax.fori_loop(0, n, wait, 0, unroll=8)


def _dispatch_kernel(dest_ref, pad_ref, hn_ref, xg_hbm, zeros, sem):
    step = pl.program_id(0)

    @pl.when(step == 0)
    def _():
        zeros[...] = jnp.zeros_like(zeros)

        def fill(j):
            row = pl.multiple_of(jnp.maximum(pad_ref[j], 0), EXPERT_TM)
            return pltpu.make_async_copy(zeros, xg_hbm.at[pl.ds(row, EXPERT_TM)], sem)

        for j in range(pad_ref.shape[0]):
            pl.when(pad_ref[j] >= 0)(lambda j=j: fill(j).start())
        for j in range(pad_ref.shape[0]):
            pl.when(pad_ref[j] >= 0)(lambda j=j: fill(j).wait())

    def copies_of(t):
        src = hn_ref.at[pl.ds(t, 1)]
        return [pltpu.make_async_copy(src, xg_hbm.at[pl.ds(dest_ref[0, 2 * t + k], 1)], sem) for k in range(2)]

    _row_dmas(ROUTE_TM, copies_of)


def _dispatch(hn, dest, pad_rows):
    return pl.pallas_call(
        _dispatch_kernel,
        grid=(hn.shape[0] // ROUTE_TM,),
        in_specs=[pl.BlockSpec((None, 1, 2 * ROUTE_TM), lambda s: (s, 0, 0), memory_space=pltpu.SMEM),
                  pl.BlockSpec(memory_space=pltpu.SMEM),
                  pl.BlockSpec((ROUTE_TM, D_MODEL), lambda s: (s, 0))],
        out_specs=pl.BlockSpec(memory_space=pl.ANY),
        out_shape=jax.ShapeDtypeStruct((SORTED_ROWS, D_MODEL), F32),
        scratch_shapes=[pltpu.VMEM((EXPERT_TM, D_MODEL), F32), pltpu.SemaphoreType.DMA],
        compiler_params=_params(),
        name="dispatch",
    )(dest, pad_rows, hn)


def _expert_kernel(st_e, st_p, st_t, st_kind, st_nv, x_ref, wg_ref, wu_ref, wd_ref, y_ref, wgb, wub, wdb):
    s = pl.program_id(0)
    kind = st_kind[s]

    @pl.when(kind == 0)
    def _():
        for w_ref, wbf in ((wg_ref, wgb), (wu_ref, wub), (wd_ref, wdb)):
            rows = w_ref.shape[0]
            r0 = pl.multiple_of(st_p[s] * rows, rows)
            wbf[pl.ds(r0, rows), :] = w_ref[...].astype(BF16)

    @pl.when(kind == 1)
    def _():
        row = lax.broadcasted_iota(jnp.int32, (x_ref.shape[0], 1), 0)
        x = jnp.where(row < st_nv[s], x_ref[...], 0.0).astype(BF16)
        gate = jnp.dot(x, wgb[...], preferred_element_type=F32)
        up = jnp.dot(x, wub[...], preferred_element_type=F32)
        h = (gate * (1.0 / (1.0 + jnp.exp(-gate)))) * up
        y_ref[...] = jnp.dot(h.astype(BF16), wdb[...], preferred_element_type=F32)

    @pl.when(kind == 2)
    def _():
        y_ref[...] = jnp.zeros_like(y_ref)


def _experts(xg, wg, wu, wd, idx, steps):
    t, p = EXPERT_TM, EXPERT_PIECES
    rows_in = _piece_rows(D_MODEL, p)
    rows_dn = _piece_rows(D_FF_EXPERT, p)
    up_spec = pl.BlockSpec((None, None, rows_in, D_FF_EXPERT), lambda s, e, pc, tl, kd, nv: (idx, e[s], pc[s], 0))
    row_spec = pl.BlockSpec((t, D_MODEL), lambda s, e, pc, tl, kd, nv: (tl[s], 0))
    return pl.pallas_call(
        _expert_kernel,
        grid_spec=pltpu.PrefetchScalarGridSpec(
            num_scalar_prefetch=5,
            grid=(EXPERT_STEPS,),
            in_specs=[row_spec, up_spec, up_spec,
                      pl.BlockSpec((None, None, rows_dn, D_MODEL),
                                   lambda s, e, pc, tl, kd, nv: (idx, e[s], pc[s], 0))],
            out_specs=row_spec,
            scratch_shapes=[pltpu.VMEM((D_MODEL, D_FF_EXPERT), BF16), pltpu.VMEM((D_MODEL, D_FF_EXPERT), BF16),
                            pltpu.VMEM((D_FF_EXPERT, D_MODEL), BF16)]),
        out_shape=jax.ShapeDtypeStruct((SORTED_ROWS, D_MODEL), F32),
        compiler_params=_params(),
        name="experts",
    )(*steps, xg, wg, wu, wd)


def _combine_kernel(dest_ref, x_ref, route_ref, *rest, final_norm):
    if final_norm:
        gf_ref, y_hbm, o_ref, ya, yb, sem = rest
    else:
        y_hbm, o_ref, ya, yb, sem = rest

    def copies_of(t):
        return [pltpu.make_async_copy(y_hbm.at[pl.ds(dest_ref[0, 2 * t + k], 1)], buf.at[pl.ds(t, 1)], sem)
                for k, buf in enumerate((ya, yb))]

    _row_dmas(x_ref.shape[0], copies_of)
    out = x_ref[...] + route_ref[:, ROUTE_G1:ROUTE_G1 + 1] * ya[...] + route_ref[:, ROUTE_G2:ROUTE_G2 + 1] * yb[...]
    if final_norm:
        out = _rms(out, gf_ref[...])
    o_ref[...] = out


def _combine(x, route, y, dest, g_final):
    m = x.shape[0]
    row_spec = lambda width: pl.BlockSpec((ROUTE_TM, width), lambda s: (s, 0))
    in_specs = [pl.BlockSpec((None, 1, 2 * ROUTE_TM), lambda s: (s, 0, 0), memory_space=pltpu.SMEM),
                row_spec(D_MODEL), row_spec(ROUTER_PAD)]
    args = [dest, x, route]
    if g_final is not None:
        in_specs.append(pl.BlockSpec((1, D_MODEL), lambda s: (0, 0)))
        args.append(g_final)
    return pl.pallas_call(
        functools.partial(_combine_kernel, final_norm=g_final is not None),
        grid=(m // ROUTE_TM,),
        in_specs=in_specs + [pl.BlockSpec(memory_space=pl.ANY)],
        out_specs=row_spec(D_MODEL),
        out_shape=jax.ShapeDtypeStruct((m, D_MODEL), F32),
        scratch_shapes=[pltpu.VMEM((ROUTE_TM, D_MODEL), F32), pltpu.VMEM((ROUTE_TM, D_MODEL), F32),
                        pltpu.SemaphoreType.DMA],
        compiler_params=_params(),
        name="combine",
    )(*args, y)


def _ffn_kernel(hn_ref, acc_ref, *rest, pieces, final_norm):
    rest = list(rest)
    gf_ref = rest.pop(0) if final_norm else None
    wg_ref, wu_ref, wd_ref, o_ref, wgb, wub, wdb = rest
    step = pl.program_id(0)
    _stash_piece(step, wg_ref, wgb, pieces[0])
    _stash_piece(step, wu_ref, wub, pieces[0])
    _stash_piece(step, wd_ref, wdb, pieces[1])

    @pl.when(step >= max(pieces))
    def _():
        hn = hn_ref[...]
        gate = jnp.dot(hn, wgb[...], preferred_element_type=F32)
        up = jnp.dot(hn, wub[...], preferred_element_type=F32)
        h = (gate * (1.0 / (1.0 + jnp.exp(-gate)))) * up
        out = acc_ref[...] + jnp.dot(h.astype(BF16), wdb[...], preferred_element_type=F32)
        if final_norm:
            out = _rms(out, gf_ref[...])
        o_ref[...] = out


def _ffn_slab(hn, acc, wg, wu, wd, idx, slab, width, *, tm, g_final=None):
    m = hn.shape[0]
    nm = m // tm
    pieces = (16, width // LANES)
    n_pro = max(pieces)
    rows_in = _piece_rows(D_MODEL, pieces[0])
    rows_dn = _piece_rows(width, pieces[1])
    tile = lambda i: jnp.maximum(i - n_pro, 0)
    row_spec = lambda w_: pl.BlockSpec((tm, w_), lambda i: (tile(i), 0))
    in_specs = [row_spec(D_MODEL), row_spec(D_MODEL)]
    args = [hn, acc]
    if g_final is not None:
        in_specs.append(pl.BlockSpec((1, D_MODEL), lambda i: (0, 0)))
        args.append(g_final)
    up_spec = pl.BlockSpec((None, rows_in, width), lambda i: (idx, jnp.minimum(i, pieces[0] - 1), slab))
    in_specs += [up_spec, up_spec,
                 pl.BlockSpec((None, rows_dn, D_MODEL),
                              lambda i: (idx, slab * pieces[1] + jnp.minimum(i, pieces[1] - 1), 0))]
    args += [wg, wu, wd]
    return pl.pallas_call(
        functools.partial(_ffn_kernel, pieces=pieces, final_norm=g_final is not None),
        grid=(n_pro + nm,),
        in_specs=in_specs,
        out_specs=row_spec(D_MODEL),
        out_shape=jax.ShapeDtypeStruct((m, D_MODEL), F32),
        scratch_shapes=[pltpu.VMEM((D_MODEL, width), BF16), pltpu.VMEM((D_MODEL, width), BF16),
                        pltpu.VMEM((width, D_MODEL), BF16)],
        compiler_params=_params(),
        name="ffn_slab",
    )(*args)


def _rope_tables(pos):
    half = ROPE_DIM // 2
    inv = ROPE_THETA ** (-jnp.arange(0, ROPE_DIM, 2, dtype=F32) / ROPE_DIM)
    ang = pos[:, None] * inv[None, :]
    cos, sin = jnp.cos(ang), jnp.sin(ang)
    n = pos.shape[0]
    rest = HEAD_DIM - ROPE_DIM
    a = jnp.concatenate([cos, cos, jnp.ones((n, rest), F32)], axis=1)
    b = jnp.concatenate([jnp.zeros((n, half), F32), sin, jnp.zeros((n, rest), F32)], axis=1)
    c = jnp.concatenate([-sin, jnp.zeros((n, half + rest), F32)], axis=1)
    reps = LANES // HEAD_DIM
    scale = HEAD_DIM ** -0.5
    return tuple(jnp.tile(t, (1, reps)) * scale for t in (a, b, c))


def kernel(x_prompt, x_sample, cache_k, cache_v, state_pool, norm_mix, w_in, attn_sinks, w_pool, pool_scale,
           w_out, norm_ffn, w_gate_dense, w_up_dense, w_down_dense, w_router, b_router, w_gate_exp, w_up_exp,
           w_down_exp, norm_final):
    tm = 512
    xp = x_prompt.reshape(BATCH * SEQ, D_MODEL)
    xs = x_sample.reshape(DEC_BATCH, D_MODEL)
    rope_p = _rope_tables(jnp.arange(SEQ, dtype=F32))
    rope_s = _rope_tables(jnp.full((DEC_BATCH,), PAST_LEN, dtype=F32))
    ck = cache_k.reshape(DEPTH, DEC_BATCH, WINDOW, KV_WIDTH)
    cv = cache_v.reshape(DEPTH, DEC_BATCH, WINDOW, KV_WIDTH)
    g_final = norm_final.reshape(1, D_MODEL)
    slab_w = D_FF // DENSE_SLABS
    heads = (N_KV_HEADS, HEAD_DIM)

    nk_p, nv_p, nu_p, nk_s, nv_s, nu_s = [], [], [], [], [], []
    for l in range(DEPTH):
        g_mix = norm_mix[l].reshape(1, D_MODEL)
        g_ffn = norm_ffn[l].reshape(1, D_MODEL)
        scale = pool_scale[l].reshape(1, POOL_WIDTH)
        moe = l % 2 == 1
        i = l // 2
        last = l == DEPTH - 1
        router = None
        if moe:
            router = (jnp.pad(w_router[i], ((0, 0), (0, ROUTER_PAD - N_EXPERTS))),
                      jnp.pad(b_router[i], (0, ROUTER_PAD - N_EXPERTS)).reshape(1, ROUTER_PAD))

        q, k, v, u, kvt, ut = _inproj(xp, g_mix, w_in, l, rope_p, tm=tm)
        mix = _mixer(q, k, v, u, attn_sinks[l], w_pool, scale, l)
        xp, hn, *routing = _outproj(mix, xp, w_out, g_ffn, l, router, tm=tm)
        nk_p.append(kvt[:, :KV_WIDTH].reshape(BATCH, WINDOW, *heads))
        nv_p.append(kvt[:, KV_WIDTH:].reshape(BATCH, WINDOW, *heads))
        nu_p.append(ut.reshape(BATCH, BF16_ROWS, POOL_WIDTH)[:, BF16_ROWS - POOL_STATE:])
        if moe:
            route, counts = routing
            dest, pad_rows, steps = _route_plan(route, counts)
            y = _experts(_dispatch(hn, dest, pad_rows), w_gate_exp, w_up_exp, w_down_exp, i, steps)
            xp = _combine(xp, route, y, dest, g_final if last else None)
        else:
            for s in range(DENSE_SLABS):
                xp = _ffn_slab(hn, xp, w_gate_dense, w_up_dense, w_down_dense, i, s, slab_w, tm=tm,
                               g_final=g_final if last and s == DENSE_SLABS - 1 else None)

        z = _sample_inproj(xs, g_mix, w_in, l, rope_s)
        a_s, nk, nv, nu, ps = _sample_mixer(
            z[:, :ATTN_WIDTH], z[:, ATTN_WIDTH:ATTN_WIDTH + KV_WIDTH],
            z[:, ATTN_WIDTH + KV_WIDTH:ATTN_WIDTH + 2 * KV_WIDTH], z[:, ATTN_WIDTH + 2 * KV_WIDTH:],
            ck, cv, state_pool, attn_sinks[l], w_pool, scale, l)
        mix_s = jnp.concatenate([a_s.reshape(DEC_BATCH, ATTN_WIDTH), ps], axis=1)
        xs = _sample_outproj(mix_s, xs, w_out, l)
        if moe:
            xs = _sample_ffn(xs, g_ffn, w_gate_exp, w_up_exp, w_down_exp, i, router=router,
                             g_final=g_final if last else None)
        else:
            xs = _sample_ffn(xs, g_ffn, w_gate_dense, w_up_dense, w_down_dense, i,
                             g_final=g_final if last else None)
        nk_s.append(nk.reshape(DEC_BATCH, WINDOW, *heads))
        nv_s.append(nv.reshape(DEC_BATCH, WINDOW, *heads))
        nu_s.append(nu)

    y_prompt = xp.reshape(BATCH, SEQ, D_MODEL)
    y_sample = xs.reshape(DEC_BATCH, 1, D_MODEL)
    return (y_prompt, y_sample, jnp.stack(nk_p), jnp.stack(nv_p), jnp.stack(nu_p),
            jnp.stack(nk_s), jnp.stack(nv_s), jnp.stack(nu_s))
```

```python
import functools

import jax
import jax.numpy as jnp
from jax import lax
from jax.experimental import pallas as pl
from jax.experimental.pallas import tpu as pltpu

F32 = jnp.float32
BF16 = jnp.bfloat16

D_MODEL = 2048
BATCH = 2
SEQ = 4096
DEPTH = 4
DEC_BATCH = 32
PAST_LEN = 16384
N_HEADS = 16
N_KV_HEADS = 4
HEAD_DIM = 64
GROUP = N_HEADS // N_KV_HEADS
ATTN_WIDTH = N_HEADS * HEAD_DIM
KV_WIDTH = N_KV_HEADS * HEAD_DIM
WINDOW = 128
ROPE_DIM = HEAD_DIM // 4
ROPE_THETA = 500000.0
POOL_WINDOWS = (2, 4, 8, 16)
POOL_WIDTH = D_MODEL - ATTN_WIDTH
POOL_GROUP_DIM = POOL_WIDTH // len(POOL_WINDOWS)
POOL_STATE = max(POOL_WINDOWS) - 1
IN_WIDTH = ATTN_WIDTH + 2 * KV_WIDTH + POOL_WIDTH
D_FF = 5632
N_EXPERTS = 8
D_FF_EXPERT = D_MODEL // 2
EPS = 1e-5
NEG_INF = -1e30

LANES = 128
BF16_ROWS = 16
VMEM_LIMIT = 56 * 1024 * 1024
DENSE_SLABS = 4
ROUTER_PAD = LANES


def _params(n_axes=1):
    return pltpu.CompilerParams(dimension_semantics=("arbitrary",) * n_axes,
                                vmem_limit_bytes=VMEM_LIMIT)


def _rms(x, g):
    ms = jnp.mean(x * x, axis=-1, keepdims=True)
    return (x * lax.rsqrt(ms + EPS)) * g


def _rope_mix(zc, ra, rb, rc):
    return zc * ra + pltpu.roll(zc, ROPE_DIM // 2, 1) * rb + pltpu.roll(zc, LANES - ROPE_DIM // 2, 1) * rc


def _split_bf16(a):
    hi = a.astype(BF16)
    return hi, (a - hi.astype(F32)).astype(BF16)


def _dot3(a, b, dims=(((1,), (0,)), ((), ()))):
    m = a.shape[0]
    ah, al = _split_bf16(a)
    bh, bl = _split_bf16(b)
    t = lax.dot_general(jnp.concatenate([ah, al], axis=0), bh, dims, preferred_element_type=F32)
    return t[:m] + t[m:] + lax.dot_general(ah, bl, dims, preferred_element_type=F32)


def _stash_piece(step, w_ref, wbf, n_pieces):
    rows = w_ref.shape[0]

    @pl.when(step < n_pieces)
    def _():
        r0 = pl.multiple_of(step * rows, rows)
        wbf[pl.ds(r0, rows), :] = w_ref[...].astype(BF16)


def _piece_rows(total_rows, n_pieces):
    rows, rem = divmod(total_rows, n_pieces)
    assert rem == 0 and rows % BF16_ROWS == 0, (total_rows, n_pieces)
    return rows


def _inproj_kernel(x_ref, g_ref, w_ref, ra_ref, rb_ref, rc_ref, q_ref, k_ref, v_ref, u_ref, kvt_ref, ut_ref,
                   wbf, *, n_pieces):
    step = pl.program_id(0)
    _stash_piece(step, w_ref, wbf, n_pieces)

    @pl.when(step >= n_pieces)
    def _():
        tm = x_ref.shape[0]
        h = _rms(x_ref[...], g_ref[...])
        z = jnp.dot(h.astype(BF16), wbf[...], preferred_element_type=F32)
        ra, rb, rc = ra_ref[...], rb_ref[...], rc_ref[...]
        n_rot = (ATTN_WIDTH + KV_WIDTH) // LANES
        rot = []
        for c in range(n_rot):
            r = _rope_mix(z[:, c * LANES:(c + 1) * LANES], ra, rb, rc)
            if c >= ATTN_WIDTH // LANES:
                r = r * (HEAD_DIM ** 0.5)
            rot.append(r)
        nq = ATTN_WIDTH // LANES
        for c in range(nq):
            q_ref[:, c * LANES:(c + 1) * LANES] = rot[c].astype(q_ref.dtype)
        for c in range(nq, n_rot):
            k_ref[:, (c - nq) * LANES:(c - nq + 1) * LANES] = rot[c].astype(k_ref.dtype)
        v = z[:, ATTN_WIDTH + KV_WIDTH:ATTN_WIDTH + 2 * KV_WIDTH]
        u = z[:, ATTN_WIDTH + 2 * KV_WIDTH:]
        v_ref[...] = v.astype(v_ref.dtype)
        u_ref[...] = u.astype(u_ref.dtype)
        for c in range(nq, n_rot):
            kvt_ref[:, (c - nq) * LANES:(c - nq + 1) * LANES] = rot[c][tm - WINDOW:, :]
        kvt_ref[:, KV_WIDTH:] = v[tm - WINDOW:, :]
        ut_ref[...] = u[tm - BF16_ROWS:, :]


def _inproj(x, g, w_in, layer, rope, *, tm):
    m = x.shape[0]
    nm = m // tm
    n_pieces = 8
    rows = _piece_rows(D_MODEL, n_pieces)
    tiles_per_seq = SEQ // tm
    tile = lambda i: jnp.maximum(i - n_pieces, 0)
    row_spec = lambda width: pl.BlockSpec((tm, width), lambda i: (tile(i), 0))
    rope_spec = pl.BlockSpec((tm, LANES), lambda i: (tile(i) % tiles_per_seq, 0))
    out_shape = [jax.ShapeDtypeStruct((m, ATTN_WIDTH), BF16),
                 jax.ShapeDtypeStruct((m, KV_WIDTH), BF16),
                 jax.ShapeDtypeStruct((m, KV_WIDTH), BF16),
                 jax.ShapeDtypeStruct((m, POOL_WIDTH), BF16),
                 jax.ShapeDtypeStruct((BATCH * WINDOW, 2 * KV_WIDTH), F32),
                 jax.ShapeDtypeStruct((BATCH * BF16_ROWS, POOL_WIDTH), F32)]
    out_specs = [row_spec(ATTN_WIDTH), row_spec(KV_WIDTH), row_spec(KV_WIDTH), row_spec(POOL_WIDTH),
                 pl.BlockSpec((WINDOW, 2 * KV_WIDTH), lambda i: (tile(i) // tiles_per_seq, 0)),
                 pl.BlockSpec((BF16_ROWS, POOL_WIDTH), lambda i: (tile(i) // tiles_per_seq, 0))]
    return pl.pallas_call(
        functools.partial(_inproj_kernel, n_pieces=n_pieces),
        grid=(n_pieces + nm,),
        in_specs=[row_spec(D_MODEL),
                  pl.BlockSpec((1, D_MODEL), lambda i: (0, 0)),
                  pl.BlockSpec((None, rows, IN_WIDTH), lambda i: (layer, jnp.minimum(i, n_pieces - 1), 0)),
                  rope_spec, rope_spec, rope_spec],
        out_specs=out_specs,
        out_shape=out_shape,
        scratch_shapes=[pltpu.VMEM((D_MODEL, IN_WIDTH), BF16)],
        compiler_params=_params(),
        name="inproj",
    )(x, g, w_in, *rope)


def _sink_softmax(s, sink, axis):
    mx = jnp.maximum(jnp.max(s, axis=axis, keepdims=True), sink)
    p = jnp.exp(s - mx)
    den = jnp.sum(p, axis=axis, keepdims=True) + jnp.exp(sink - mx)
    return p * (1.0 / den)


def _pool_group(d, wpb_g, scale_g):
    return jnp.dot(d.astype(BF16), wpb_g, preferred_element_type=F32) * scale_g


def _mixer_kernel(sink_ref, q_ref, kc_ref, kp_ref, vc_ref, vp_ref, uc_ref, up_ref, wp_ref, sc_ref,
                  o_ref, wpb, band):
    b = pl.program_id(0)
    n = pl.program_id(1)
    blk = WINDOW

    @pl.when((b == 0) & (n == 0))
    def _():
        wpb[...] = wp_ref[...].astype(BF16)
        tok = lax.broadcasted_iota(jnp.int32, (blk, 2 * blk), 0) + blk
        src = lax.broadcasted_iota(jnp.int32, (blk, 2 * blk), 1)
        for g, w in enumerate(POOL_WINDOWS):
            band[g] = jnp.where((src <= tok) & (src > tok - w), 1.0, 0.0).astype(BF16)

    half = LANES // 2
    kk = jnp.concatenate([kp_ref[...], kc_ref[...]], axis=0).astype(F32)
    vv = jnp.concatenate([vp_ref[...], vc_ref[...]], axis=0).astype(F32)
    lane = lax.broadcasted_iota(jnp.int32, (2 * blk, LANES), 1)
    low = lane < half
    key = lax.broadcasted_iota(jnp.int32, (4 * blk, blk), 0) & (2 * blk - 1)
    qry = lax.broadcasted_iota(jnp.int32, (4 * blk, blk), 1)
    valid = (key > qry) & (key <= qry + blk) & ((n > 0) | (key >= blk))

    for h in range(N_KV_HEADS):
        g0 = (h // 2) * LANES
        kg, vg = kk[:, g0:g0 + LANES], vv[:, g0:g0 + LANES]
        if h % 2 == 0:
            klo, vlo = jnp.where(low, kg, 0.0), jnp.where(low, vg, 0.0)
            khi, vhi = pltpu.roll(klo, half, 1), pltpu.roll(vlo, half, 1)
        else:
            khi, vhi = jnp.where(low, 0.0, kg), jnp.where(low, 0.0, vg)
            klo, vlo = pltpu.roll(khi, half, 1), pltpu.roll(vhi, half, 1)
        kcat = jnp.concatenate([klo, khi], axis=0).astype(BF16)
        vcat = jnp.concatenate([vlo, vhi], axis=0).astype(BF16)
        for pair in range(h * GROUP // 2, (h + 1) * GROUP // 2):
            qp = q_ref[:, pair * LANES:(pair + 1) * LANES]
            st = lax.dot_general(kcat, qp, (((1,), (1,)), ((), ())), preferred_element_type=F32)
            st = jnp.where(valid, st, NEG_INF)
            pt = [_sink_softmax(st[t * 2 * blk:(t + 1) * 2 * blk], sink_ref[2 * pair + t], 0).astype(BF16)
                  for t in range(2)]
            o = lax.dot_general(jnp.concatenate(pt, axis=0), vcat, (((0,), (0,)), ((), ())),
                                preferred_element_type=F32)
            o_ref[:, pair * LANES:(pair + 1) * LANES] = o.astype(o_ref.dtype)

    ext = jnp.concatenate([jnp.where(n > 0, up_ref[...], jnp.zeros_like(up_ref)), uc_ref[...]], axis=0)
    pos = (n * blk + lax.broadcasted_iota(jnp.int32, (blk, 1), 0)).astype(F32)
    for g, w in enumerate(POOL_WINDOWS):
        c0, c1 = g * POOL_GROUP_DIM, (g + 1) * POOL_GROUP_DIM
        wsum = jnp.dot(band[g], ext[:, c0:c1], preferred_element_type=F32)
        d = wsum * (1.0 / jnp.minimum(float(w), pos + 1.0)) - uc_ref[:, c0:c1].astype(F32)
        y = _pool_group(d, wpb[g], sc_ref[:, c0:c1])
        o_ref[:, ATTN_WIDTH + c0:ATTN_WIDTH + c1] = y.astype(o_ref.dtype)


def _mixer(q, k, v, u, sinks, w_pool, scale, layer):
    assert u.dtype == BF16
    nb = SEQ // WINDOW
    cur = lambda b, n: (b * nb + n, 0)
    prev = lambda b, n: (b * nb + jnp.maximum(n - 1, 0), 0)
    n_grp = len(POOL_WINDOWS)
    return pl.pallas_call(
        _mixer_kernel,
        grid=(BATCH, nb),
        in_specs=[pl.BlockSpec(memory_space=pltpu.SMEM),
                  pl.BlockSpec((WINDOW, ATTN_WIDTH), cur),
                  pl.BlockSpec((WINDOW, KV_WIDTH), cur), pl.BlockSpec((WINDOW, KV_WIDTH), prev),
                  pl.BlockSpec((WINDOW, KV_WIDTH), cur), pl.BlockSpec((WINDOW, KV_WIDTH), prev),
                  pl.BlockSpec((WINDOW, POOL_WIDTH), cur), pl.BlockSpec((WINDOW, POOL_WIDTH), prev),
                  pl.BlockSpec((None, n_grp, POOL_GROUP_DIM, POOL_GROUP_DIM), lambda b, n: (layer, 0, 0, 0)),
                  pl.BlockSpec((1, POOL_WIDTH), lambda b, n: (0, 0))],
        out_specs=pl.BlockSpec((WINDOW, D_MODEL), cur),
        out_shape=jax.ShapeDtypeStruct((BATCH * SEQ, D_MODEL), BF16),
        scratch_shapes=[pltpu.VMEM((n_grp, POOL_GROUP_DIM, POOL_GROUP_DIM), BF16),
                        pltpu.VMEM((n_grp, WINDOW, 2 * WINDOW), BF16)],
        compiler_params=_params(2),
        name="mixer",
    )(sinks, q, k, k, v, v, u, u, w_pool, scale)


SAMPLE_SB = 4


def _sample_mixer_kernel(sink_ref, q_ref, kn_ref, vn_ref, un_ref, ck_ref, cv_ref, st_ref, wp_ref, sc_ref,
                         a_ref, nk_ref, nv_ref, nu_ref, po_ref, dscr):
    step = pl.program_id(0)
    sink = sink_ref[...]
    own = (lax.broadcasted_iota(jnp.int32, (N_HEADS, KV_WIDTH), 0) // GROUP
           == lax.broadcasted_iota(jnp.int32, (N_HEADS, KV_WIDTH), 1) // HEAD_DIM)
    for j in range(SAMPLE_SB):
        nk_ref[j, 0:WINDOW - 1, :] = ck_ref[j, 1:WINDOW, :]
        nk_ref[j, WINDOW - 1:WINDOW, :] = kn_ref[j]
        nv_ref[j, 0:WINDOW - 1, :] = cv_ref[j, 1:WINDOW, :]
        nv_ref[j, WINDOW - 1:WINDOW, :] = vn_ref[j]

        q = q_ref[j]
        qe = jnp.where(own, jnp.concatenate([q] * N_KV_HEADS, axis=1), 0.0)
        s = _dot3(qe, nk_ref[j], (((1,), (1,)), ((), ())))
        r = _dot3(_sink_softmax(s, sink, 1), nv_ref[j])
        r = jnp.where(own, r, 0.0)
        o = r[:, 0:HEAD_DIM]
        for h in range(1, N_KV_HEADS):
            o = o + r[:, h * HEAD_DIM:(h + 1) * HEAD_DIM]
        a_ref[j] = o

        st = st_ref[j]
        un = un_ref[j]
        nu_ref[j, 0:POOL_STATE - 1, :] = st[1:POOL_STATE, :]
        nu_ref[j, POOL_STATE - 1:POOL_STATE, :] = un
        for g, w in enumerate(POOL_WINDOWS):
            c0, c1 = g * POOL_GROUP_DIM, (g + 1) * POOL_GROUP_DIM
            tok = un[:, c0:c1]
            wsum = tok + jnp.sum(st[POOL_STATE - (w - 1):POOL_STATE, c0:c1], axis=0, keepdims=True)
            dscr[pl.ds(step * SAMPLE_SB + j, 1), c0:c1] = wsum / float(min(w, PAST_LEN + 1)) - tok

    @pl.when(step == pl.num_programs(0) - 1)
    def _():
        for g in range(len(POOL_WINDOWS)):
            c0, c1 = g * POOL_GROUP_DIM, (g + 1) * POOL_GROUP_DIM
            po_ref[:, c0:c1] = _dot3(dscr[:, c0:c1], wp_ref[g]) * sc_ref[:, c0:c1]


def _sample_mixer(q, k, v, u, cache_k, cache_v, state, sinks, w_pool, scale, layer):
    nb, sb = DEC_BATCH, SAMPLE_SB
    per_b = lambda *shape: pl.BlockSpec((sb,) + shape, lambda b: (b,) + (0,) * len(shape))
    per_lb = lambda *shape: pl.BlockSpec((None, sb) + shape, lambda b: (layer, b) + (0,) * len(shape))
    return pl.pallas_call(
        _sample_mixer_kernel,
        grid=(nb // sb,),
        in_specs=[pl.BlockSpec((N_HEADS, 1), lambda b: (0, 0)),
                  per_b(N_HEADS, HEAD_DIM), per_b(1, KV_WIDTH), per_b(1, KV_WIDTH), per_b(1, POOL_WIDTH),
                  per_lb(WINDOW, KV_WIDTH), per_lb(WINDOW, KV_WIDTH), per_lb(POOL_STATE, POOL_WIDTH),
                  pl.BlockSpec((None, len(POOL_WINDOWS), POOL_GROUP_DIM, POOL_GROUP_DIM),
                               lambda b: (layer, 0, 0, 0)),
                  pl.BlockSpec((1, POOL_WIDTH), lambda b: (0, 0))],
        out_specs=[per_b(N_HEADS, HEAD_DIM), per_b(WINDOW, KV_WIDTH), per_b(WINDOW, KV_WIDTH),
                   per_b(POOL_STATE, POOL_WIDTH), pl.BlockSpec((nb, POOL_WIDTH), lambda b: (0, 0))],
        out_shape=[jax.ShapeDtypeStruct((nb, N_HEADS, HEAD_DIM), F32),
                   jax.ShapeDtypeStruct((nb, WINDOW, KV_WIDTH), F32),
                   jax.ShapeDtypeStruct((nb, WINDOW, KV_WIDTH), F32),
                   jax.ShapeDtypeStruct((nb, POOL_STATE, POOL_WIDTH), F32),
                   jax.ShapeDtypeStruct((nb, POOL_WIDTH), F32)],
        scratch_shapes=[pltpu.VMEM((nb, POOL_WIDTH), F32)],
        compiler_params=_params(),
        name="sample_mixer",
    )(sinks.reshape(N_HEADS, 1), q.reshape(nb, N_HEADS, HEAD_DIM), k.reshape(nb, 1, KV_WIDTH),
      v.reshape(nb, 1, KV_WIDTH), u.reshape(nb, 1, POOL_WIDTH), cache_k, cache_v, state, w_pool, scale)


SAMPLE_TN = 512


def _sample_inproj_kernel(x_ref, g_ref, w_ref, ra_ref, rb_ref, rc_ref, z_ref):
    j = pl.program_id(0)
    z = _dot3(_rms(x_ref[...], g_ref[...]), w_ref[...])
    ra, rb, rc = ra_ref[...], rb_ref[...], rc_ref[...]
    per_tile = SAMPLE_TN // LANES
    for c in range(per_tile):
        zc = z[:, c * LANES:(c + 1) * LANES]
        grp = j * per_tile + c
        r = _rope_mix(zc, ra, rb, rc)
        r = jnp.where(grp >= ATTN_WIDTH // LANES, r * (HEAD_DIM ** 0.5), r)
        z_ref[:, c * LANES:(c + 1) * LANES] = jnp.where(grp < (ATTN_WIDTH + KV_WIDTH) // LANES, r, zc)


def _sample_inproj(x, g, w_in, layer, rope):
    m = x.shape[0]
    const = lambda r, c: pl.BlockSpec((r, c), lambda j: (0, 0))
    return pl.pallas_call(
        _sample_inproj_kernel,
        grid=(IN_WIDTH // SAMPLE_TN,),
        in_specs=[const(m, D_MODEL), const(1, D_MODEL),
                  pl.BlockSpec((None, D_MODEL, SAMPLE_TN), lambda j: (layer, 0, j)),
                  const(m, LANES), const(m, LANES), const(m, LANES)],
        out_specs=pl.BlockSpec((m, SAMPLE_TN), lambda j: (0, j)),
        out_shape=jax.ShapeDtypeStruct((m, IN_WIDTH), F32),
        compiler_params=_params(),
        name="sample_inproj",
    )(x, g, w_in, *rope)


def _sample_outproj_kernel(mix_ref, x_ref, w_ref, o_ref):
    o_ref[...] = x_ref[...] + _dot3(mix_ref[...], w_ref[...])


def _sample_outproj(mix, x, w_out, layer):
    m = x.shape[0]
    col = pl.BlockSpec((m, SAMPLE_TN), lambda j: (0, j))
    return pl.pallas_call(
        _sample_outproj_kernel,
        grid=(D_MODEL // SAMPLE_TN,),
        in_specs=[pl.BlockSpec((m, D_MODEL), lambda j: (0, 0)), col,
                  pl.BlockSpec((None, D_MODEL, SAMPLE_TN), lambda j: (layer, 0, j))],
        out_specs=col,
        out_shape=jax.ShapeDtypeStruct((m, D_MODEL), F32),
        compiler_params=_params(),
        name="sample_outproj",
    )(mix, x, w_out)


def _sample_ffn_kernel(x_ref, g_ref, *rest, moe, final_norm):
    rest = list(rest)
    wr_ref, br_ref = (rest.pop(0), rest.pop(0)) if moe else (None, None)
    gf_ref = rest.pop(0) if final_norm else None
    wg_ref, wu_ref, wd_ref, o_ref, acc, hn_scr, cmb = rest
    e, f = pl.program_id(0), pl.program_id(1)

    @pl.when((e == 0) & (f == 0))
    def _():
        hn = _rms(x_ref[...], g_ref[...])
        hn_scr[...] = hn
        acc[...] = jnp.zeros_like(acc)
        if moe:
            cmb[...] = _top2_gates(_dot3(hn, wr_ref[...]) + br_ref[...])

    hn = hn_scr[...]
    gate = _dot3(hn, wg_ref[...])
    up = _dot3(hn, wu_ref[...])
    h = (gate * (1.0 / (1.0 + jnp.exp(-gate)))) * up
    if moe:
        lane = lax.broadcasted_iota(jnp.int32, cmb.shape, 1)
        h = h * jnp.sum(jnp.where(lane == e, cmb[...], 0.0), axis=1, keepdims=True)
    acc[...] += _dot3(h, wd_ref[...])

    @pl.when((e == pl.num_programs(0) - 1) & (f == pl.num_programs(1) - 1))
    def _():
        out = x_ref[...] + acc[...]
        if final_norm:
            out = _rms(out, gf_ref[...])
        o_ref[...] = out


def _sample_ffn(x, g, wg, wu, wd, idx, *, router=None, g_final=None):
    m = x.shape[0]
    moe = router is not None
    n_e = N_EXPERTS if moe else 1
    width = wg.shape[-1]
    const = lambda r, c: pl.BlockSpec((r, c), lambda e, f: (0, 0))
    in_specs = [const(m, D_MODEL), const(1, D_MODEL)]
    args = [x, g]
    if moe:
        in_specs += [const(D_MODEL, ROUTER_PAD), const(1, ROUTER_PAD)]
        args += list(router)
        up_spec = pl.BlockSpec((None, None, D_MODEL, SAMPLE_TN), lambda e, f: (idx, e, 0, f))
        dn_spec = pl.BlockSpec((None, None, SAMPLE_TN, D_MODEL), lambda e, f: (idx, e, f, 0))
    else:
        up_spec = pl.BlockSpec((None, D_MODEL, SAMPLE_TN), lambda e, f: (idx, 0, f))
        dn_spec = pl.BlockSpec((None, SAMPLE_TN, D_MODEL), lambda e, f: (idx, f, 0))
    if g_final is not None:
        in_specs.append(const(1, D_MODEL))
        args.append(g_final)
    return pl.pallas_call(
        functools.partial(_sample_ffn_kernel, moe=moe, final_norm=g_final is not None),
        grid=(n_e, width // SAMPLE_TN),
        in_specs=in_specs + [up_spec, up_spec, dn_spec],
        out_specs=const(m, D_MODEL),
        out_shape=jax.ShapeDtypeStruct((m, D_MODEL), F32),
        scratch_shapes=[pltpu.VMEM((m, D_MODEL), F32), pltpu.VMEM((m, D_MODEL), F32),
                        pltpu.VMEM((m, ROUTER_PAD), F32)],
        compiler_params=_params(2),
        name="sample_ffn",
    )(*args, wg, wu, wd)


def _top2(logits):
    lane = lax.broadcasted_iota(jnp.int32, logits.shape, 1)
    lg = jnp.where(lane < N_EXPERTS, logits, -jnp.inf)
    m1 = jnp.max(lg, axis=1, keepdims=True)
    i1 = jnp.min(jnp.where(lg == m1, lane, ROUTER_PAD), axis=1, keepdims=True)
    lg2 = jnp.where(lane == i1, -jnp.inf, lg)
    m2 = jnp.max(lg2, axis=1, keepdims=True)
    i2 = jnp.min(jnp.where(lg2 == m2, lane, ROUTER_PAD), axis=1, keepdims=True)
    e2 = jnp.exp(m2 - m1)
    den = 1.0 + e2
    return i1, i2, 1.0 / den, e2 / den


def _top2_gates(logits):
    i1, i2, g1, g2 = _top2(logits)
    lane = lax.broadcasted_iota(jnp.int32, logits.shape, 1)
    return jnp.where(lane == i1, g1, 0.0) + jnp.where(lane == i2, g2, 0.0)


ROUTE_I1, ROUTE_I2, ROUTE_G1, ROUTE_G2, ROUTE_P1, ROUTE_P2 = range(6)


def _outproj_kernel(mix_ref, x_ref, w_ref, g_ref, *rest, n_pieces, with_router):
    if with_router:
        wr_ref, br_ref, xo_ref, hn_ref, route_ref, cnt_ref, wbf, tri, carry = rest
    else:
        xo_ref, hn_ref, wbf = rest
    step = pl.program_id(0)
    _stash_piece(step, w_ref, wbf, n_pieces)

    if with_router:
        @pl.when(step == 0)
        def _():
            row = lax.broadcasted_iota(jnp.int32, tri.shape, 0)
            col = lax.broadcasted_iota(jnp.int32, tri.shape, 1)
            tri[...] = jnp.where(col < row, 1.0, 0.0).astype(BF16)
            carry[...] = jnp.zeros_like(carry)

    @pl.when(step >= n_pieces)
    def _():
        y = jnp.dot(mix_ref[...], wbf[...], preferred_element_type=F32)
        xo = x_ref[...] + y
        xo_ref[...] = xo
        hn = _rms(xo, g_ref[...])
        if not with_router:
            hn_ref[...] = hn.astype(BF16)
            return
        hn_ref[...] = hn
        logits = jnp.dot(hn.astype(BF16), wr_ref[...].astype(BF16), preferred_element_type=F32) + br_ref[...]
        i1, i2, g1, g2 = _top2(logits)
        lane = lax.broadcasted_iota(jnp.int32, logits.shape, 1)
        picked = jnp.where((lane == i1) | (lane == i2), 1.0, 0.0)
        ahead = jnp.dot(tri[...], picked.astype(BF16), preferred_element_type=F32) + carry[...]
        p1 = jnp.sum(jnp.where(lane == i1, ahead, 0.0), axis=1, keepdims=True)
        p2 = jnp.sum(jnp.where(lane == i2, ahead, 0.0), axis=1, keepdims=True)
        carry[...] += jnp.sum(picked, axis=0, keepdims=True)
        cnt_ref[...] = carry[...]
        rec = jnp.zeros(logits.shape, F32)
        for k, val in ((ROUTE_I1, i1.astype(F32)), (ROUTE_I2, i2.astype(F32)), (ROUTE_G1, g1), (ROUTE_G2, g2),
                       (ROUTE_P1, p1), (ROUTE_P2, p2)):
            rec = jnp.where(lane == k, val, rec)
        route_ref[...] = rec


def _outproj(mix, x, w_out, g, layer, router, *, tm):
    m = x.shape[0]
    nm = m // tm
    n_pieces = 8
    rows = _piece_rows(D_MODEL, n_pieces)
    tile = lambda i: jnp.maximum(i - n_pieces, 0)
    row_spec = lambda width: pl.BlockSpec((tm, width), lambda i: (tile(i), 0))
    const = lambda r, c: pl.BlockSpec((r, c), lambda i: (0, 0))
    in_specs = [row_spec(D_MODEL), row_spec(D_MODEL),
                pl.BlockSpec((None, rows, D_MODEL), lambda i: (layer, jnp.minimum(i, n_pieces - 1), 0)),
                const(1, D_MODEL)]
    args = [mix, x, w_out, g]
    scratch = [pltpu.VMEM((D_MODEL, D_MODEL), BF16)]
    if router is None:
        out_shape = [jax.ShapeDtypeStruct((m, D_MODEL), F32), jax.ShapeDtypeStruct((m, D_MODEL), BF16)]
        out_specs = [row_spec(D_MODEL), row_spec(D_MODEL)]
    else:
        in_specs += [const(D_MODEL, ROUTER_PAD), const(1, ROUTER_PAD)]
        args += list(router)
        out_shape = [jax.ShapeDtypeStruct((m, D_MODEL), F32), jax.ShapeDtypeStruct((m, D_MODEL), F32),
                     jax.ShapeDtypeStruct((m, ROUTER_PAD), F32), jax.ShapeDtypeStruct((1, ROUTER_PAD), F32)]
        out_specs = [row_spec(D_MODEL), row_spec(D_MODEL), row_spec(ROUTER_PAD), const(1, ROUTER_PAD)]
        scratch += [pltpu.VMEM((tm, tm), BF16), pltpu.VMEM((1, ROUTER_PAD), F32)]
    return pl.pallas_call(
        functools.partial(_outproj_kernel, n_pieces=n_pieces, with_router=router is not None),
        grid=(n_pieces + nm,),
        in_specs=in_specs,
        out_specs=out_specs,
        out_shape=out_shape,
        scratch_shapes=scratch,
        compiler_params=_params(),
        name="outproj",
    )(*args)


EXPERT_TM = 512
EXPERT_PIECES = 4
ROUTE_TM = 256
SORTED_ROWS = BATCH * SEQ * 2 + N_EXPERTS * EXPERT_TM
EXPERT_STEPS = N_EXPERTS * EXPERT_PIECES + SORTED_ROWS // EXPERT_TM


def _take(table, idx):
    hot = idx[..., None] == jnp.arange(table.shape[0], dtype=idx.dtype)
    return jnp.sum(jnp.where(hot, table, 0), axis=-1)


def _route_plan(route, counts):
    i32 = jnp.int32
    t, p = EXPERT_TM, EXPERT_PIECES
    cnt = counts[0, :N_EXPERTS].astype(i32)
    ntile = (cnt + t - 1) // t
    tile0 = jnp.cumsum(ntile) - ntile
    experts = route[:, ROUTE_I1:ROUTE_I2 + 1].astype(i32)
    dest = _take(tile0, experts) * t + route[:, ROUTE_P1:ROUTE_P2 + 1].astype(i32)
    step0 = jnp.arange(N_EXPERTS, dtype=i32) * p + tile0
    s = jnp.arange(EXPERT_STEPS, dtype=i32)
    e = jnp.sum((s[:, None] >= step0[None, :]).astype(i32), axis=1) - 1
    k = s - _take(step0, e)
    load = k < p
    comp = jnp.logical_not(load) & (k - p < _take(ntile, e))
    n_tiles = jnp.sum(ntile)
    first = _take(tile0, e)
    spare = jnp.minimum(n_tiles + k - p - _take(ntile, e), SORTED_ROWS // t - 1)
    tile = jnp.where(comp, first + k - p, jnp.where(load, jnp.minimum(first, SORTED_ROWS // t - 1), spare))
    piece = jnp.where(load, k, p - 1)
    kind = jnp.where(load, 0, jnp.where(comp, 1, 2))
    nvalid = jnp.clip(_take(cnt, e) - (tile - first) * t, 0, t)
    dest = dest.reshape(dest.shape[0] // ROUTE_TM, 1, 2 * ROUTE_TM)
    n_all = SORTED_ROWS // t
    partial = jnp.where(ntile > 0, tile0 + ntile - 1, -1)
    unused = n_tiles + jnp.arange(N_EXPERTS, dtype=i32)
    pad_rows = jnp.concatenate([partial, jnp.where(unused < n_all, unused, -1)]) * t
    return dest, pad_rows, (e, piece, tile, kind, nvalid)


def _start_rows(n, copies_of):
    def start(t, c):
        for cp in copies_of(t):
            cp.start()
        return c

    lax.fori_loop(0, n, start, 0, unroll=8)


def _wait_rows(n, copies_of):
    def wait(t, c):
        for cp in copies_of(t):
            cp.wait()
        return c

    lax.fori_loop(0, n, wait, 0, unroll=8)


def _dispatch_kernel(dest_ref, pad_ref, hn_ref, xg_hbm, zeros, sem):
    step = pl.program_id(0)

    @pl.when(step == 0)
    def _():
        zeros[...] = jnp.zeros_like(zeros)

        def fill(j):
            row = pl.multiple_of(jnp.maximum(pad_ref[j], 0), EXPERT_TM)
            return pltpu.make_async_copy(zeros, xg_hbm.at[pl.ds(row, EXPERT_TM)], sem)

        for j in range(pad_ref.shape[0]):
            pl.when(pad_ref[j] >= 0)(lambda j=j: fill(j).start())
        for j in range(pad_ref.shape[0]):
            pl.when(pad_ref[j] >= 0)(lambda j=j: fill(j).wait())

    def copies_of(t):
        src = hn_ref.at[pl.ds(t, 1)]
        return [pltpu.make_async_copy(src, xg_hbm.at[pl.ds(dest_ref[0, 2 * t + k], 1)], sem) for k in range(2)]

    _start_rows(ROUTE_TM, copies_of)
    _wait_rows(ROUTE_TM, copies_of)


def _dispatch(hn, dest, pad_rows):
    return pl.pallas_call(
        _dispatch_kernel,
        grid=(hn.shape[0] // ROUTE_TM,),
        in_specs=[pl.BlockSpec((None, 1, 2 * ROUTE_TM), lambda s: (s, 0, 0), memory_space=pltpu.SMEM),
                  pl.BlockSpec(memory_space=pltpu.SMEM),
                  pl.BlockSpec((ROUTE_TM, D_MODEL), lambda s: (s, 0))],
        out_specs=pl.BlockSpec(memory_space=pl.ANY),
        out_shape=jax.ShapeDtypeStruct((SORTED_ROWS, D_MODEL), F32),
        scratch_shapes=[pltpu.VMEM((EXPERT_TM, D_MODEL), F32), pltpu.SemaphoreType.DMA],
        compiler_params=_params(),
        name="dispatch",
    )(dest, pad_rows, hn)


def _expert_kernel(st_e, st_p, st_t, st_kind, st_nv, x_ref, wg_ref, wu_ref, wd_ref, y_ref, wgb, wub, wdb):
    s = pl.program_id(0)
    kind = st_kind[s]

    @pl.when(kind == 0)
    def _():
        for w_ref, wbf in ((wg_ref, wgb), (wu_ref, wub), (wd_ref, wdb)):
            rows = w_ref.shape[0]
            r0 = pl.multiple_of(st_p[s] * rows, rows)
            wbf[pl.ds(r0, rows), :] = w_ref[...].astype(BF16)

    @pl.when(kind == 1)
    def _():
        row = lax.broadcasted_iota(jnp.int32, (x_ref.shape[0], 1), 0)
        x = jnp.where(row < st_nv[s], x_ref[...], 0.0).astype(BF16)
        gate = jnp.dot(x, wgb[...], preferred_element_type=F32)
        up = jnp.dot(x, wub[...], preferred_element_type=F32)
        h = (gate * (1.0 / (1.0 + jnp.exp(-gate)))) * up
        y_ref[...] = jnp.dot(h.astype(BF16), wdb[...], preferred_element_type=F32)

    @pl.when(kind == 2)
    def _():
        y_ref[...] = jnp.zeros_like(y_ref)


def _experts(xg, wg, wu, wd, idx, steps):
    t, p = EXPERT_TM, EXPERT_PIECES
    rows_in = _piece_rows(D_MODEL, p)
    rows_dn = _piece_rows(D_FF_EXPERT, p)
    up_spec = pl.BlockSpec((None, None, rows_in, D_FF_EXPERT), lambda s, e, pc, tl, kd, nv: (idx, e[s], pc[s], 0))
    row_spec = pl.BlockSpec((t, D_MODEL), lambda s, e, pc, tl, kd, nv: (tl[s], 0))
    return pl.pallas_call(
        _expert_kernel,
        grid_spec=pltpu.PrefetchScalarGridSpec(
            num_scalar_prefetch=5,
            grid=(EXPERT_STEPS,),
            in_specs=[row_spec, up_spec, up_spec,
                      pl.BlockSpec((None, None, rows_dn, D_MODEL),
                                   lambda s, e, pc, tl, kd, nv: (idx, e[s], pc[s], 0))],
            out_specs=row_spec,
            scratch_shapes=[pltpu.VMEM((D_MODEL, D_FF_EXPERT), BF16), pltpu.VMEM((D_MODEL, D_FF_EXPERT), BF16),
                            pltpu.VMEM((D_FF_EXPERT, D_MODEL), BF16)]),
        out_shape=jax.ShapeDtypeStruct((SORTED_ROWS, D_MODEL), F32),
        compiler_params=_params(),
        name="experts",
    )(*steps, xg, wg, wu, wd)


def _combine_kernel(dest_ref, next_ref, x_ref, route_ref, *rest, final_norm):
    if final_norm:
        gf_ref, y_hbm, o_ref, ya, yb, sem = rest
    else:
        y_hbm, o_ref, ya, yb, sem = rest
    step = pl.program_id(0)
    n = x_ref.shape[0]
    slot = step % 2

    def gather(idx_ref, slot):
        def copies_of(t):
            return [pltpu.make_async_copy(y_hbm.at[pl.ds(idx_ref[0, 2 * t + k], 1)], buf.at[slot, pl.ds(t, 1)],
                                          sem.at[slot])
                    for k, buf in enumerate((ya, yb))]
        return copies_of

    @pl.when(step == 0)
    def _():
        _start_rows(n, gather(dest_ref, 0))

    @pl.when(step + 1 < pl.num_programs(0))
    def _():
        _start_rows(n, gather(next_ref, 1 - slot))

    _wait_rows(n, gather(dest_ref, slot))
    out = (x_ref[...] + route_ref[:, ROUTE_G1:ROUTE_G1 + 1] * ya[slot]
           + route_ref[:, ROUTE_G2:ROUTE_G2 + 1] * yb[slot])
    if final_norm:
        out = _rms(out, gf_ref[...])
    o_ref[...] = out


def _combine(x, route, y, dest, g_final):
    m = x.shape[0]
    n_steps = m // ROUTE_TM
    row_spec = lambda width: pl.BlockSpec((ROUTE_TM, width), lambda s: (s, 0))
    idx_spec = lambda ahead: pl.BlockSpec((None, 1, 2 * ROUTE_TM),
                                          lambda s: (jnp.minimum(s + ahead, n_steps - 1), 0, 0),
                                          memory_space=pltpu.SMEM)
    in_specs = [idx_spec(0), idx_spec(1), row_spec(D_MODEL), row_spec(ROUTER_PAD)]
    args = [dest, dest, x, route]
    if g_final is not None:
        in_specs.append(pl.BlockSpec((1, D_MODEL), lambda s: (0, 0)))
        args.append(g_final)
    return pl.pallas_call(
        functools.partial(_combine_kernel, final_norm=g_final is not None),
        grid=(n_steps,),
        in_specs=in_specs + [pl.BlockSpec(memory_space=pl.ANY)],
        out_specs=row_spec(D_MODEL),
        out_shape=jax.ShapeDtypeStruct((m, D_MODEL), F32),
        scratch_shapes=[pltpu.VMEM((2, ROUTE_TM, D_MODEL), F32), pltpu.VMEM((2, ROUTE_TM, D_MODEL), F32),
                        pltpu.SemaphoreType.DMA((2,))],
        compiler_params=_params(),
        name="combine",
    )(*args, y)


def _ffn_kernel(hn_ref, acc_ref, *rest, pieces, final_norm):
    rest = list(rest)
    gf_ref = rest.pop(0) if final_norm else None
    wg_ref, wu_ref, wd_ref, o_ref, wgb, wub, wdb = rest
    step = pl.program_id(0)
    _stash_piece(step, wg_ref, wgb, pieces[0])
    _stash_piece(step, wu_ref, wub, pieces[0])
    _stash_piece(step, wd_ref, wdb, pieces[1])

    @pl.when(step >= max(pieces))
    def _():
        hn = hn_ref[...]
        gate = jnp.dot(hn, wgb[...], preferred_element_type=F32)
        up = jnp.dot(hn, wub[...], preferred_element_type=F32)
        h = (gate * (1.0 / (1.0 + jnp.exp(-gate)))) * up
        out = acc_ref[...] + jnp.dot(h.astype(BF16), wdb[...], preferred_element_type=F32)
        if final_norm:
            out = _rms(out, gf_ref[...])
        o_ref[...] = out


def _ffn_slab(hn, acc, wg, wu, wd, idx, slab, width, *, tm, g_final=None):
    m = hn.shape[0]
    nm = m // tm
    pieces = (16, width // LANES)
    n_pro = max(pieces)
    rows_in = _piece_rows(D_MODEL, pieces[0])
    rows_dn = _piece_rows(width, pieces[1])
    tile = lambda i: jnp.maximum(i - n_pro, 0)
    row_spec = lambda w_: pl.BlockSpec((tm, w_), lambda i: (tile(i), 0))
    in_specs = [row_spec(D_MODEL), row_spec(D_MODEL)]
    args = [hn, acc]
    if g_final is not None:
        in_specs.append(pl.BlockSpec((1, D_MODEL), lambda i: (0, 0)))
        args.append(g_final)
    up_spec = pl.BlockSpec((None, rows_in, width), lambda i: (idx, jnp.minimum(i, pieces[0] - 1), slab))
    in_specs += [up_spec, up_spec,
                 pl.BlockSpec((None, rows_dn, D_MODEL),
                              lambda i: (idx, slab * pieces[1] + jnp.minimum(i, pieces[1] - 1), 0))]
    args += [wg, wu, wd]
    return pl.pallas_call(
        functools.partial(_ffn_kernel, pieces=pieces, final_norm=g_final is not None),
        grid=(n_pro + nm,),
        in_specs=in_specs,
        out_specs=row_spec(D_MODEL),
        out_shape=jax.ShapeDtypeStruct((m, D_MODEL), F32),
        scratch_shapes=[pltpu.VMEM((D_MODEL, width), BF16), pltpu.VMEM((D_MODEL, width), BF16),
                        pltpu.VMEM((width, D_MODEL), BF16)],
        compiler_params=_params(),
        name="ffn_slab",
    )(*args)


def _rope_tables(pos):
    half = ROPE_DIM // 2
    inv = ROPE_THETA ** (-jnp.arange(0, ROPE_DIM, 2, dtype=F32) / ROPE_DIM)
    ang = pos[:, None] * inv[None, :]
    cos, sin = jnp.cos(ang), jnp.sin(ang)
    n = pos.shape[0]
    rest = HEAD_DIM - ROPE_DIM
    a = jnp.concatenate([cos, cos, jnp.ones((n, rest), F32)], axis=1)
    b = jnp.concatenate([jnp.zeros((n, half), F32), sin, jnp.zeros((n, rest), F32)], axis=1)
    c = jnp.concatenate([-sin, jnp.zeros((n, half + rest), F32)], axis=1)
    reps = LANES // HEAD_DIM
    scale = HEAD_DIM ** -0.5
    return tuple(jnp.tile(t, (1, reps)) * scale for t in (a, b, c))


def kernel(x_prompt, x_sample, cache_k, cache_v, state_pool, norm_mix, w_in, attn_sinks, w_pool, pool_scale,
           w_out, norm_ffn, w_gate_dense, w_up_dense, w_down_dense, w_router, b_router, w_gate_exp, w_up_exp,
           w_down_exp, norm_final):
    tm = 512
    xp = x_prompt.reshape(BATCH * SEQ, D_MODEL)
    xs = x_sample.reshape(DEC_BATCH, D_MODEL)
    rope_p = _rope_tables(jnp.arange(SEQ, dtype=F32))
    rope_s = _rope_tables(jnp.full((DEC_BATCH,), PAST_LEN, dtype=F32))
    ck = cache_k.reshape(DEPTH, DEC_BATCH, WINDOW, KV_WIDTH)
    cv = cache_v.reshape(DEPTH, DEC_BATCH, WINDOW, KV_WIDTH)
    g_final = norm_final.reshape(1, D_MODEL)
    slab_w = D_FF // DENSE_SLABS
    heads = (N_KV_HEADS, HEAD_DIM)

    nk_p, nv_p, nu_p, nk_s, nv_s, nu_s = [], [], [], [], [], []
    for l in range(DEPTH):
        g_mix = norm_mix[l].reshape(1, D_MODEL)
        g_ffn = norm_ffn[l].reshape(1, D_MODEL)
        scale = pool_scale[l].reshape(1, POOL_WIDTH)
        moe = l % 2 == 1
        i = l // 2
        last = l == DEPTH - 1
        router = None
        if moe:
            router = (jnp.pad(w_router[i], ((0, 0), (0, ROUTER_PAD - N_EXPERTS))),
                      jnp.pad(b_router[i], (0, ROUTER_PAD - N_EXPERTS)).reshape(1, ROUTER_PAD))

        q, k, v, u, kvt, ut = _inproj(xp, g_mix, w_in, l, rope_p, tm=tm)
        mix = _mixer(q, k, v, u, attn_sinks[l], w_pool, scale, l)
        xp, hn, *routing = _outproj(mix, xp, w_out, g_ffn, l, router, tm=tm)
        nk_p.append(kvt[:, :KV_WIDTH].reshape(BATCH, WINDOW, *heads))
        nv_p.append(kvt[:, KV_WIDTH:].reshape(BATCH, WINDOW, *heads))
        nu_p.append(ut.reshape(BATCH, BF16_ROWS, POOL_WIDTH)[:, BF16_ROWS - POOL_STATE:])
        if moe:
            route, counts = routing
            dest, pad_rows, steps = _route_plan(route, counts)
            y = _experts(_dispatch(hn, dest, pad_rows), w_gate_exp, w_up_exp, w_down_exp, i, steps)
            xp = _combine(xp, route, y, dest, g_final if last else None)
        else:
            for s in range(DENSE_SLABS):
                xp = _ffn_slab(hn, xp, w_gate_dense, w_up_dense, w_down_dense, i, s, slab_w, tm=tm,
                               g_final=g_final if last and s == DENSE_SLABS - 1 else None)

        z = _sample_inproj(xs, g_mix, w_in, l, rope_s)
        a_s, nk, nv, nu, ps = _sample_mixer(
            z[:, :ATTN_WIDTH], z[:, ATTN_WIDTH:ATTN_WIDTH + KV_WIDTH],
            z[:, ATTN_WIDTH + KV_WIDTH:ATTN_WIDTH + 2 * KV_WIDTH], z[:, ATTN_WIDTH + 2 * KV_WIDTH:],
            ck, cv, state_pool, attn_sinks[l], w_pool, scale, l)
        mix_s = jnp.concatenate([a_s.reshape(DEC_BATCH, ATTN_WIDTH), ps], axis=1)
        xs = _sample_outproj(mix_s, xs, w_out, l)
        if moe:
            xs = _sample_ffn(xs, g_ffn, w_gate_exp, w_up_exp, w_down_exp, i, router=router,
                             g_final=g_final if last else None)
        else:
            xs = _sample_ffn(xs, g_ffn, w_gate_dense, w_up_dense, w_down_dense, i,
                             g_final=g_final if last else None)
        nk_s.append(nk.reshape(DEC_BATCH, WINDOW, *heads))
        nv_s.append(nv.reshape(DEC_BATCH, WINDOW, *heads))
        nu_s.append(nu)

    y_prompt = xp.reshape(BATCH, SEQ, D_MODEL)
    y_sample = xs.reshape(DEC_BATCH, 1, D_MODEL)
    return (y_prompt, y_sample, jnp.stack(nk_p), jnp.stack(nv_p), jnp.stack(nu_p),
            jnp.stack(nk_s), jnp.stack(nv_s), jnp.stack(nu_s))
```

```python
import functools

import jax
import jax.numpy as jnp
from jax import lax
from jax.experimental import pallas as pl
from jax.experimental.pallas import tpu as pltpu

F32 = jnp.float32
BF16 = jnp.bfloat16

D_MODEL = 2048
BATCH = 2
SEQ = 4096
DEPTH = 4
DEC_BATCH = 32
PAST_LEN = 16384
N_HEADS = 16
N_KV_HEADS = 4
HEAD_DIM = 64
GROUP = N_HEADS // N_KV_HEADS
ATTN_WIDTH = N_HEADS * HEAD_DIM
KV_WIDTH = N_KV_HEADS * HEAD_DIM
WINDOW = 128
ROPE_DIM = HEAD_DIM // 4
ROPE_THETA = 500000.0
POOL_WINDOWS = (2, 4, 8, 16)
POOL_WIDTH = D_MODEL - ATTN_WIDTH
POOL_GROUP_DIM = POOL_WIDTH // len(POOL_WINDOWS)
POOL_STATE = max(POOL_WINDOWS) - 1
IN_WIDTH = ATTN_WIDTH + 2 * KV_WIDTH + POOL_WIDTH
D_FF = 5632
N_EXPERTS = 8
D_FF_EXPERT = D_MODEL // 2
EPS = 1e-5
NEG_INF = -1e30

LANES = 128
BF16_ROWS = 16
VMEM_LIMIT = 56 * 1024 * 1024
DENSE_SLABS = 4
ROUTER_PAD = LANES


def _params(n_axes=1):
    return pltpu.CompilerParams(dimension_semantics=("arbitrary",) * n_axes,
                                vmem_limit_bytes=VMEM_LIMIT)


def _rms(x, g):
    ms = jnp.mean(x * x, axis=-1, keepdims=True)
    return (x * lax.rsqrt(ms + EPS)) * g


def _rope_mix(zc, ra, rb, rc):
    return zc * ra + pltpu.roll(zc, ROPE_DIM // 2, 1) * rb + pltpu.roll(zc, LANES - ROPE_DIM // 2, 1) * rc


def _split_bf16(a):
    hi = a.astype(BF16)
    return hi, (a - hi.astype(F32)).astype(BF16)


def _dot3(a, b, dims=(((1,), (0,)), ((), ()))):
    m = a.shape[0]
    ah, al = _split_bf16(a)
    bh, bl = _split_bf16(b)
    t = lax.dot_general(jnp.concatenate([ah, al], axis=0), bh, dims, preferred_element_type=F32)
    return t[:m] + t[m:] + lax.dot_general(ah, bl, dims, preferred_element_type=F32)


def _stash_piece(step, w_ref, wbf, n_pieces):
    rows = w_ref.shape[0]

    @pl.when(step < n_pieces)
    def _():
        r0 = pl.multiple_of(step * rows, rows)
        wbf[pl.ds(r0, rows), :] = w_ref[...].astype(BF16)


def _piece_rows(total_rows, n_pieces):
    rows, rem = divmod(total_rows, n_pieces)
    assert rem == 0 and rows % BF16_ROWS == 0, (total_rows, n_pieces)
    return rows


def _inproj_kernel(x_ref, g_ref, w_ref, ra_ref, rb_ref, rc_ref, q_ref, k_ref, v_ref, u_ref, kvt_ref, ut_ref,
                   wbf, *, n_pieces):
    step = pl.program_id(0)
    _stash_piece(step, w_ref, wbf, n_pieces)

    @pl.when(step >= n_pieces)
    def _():
        tm = x_ref.shape[0]
        h = _rms(x_ref[...], g_ref[...])
        z = jnp.dot(h.astype(BF16), wbf[...], preferred_element_type=F32)
        ra, rb, rc = ra_ref[...], rb_ref[...], rc_ref[...]
        n_rot = (ATTN_WIDTH + KV_WIDTH) // LANES
        rot = []
        for c in range(n_rot):
            r = _rope_mix(z[:, c * LANES:(c + 1) * LANES], ra, rb, rc)
            if c >= ATTN_WIDTH // LANES:
                r = r * (HEAD_DIM ** 0.5)
            rot.append(r)
        nq = ATTN_WIDTH // LANES
        for c in range(nq):
            q_ref[:, c * LANES:(c + 1) * LANES] = rot[c].astype(q_ref.dtype)
        for c in range(nq, n_rot):
            k_ref[:, (c - nq) * LANES:(c - nq + 1) * LANES] = rot[c].astype(k_ref.dtype)
        v = z[:, ATTN_WIDTH + KV_WIDTH:ATTN_WIDTH + 2 * KV_WIDTH]
        u = z[:, ATTN_WIDTH + 2 * KV_WIDTH:]
        v_ref[...] = v.astype(v_ref.dtype)
        u_ref[...] = u.astype(u_ref.dtype)
        for c in range(nq, n_rot):
            kvt_ref[:, (c - nq) * LANES:(c - nq + 1) * LANES] = rot[c][tm - WINDOW:, :]
        kvt_ref[:, KV_WIDTH:] = v[tm - WINDOW:, :]
        ut_ref[...] = u[tm - BF16_ROWS:, :]


def _inproj(x, g, w_in, layer, rope, *, tm):
    m = x.shape[0]
    nm = m // tm
    n_pieces = 8
    rows = _piece_rows(D_MODEL, n_pieces)
    tiles_per_seq = SEQ // tm
    tile = lambda i: jnp.maximum(i - n_pieces, 0)
    row_spec = lambda width: pl.BlockSpec((tm, width), lambda i: (tile(i), 0))
    rope_spec = pl.BlockSpec((tm, LANES), lambda i: (tile(i) % tiles_per_seq, 0))
    out_shape = [jax.ShapeDtypeStruct((m, ATTN_WIDTH), BF16),
                 jax.ShapeDtypeStruct((m, KV_WIDTH), BF16),
                 jax.ShapeDtypeStruct((m, KV_WIDTH), BF16),
                 jax.ShapeDtypeStruct((m, POOL_WIDTH), BF16),
                 jax.ShapeDtypeStruct((BATCH * WINDOW, 2 * KV_WIDTH), F32),
                 jax.ShapeDtypeStruct((BATCH * BF16_ROWS, POOL_WIDTH), F32)]
    out_specs = [row_spec(ATTN_WIDTH), row_spec(KV_WIDTH), row_spec(KV_WIDTH), row_spec(POOL_WIDTH),
                 pl.BlockSpec((WINDOW, 2 * KV_WIDTH), lambda i: (tile(i) // tiles_per_seq, 0)),
                 pl.BlockSpec((BF16_ROWS, POOL_WIDTH), lambda i: (tile(i) // tiles_per_seq, 0))]
    return pl.pallas_call(
        functools.partial(_inproj_kernel, n_pieces=n_pieces),
        grid=(n_pieces + nm,),
        in_specs=[row_spec(D_MODEL),
                  pl.BlockSpec((1, D_MODEL), lambda i: (0, 0)),
                  pl.BlockSpec((None, rows, IN_WIDTH), lambda i: (layer, jnp.minimum(i, n_pieces - 1), 0)),
                  rope_spec, rope_spec, rope_spec],
        out_specs=out_specs,
        out_shape=out_shape,
        scratch_shapes=[pltpu.VMEM((D_MODEL, IN_WIDTH), BF16)],
        compiler_params=_params(),
        name="inproj",
    )(x, g, w_in, *rope)


def _sink_softmax(s, sink, axis):
    mx = jnp.maximum(jnp.max(s, axis=axis, keepdims=True), sink)
    p = jnp.exp(s - mx)
    den = jnp.sum(p, axis=axis, keepdims=True) + jnp.exp(sink - mx)
    return p * (1.0 / den)


def _pool_group(d, wpb_g, scale_g):
    return jnp.dot(d.astype(BF16), wpb_g, preferred_element_type=F32) * scale_g


def _mixer_kernel(sink_ref, q_ref, kc_ref, kp_ref, vc_ref, vp_ref, uc_ref, up_ref, wp_ref, sc_ref,
                  o_ref, wpb, band):
    b = pl.program_id(0)
    n = pl.program_id(1)
    blk = WINDOW

    @pl.when((b == 0) & (n == 0))
    def _():
        wpb[...] = wp_ref[...].astype(BF16)
        tok = lax.broadcasted_iota(jnp.int32, (blk, 2 * blk), 0) + blk
        src = lax.broadcasted_iota(jnp.int32, (blk, 2 * blk), 1)
        for g, w in enumerate(POOL_WINDOWS):
            band[g] = jnp.where((src <= tok) & (src > tok - w), 1.0, 0.0).astype(BF16)

    half = LANES // 2
    kk = jnp.concatenate([kp_ref[...], kc_ref[...]], axis=0).astype(F32)
    vv = jnp.concatenate([vp_ref[...], vc_ref[...]], axis=0).astype(F32)
    lane = lax.broadcasted_iota(jnp.int32, (2 * blk, LANES), 1)
    low = lane < half
    key = lax.broadcasted_iota(jnp.int32, (4 * blk, blk), 0) & (2 * blk - 1)
    qry = lax.broadcasted_iota(jnp.int32, (4 * blk, blk), 1)
    valid = (key > qry) & (key <= qry + blk) & ((n > 0) | (key >= blk))

    for h in range(N_KV_HEADS):
        g0 = (h // 2) * LANES
        kg, vg = kk[:, g0:g0 + LANES], vv[:, g0:g0 + LANES]
        if h % 2 == 0:
            klo, vlo = jnp.where(low, kg, 0.0), jnp.where(low, vg, 0.0)
            khi, vhi = pltpu.roll(klo, half, 1), pltpu.roll(vlo, half, 1)
        else:
            khi, vhi = jnp.where(low, 0.0, kg), jnp.where(low, 0.0, vg)
            klo, vlo = pltpu.roll(khi, half, 1), pltpu.roll(vhi, half, 1)
        kcat = jnp.concatenate([klo, khi], axis=0).astype(BF16)
        vcat = jnp.concatenate([vlo, vhi], axis=0).astype(BF16)
        for pair in range(h * GROUP // 2, (h + 1) * GROUP // 2):
            qp = q_ref[:, pair * LANES:(pair + 1) * LANES]
            st = lax.dot_general(kcat, qp, (((1,), (1,)), ((), ())), preferred_element_type=F32)
            st = jnp.where(valid, st, NEG_INF)
            pt = [_sink_softmax(st[t * 2 * blk:(t + 1) * 2 * blk], sink_ref[2 * pair + t], 0).astype(BF16)
                  for t in range(2)]
            o = lax.dot_general(jnp.concatenate(pt, axis=0), vcat, (((0,), (0,)), ((), ())),
                                preferred_element_type=F32)
            o_ref[:, pair * LANES:(pair + 1) * LANES] = o.astype(o_ref.dtype)

    ext = jnp.concatenate([jnp.where(n > 0, up_ref[...], jnp.zeros_like(up_ref)), uc_ref[...]], axis=0)
    pos = (n * blk + lax.broadcasted_iota(jnp.int32, (blk, 1), 0)).astype(F32)
    for g, w in enumerate(POOL_WINDOWS):
        c0, c1 = g * POOL_GROUP_DIM, (g + 1) * POOL_GROUP_DIM
        wsum = jnp.dot(band[g], ext[:, c0:c1], preferred_element_type=F32)
        d = wsum * (1.0 / jnp.minimum(float(w), pos + 1.0)) - uc_ref[:, c0:c1].astype(F32)
        y = _pool_group(d, wpb[g], sc_ref[:, c0:c1])
        o_ref[:, ATTN_WIDTH + c0:ATTN_WIDTH + c1] = y.astype(o_ref.dtype)


def _mixer(q, k, v, u, sinks, w_pool, scale, layer):
    assert u.dtype == BF16
    nb = SEQ // WINDOW
    cur = lambda b, n: (b * nb + n, 0)
    prev = lambda b, n: (b * nb + jnp.maximum(n - 1, 0), 0)
    n_grp = len(POOL_WINDOWS)
    return pl.pallas_call(
        _mixer_kernel,
        grid=(BATCH, nb),
        in_specs=[pl.BlockSpec(memory_space=pltpu.SMEM),
                  pl.BlockSpec((WINDOW, ATTN_WIDTH), cur),
                  pl.BlockSpec((WINDOW, KV_WIDTH), cur), pl.BlockSpec((WINDOW, KV_WIDTH), prev),
                  pl.BlockSpec((WINDOW, KV_WIDTH), cur), pl.BlockSpec((WINDOW, KV_WIDTH), prev),
                  pl.BlockSpec((WINDOW, POOL_WIDTH), cur), pl.BlockSpec((WINDOW, POOL_WIDTH), prev),
                  pl.BlockSpec((None, n_grp, POOL_GROUP_DIM, POOL_GROUP_DIM), lambda b, n: (layer, 0, 0, 0)),
                  pl.BlockSpec((1, POOL_WIDTH), lambda b, n: (0, 0))],
        out_specs=pl.BlockSpec((WINDOW, D_MODEL), cur),
        out_shape=jax.ShapeDtypeStruct((BATCH * SEQ, D_MODEL), BF16),
        scratch_shapes=[pltpu.VMEM((n_grp, POOL_GROUP_DIM, POOL_GROUP_DIM), BF16),
                        pltpu.VMEM((n_grp, WINDOW, 2 * WINDOW), BF16)],
        compiler_params=_params(2),
        name="mixer",
    )(sinks, q, k, k, v, v, u, u, w_pool, scale)


SAMPLE_SB = 4


def _sample_mixer_kernel(sink_ref, q_ref, kn_ref, vn_ref, un_ref, ck_ref, cv_ref, st_ref, wp_ref, sc_ref,
                         a_ref, nk_ref, nv_ref, nu_ref, po_ref, dscr):
    step = pl.program_id(0)
    sink = sink_ref[...]
    own = (lax.broadcasted_iota(jnp.int32, (N_HEADS, KV_WIDTH), 0) // GROUP
           == lax.broadcasted_iota(jnp.int32, (N_HEADS, KV_WIDTH), 1) // HEAD_DIM)
    for j in range(SAMPLE_SB):
        nk_ref[j, 0:WINDOW - 1, :] = ck_ref[j, 1:WINDOW, :]
        nk_ref[j, WINDOW - 1:WINDOW, :] = kn_ref[j]
        nv_ref[j, 0:WINDOW - 1, :] = cv_ref[j, 1:WINDOW, :]
        nv_ref[j, WINDOW - 1:WINDOW, :] = vn_ref[j]

        q = q_ref[j]
        qe = jnp.where(own, jnp.concatenate([q] * N_KV_HEADS, axis=1), 0.0)
        s = _dot3(qe, nk_ref[j], (((1,), (1,)), ((), ())))
        r = _dot3(_sink_softmax(s, sink, 1), nv_ref[j])
        r = jnp.where(own, r, 0.0)
        o = r[:, 0:HEAD_DIM]
        for h in range(1, N_KV_HEADS):
            o = o + r[:, h * HEAD_DIM:(h + 1) * HEAD_DIM]
        a_ref[j] = o

        st = st_ref[j]
        un = un_ref[j]
        nu_ref[j, 0:POOL_STATE - 1, :] = st[1:POOL_STATE, :]
        nu_ref[j, POOL_STATE - 1:POOL_STATE, :] = un
        for g, w in enumerate(POOL_WINDOWS):
            c0, c1 = g * POOL_GROUP_DIM, (g + 1) * POOL_GROUP_DIM
            tok = un[:, c0:c1]
            wsum = tok + jnp.sum(st[POOL_STATE - (w - 1):POOL_STATE, c0:c1], axis=0, keepdims=True)
            dscr[pl.ds(step * SAMPLE_SB + j, 1), c0:c1] = wsum / float(min(w, PAST_LEN + 1)) - tok

    @pl.when(step == pl.num_programs(0) - 1)
    def _():
        for g in range(len(POOL_WINDOWS)):
            c0, c1 = g * POOL_GROUP_DIM, (g + 1) * POOL_GROUP_DIM
            po_ref[:, c0:c1] = _dot3(dscr[:, c0:c1], wp_ref[g]) * sc_ref[:, c0:c1]


def _sample_mixer(q, k, v, u, cache_k, cache_v, state, sinks, w_pool, scale, layer):
    nb, sb = DEC_BATCH, SAMPLE_SB
    per_b = lambda *shape: pl.BlockSpec((sb,) + shape, lambda b: (b,) + (0,) * len(shape))
    per_lb = lambda *shape: pl.BlockSpec((None, sb) + shape, lambda b: (layer, b) + (0,) * len(shape))
    return pl.pallas_call(
        _sample_mixer_kernel,
        grid=(nb // sb,),
        in_specs=[pl.BlockSpec((N_HEADS, 1), lambda b: (0, 0)),
                  per_b(N_HEADS, HEAD_DIM), per_b(1, KV_WIDTH), per_b(1, KV_WIDTH), per_b(1, POOL_WIDTH),
                  per_lb(WINDOW, KV_WIDTH), per_lb(WINDOW, KV_WIDTH), per_lb(POOL_STATE, POOL_WIDTH),
                  pl.BlockSpec((None, len(POOL_WINDOWS), POOL_GROUP_DIM, POOL_GROUP_DIM),
                               lambda b: (layer, 0, 0, 0)),
                  pl.BlockSpec((1, POOL_WIDTH), lambda b: (0, 0))],
        out_specs=[per_b(N_HEADS, HEAD_DIM), per_lb(WINDOW, KV_WIDTH), per_lb(WINDOW, KV_WIDTH),
                   per_lb(POOL_STATE, POOL_WIDTH), pl.BlockSpec((nb, POOL_WIDTH), lambda b: (0, 0))],
        out_shape=[jax.ShapeDtypeStruct((nb, N_HEADS, HEAD_DIM), F32),
                   jax.ShapeDtypeStruct(cache_k.shape, F32),
                   jax.ShapeDtypeStruct(cache_v.shape, F32),
                   jax.ShapeDtypeStruct(state.shape, F32),
                   jax.ShapeDtypeStruct((nb, POOL_WIDTH), F32)],
        scratch_shapes=[pltpu.VMEM((nb, POOL_WIDTH), F32)],
        input_output_aliases={5: 1, 6: 2, 7: 3},
        compiler_params=_params(),
        name="sample_mixer",
    )(sinks.reshape(N_HEADS, 1), q.reshape(nb, N_HEADS, HEAD_DIM), k.reshape(nb, 1, KV_WIDTH),
      v.reshape(nb, 1, KV_WIDTH), u.reshape(nb, 1, POOL_WIDTH), cache_k, cache_v, state, w_pool, scale)


SAMPLE_TN = 512


def _sample_inproj_kernel(x_ref, g_ref, w_ref, ra_ref, rb_ref, rc_ref, z_ref):
    j = pl.program_id(0)
    z = _dot3(_rms(x_ref[...], g_ref[...]), w_ref[...])
    ra, rb, rc = ra_ref[...], rb_ref[...], rc_ref[...]
    per_tile = SAMPLE_TN // LANES
    for c in range(per_tile):
        zc = z[:, c * LANES:(c + 1) * LANES]
        grp = j * per_tile + c
        r = _rope_mix(zc, ra, rb, rc)
        r = jnp.where(grp >= ATTN_WIDTH // LANES, r * (HEAD_DIM ** 0.5), r)
        z_ref[:, c * LANES:(c + 1) * LANES] = jnp.where(grp < (ATTN_WIDTH + KV_WIDTH) // LANES, r, zc)


def _sample_inproj(x, g, w_in, layer, rope):
    m = x.shape[0]
    const = lambda r, c: pl.BlockSpec((r, c), lambda j: (0, 0))
    return pl.pallas_call(
        _sample_inproj_kernel,
        grid=(IN_WIDTH // SAMPLE_TN,),
        in_specs=[const(m, D_MODEL), const(1, D_MODEL),
                  pl.BlockSpec((None, D_MODEL, SAMPLE_TN), lambda j: (layer, 0, j)),
                  const(m, LANES), const(m, LANES), const(m, LANES)],
        out_specs=pl.BlockSpec((m, SAMPLE_TN), lambda j: (0, j)),
        out_shape=jax.ShapeDtypeStruct((m, IN_WIDTH), F32),
        compiler_params=_params(),
        name="sample_inproj",
    )(x, g, w_in, *rope)


def _sample_outproj_kernel(mix_ref, x_ref, w_ref, o_ref):
    o_ref[...] = x_ref[...] + _dot3(mix_ref[...], w_ref[...])


def _sample_outproj(mix, x, w_out, layer):
    m = x.shape[0]
    col = pl.BlockSpec((m, SAMPLE_TN), lambda j: (0, j))
    return pl.pallas_call(
        _sample_outproj_kernel,
        grid=(D_MODEL // SAMPLE_TN,),
        in_specs=[pl.BlockSpec((m, D_MODEL), lambda j: (0, 0)), col,
                  pl.BlockSpec((None, D_MODEL, SAMPLE_TN), lambda j: (layer, 0, j))],
        out_specs=col,
        out_shape=jax.ShapeDtypeStruct((m, D_MODEL), F32),
        compiler_params=_params(),
        name="sample_outproj",
    )(mix, x, w_out)


def _sample_ffn_kernel(x_ref, g_ref, *rest, moe, final_norm):
    rest = list(rest)
    wr_ref, br_ref = (rest.pop(0), rest.pop(0)) if moe else (None, None)
    gf_ref = rest.pop(0) if final_norm else None
    wg_ref, wu_ref, wd_ref, o_ref, acc, hn_scr, cmb = rest
    e, f = pl.program_id(0), pl.program_id(1)

    @pl.when((e == 0) & (f == 0))
    def _():
        hn = _rms(x_ref[...], g_ref[...])
        hn_scr[...] = hn
        acc[...] = jnp.zeros_like(acc)
        if moe:
            cmb[...] = _top2_gates(_dot3(hn, wr_ref[...]) + br_ref[...])

    hn = hn_scr[...]
    gate = _dot3(hn, wg_ref[...])
    up = _dot3(hn, wu_ref[...])
    h = (gate * (1.0 / (1.0 + jnp.exp(-gate)))) * up
    if moe:
        lane = lax.broadcasted_iota(jnp.int32, cmb.shape, 1)
        h = h * jnp.sum(jnp.where(lane == e, cmb[...], 0.0), axis=1, keepdims=True)
    acc[...] += _dot3(h, wd_ref[...])

    @pl.when((e == pl.num_programs(0) - 1) & (f == pl.num_programs(1) - 1))
    def _():
        out = x_ref[...] + acc[...]
        if final_norm:
            out = _rms(out, gf_ref[...])
        o_ref[...] = out


def _sample_ffn(x, g, wg, wu, wd, idx, *, router=None, g_final=None):
    m = x.shape[0]
    moe = router is not None
    n_e = N_EXPERTS if moe else 1
    width = wg.shape[-1]
    const = lambda r, c: pl.BlockSpec((r, c), lambda e, f: (0, 0))
    in_specs = [const(m, D_MODEL), const(1, D_MODEL)]
    args = [x, g]
    if moe:
        in_specs += [const(D_MODEL, ROUTER_PAD), const(1, ROUTER_PAD)]
        args += list(router)
        up_spec = pl.BlockSpec((None, None, D_MODEL, SAMPLE_TN), lambda e, f: (idx, e, 0, f))
        dn_spec = pl.BlockSpec((None, None, SAMPLE_TN, D_MODEL), lambda e, f: (idx, e, f, 0))
    else:
        up_spec = pl.BlockSpec((None, D_MODEL, SAMPLE_TN), lambda e, f: (idx, 0, f))
        dn_spec = pl.BlockSpec((None, SAMPLE_TN, D_MODEL), lambda e, f: (idx, f, 0))
    if g_final is not None:
        in_specs.append(const(1, D_MODEL))
        args.append(g_final)
    return pl.pallas_call(
        functools.partial(_sample_ffn_kernel, moe=moe, final_norm=g_final is not None),
        grid=(n_e, width // SAMPLE_TN),
        in_specs=in_specs + [up_spec, up_spec, dn_spec],
        out_specs=const(m, D_MODEL),
        out_shape=jax.ShapeDtypeStruct((m, D_MODEL), F32),
        scratch_shapes=[pltpu.VMEM((m, D_MODEL), F32), pltpu.VMEM((m, D_MODEL), F32),
                        pltpu.VMEM((m, ROUTER_PAD), F32)],
        compiler_params=_params(2),
        name="sample_ffn",
    )(*args, wg, wu, wd)


def _top2(logits):
    lane = lax.broadcasted_iota(jnp.int32, logits.shape, 1)
    lg = jnp.where(lane < N_EXPERTS, logits, -jnp.inf)
    m1 = jnp.max(lg, axis=1, keepdims=True)
    i1 = jnp.min(jnp.where(lg == m1, lane, ROUTER_PAD), axis=1, keepdims=True)
    lg2 = jnp.where(lane == i1, -jnp.inf, lg)
    m2 = jnp.max(lg2, axis=1, keepdims=True)
    i2 = jnp.min(jnp.where(lg2 == m2, lane, ROUTER_PAD), axis=1, keepdims=True)
    e2 = jnp.exp(m2 - m1)
    den = 1.0 + e2
    return i1, i2, 1.0 / den, e2 / den


def _top2_gates(logits):
    i1, i2, g1, g2 = _top2(logits)
    lane = lax.broadcasted_iota(jnp.int32, logits.shape, 1)
    return jnp.where(lane == i1, g1, 0.0) + jnp.where(lane == i2, g2, 0.0)


ROUTE_I1, ROUTE_I2, ROUTE_G1, ROUTE_G2, ROUTE_P1, ROUTE_P2 = range(6)


def _outproj_kernel(mix_ref, x_ref, w_ref, g_ref, *rest, n_pieces, with_router):
    if with_router:
        wr_ref, br_ref, xo_ref, hn_ref, route_ref, cnt_ref, wbf, tri, carry = rest
    else:
        xo_ref, hn_ref, wbf = rest
    step = pl.program_id(0)
    _stash_piece(step, w_ref, wbf, n_pieces)

    if with_router:
        @pl.when(step == 0)
        def _():
            row = lax.broadcasted_iota(jnp.int32, tri.shape, 0)
            col = lax.broadcasted_iota(jnp.int32, tri.shape, 1)
            tri[...] = jnp.where(col < row, 1.0, 0.0).astype(BF16)
            carry[...] = jnp.zeros_like(carry)

    @pl.when(step >= n_pieces)
    def _():
        y = jnp.dot(mix_ref[...], wbf[...], preferred_element_type=F32)
        xo = x_ref[...] + y
        xo_ref[...] = xo
        hn = _rms(xo, g_ref[...])
        if not with_router:
            hn_ref[...] = hn.astype(BF16)
            return
        hn_ref[...] = hn
        logits = jnp.dot(hn.astype(BF16), wr_ref[...].astype(BF16), preferred_element_type=F32) + br_ref[...]
        i1, i2, g1, g2 = _top2(logits)
        lane = lax.broadcasted_iota(jnp.int32, logits.shape, 1)
        picked = jnp.where((lane == i1) | (lane == i2), 1.0, 0.0)
        ahead = jnp.dot(tri[...], picked.astype(BF16), preferred_element_type=F32) + carry[...]
        p1 = jnp.sum(jnp.where(lane == i1, ahead, 0.0), axis=1, keepdims=True)
        p2 = jnp.sum(jnp.where(lane == i2, ahead, 0.0), axis=1, keepdims=True)
        carry[...] += jnp.sum(picked, axis=0, keepdims=True)
        cnt_ref[...] = carry[...]
        rec = jnp.zeros(logits.shape, F32)
        for k, val in ((ROUTE_I1, i1.astype(F32)), (ROUTE_I2, i2.astype(F32)), (ROUTE_G1, g1), (ROUTE_G2, g2),
                       (ROUTE_P1, p1), (ROUTE_P2, p2)):
            rec = jnp.where(lane == k, val, rec)
        route_ref[...] = rec


def _outproj(mix, x, w_out, g, layer, router, *, tm):
    m = x.shape[0]
    nm = m // tm
    n_pieces = 8
    rows = _piece_rows(D_MODEL, n_pieces)
    tile = lambda i: jnp.maximum(i - n_pieces, 0)
    row_spec = lambda width: pl.BlockSpec((tm, width), lambda i: (tile(i), 0))
    const = lambda r, c: pl.BlockSpec((r, c), lambda i: (0, 0))
    in_specs = [row_spec(D_MODEL), row_spec(D_MODEL),
                pl.BlockSpec((None, rows, D_MODEL), lambda i: (layer, jnp.minimum(i, n_pieces - 1), 0)),
                const(1, D_MODEL)]
    args = [mix, x, w_out, g]
    scratch = [pltpu.VMEM((D_MODEL, D_MODEL), BF16)]
    if router is None:
        out_shape = [jax.ShapeDtypeStruct((m, D_MODEL), F32), jax.ShapeDtypeStruct((m, D_MODEL), BF16)]
        out_specs = [row_spec(D_MODEL), row_spec(D_MODEL)]
    else:
        in_specs += [const(D_MODEL, ROUTER_PAD), const(1, ROUTER_PAD)]
        args += list(router)
        out_shape = [jax.ShapeDtypeStruct((m, D_MODEL), F32), jax.ShapeDtypeStruct((m, D_MODEL), F32),
                     jax.ShapeDtypeStruct((m, ROUTER_PAD), F32), jax.ShapeDtypeStruct((1, ROUTER_PAD), F32)]
        out_specs = [row_spec(D_MODEL), row_spec(D_MODEL), row_spec(ROUTER_PAD), const(1, ROUTER_PAD)]
        scratch += [pltpu.VMEM((tm, tm), BF16), pltpu.VMEM((1, ROUTER_PAD), F32)]
    return pl.pallas_call(
        functools.partial(_outproj_kernel, n_pieces=n_pieces, with_router=router is not None),
        grid=(n_pieces + nm,),
        in_specs=in_specs,
        out_specs=out_specs,
        out_shape=out_shape,
        scratch_shapes=scratch,
        compiler_params=_params(),
        name="outproj",
    )(*args)


EXPERT_TM = 512
EXPERT_PIECES = 4
ROUTE_TM = 256
SORTED_ROWS = BATCH * SEQ * 2 + N_EXPERTS * EXPERT_TM
EXPERT_STEPS = N_EXPERTS * EXPERT_PIECES + SORTED_ROWS // EXPERT_TM


def _take(table, idx):
    hot = idx[..., None] == jnp.arange(table.shape[0], dtype=idx.dtype)
    return jnp.sum(jnp.where(hot, table, 0), axis=-1)


def _route_plan(route, counts):
    i32 = jnp.int32
    t, p = EXPERT_TM, EXPERT_PIECES
    cnt = counts[0, :N_EXPERTS].astype(i32)
    ntile = (cnt + t - 1) // t
    tile0 = jnp.cumsum(ntile) - ntile
    experts = route[:, ROUTE_I1:ROUTE_I2 + 1].astype(i32)
    dest = _take(tile0, experts) * t + route[:, ROUTE_P1:ROUTE_P2 + 1].astype(i32)
    step0 = jnp.arange(N_EXPERTS, dtype=i32) * p + tile0
    s = jnp.arange(EXPERT_STEPS, dtype=i32)
    e = jnp.sum((s[:, None] >= step0[None, :]).astype(i32), axis=1) - 1
    k = s - _take(step0, e)
    load = k < p
    comp = jnp.logical_not(load) & (k - p < _take(ntile, e))
    n_tiles = jnp.sum(ntile)
    first = _take(tile0, e)
    spare = jnp.minimum(n_tiles + k - p - _take(ntile, e), SORTED_ROWS // t - 1)
    tile = jnp.where(comp, first + k - p, jnp.where(load, jnp.minimum(first, SORTED_ROWS // t - 1), spare))
    piece = jnp.where(load, k, p - 1)
    kind = jnp.where(load, 0, jnp.where(comp, 1, 2))
    nvalid = jnp.clip(_take(cnt, e) - (tile - first) * t, 0, t)
    dest = dest.reshape(dest.shape[0] // ROUTE_TM, 1, 2 * ROUTE_TM)
    n_all = SORTED_ROWS // t
    partial = jnp.where(ntile > 0, tile0 + ntile - 1, -1)
    unused = n_tiles + jnp.arange(N_EXPERTS, dtype=i32)
    pad_rows = jnp.concatenate([partial, jnp.where(unused < n_all, unused, -1)]) * t
    return dest, pad_rows, (e, piece, tile, kind, nvalid)


def _start_rows(n, copies_of):
    def start(t, c):
        for cp in copies_of(t):
            cp.start()
        return c

    lax.fori_loop(0, n, start, 0, unroll=8)


def _wait_rows(n, copies_of):
    def wait(t, c):
        for cp in copies_of(t):
            cp.wait()
        return c

    lax.fori_loop(0, n, wait, 0, unroll=8)


def _dispatch_kernel(dest_ref, pad_ref, hn_ref, xg_hbm, zeros, sem):
    step = pl.program_id(0)

    @pl.when(step == 0)
    def _():
        zeros[...] = jnp.zeros_like(zeros)

        def fill(j):
            row = pl.multiple_of(jnp.maximum(pad_ref[j], 0), EXPERT_TM)
            return pltpu.make_async_copy(zeros, xg_hbm.at[pl.ds(row, EXPERT_TM)], sem)

        for j in range(pad_ref.shape[0]):
            pl.when(pad_ref[j] >= 0)(lambda j=j: fill(j).start())
        for j in range(pad_ref.shape[0]):
            pl.when(pad_ref[j] >= 0)(lambda j=j: fill(j).wait())

    def copies_of(t):
        src = hn_ref.at[pl.ds(t, 1)]
        return [pltpu.make_async_copy(src, xg_hbm.at[pl.ds(dest_ref[0, 2 * t + k], 1)], sem) for k in range(2)]

    _start_rows(ROUTE_TM, copies_of)
    _wait_rows(ROUTE_TM, copies_of)


def _dispatch(hn, dest, pad_rows):
    return pl.pallas_call(
        _dispatch_kernel,
        grid=(hn.shape[0] // ROUTE_TM,),
        in_specs=[pl.BlockSpec((None, 1, 2 * ROUTE_TM), lambda s: (s, 0, 0), memory_space=pltpu.SMEM),
                  pl.BlockSpec(memory_space=pltpu.SMEM),
                  pl.BlockSpec((ROUTE_TM, D_MODEL), lambda s: (s, 0))],
        out_specs=pl.BlockSpec(memory_space=pl.ANY),
        out_shape=jax.ShapeDtypeStruct((SORTED_ROWS, D_MODEL), F32),
        scratch_shapes=[pltpu.VMEM((EXPERT_TM, D_MODEL), F32), pltpu.SemaphoreType.DMA],
        compiler_params=_params(),
        name="dispatch",
    )(dest, pad_rows, hn)


def _expert_kernel(st_e, st_p, st_t, st_kind, st_nv, x_ref, wg_ref, wu_ref, wd_ref, y_ref, wgb, wub, wdb):
    s = pl.program_id(0)
    kind = st_kind[s]

    @pl.when(kind == 0)
    def _():
        for w_ref, wbf in ((wg_ref, wgb), (wu_ref, wub), (wd_ref, wdb)):
            rows = w_ref.shape[0]
            r0 = pl.multiple_of(st_p[s] * rows, rows)
            wbf[pl.ds(r0, rows), :] = w_ref[...].astype(BF16)

    @pl.when(kind == 1)
    def _():
        row = lax.broadcasted_iota(jnp.int32, (x_ref.shape[0], 1), 0)
        x = jnp.where(row < st_nv[s], x_ref[...], 0.0).astype(BF16)
        gate = jnp.dot(x, wgb[...], preferred_element_type=F32)
        up = jnp.dot(x, wub[...], preferred_element_type=F32)
        h = (gate * (1.0 / (1.0 + jnp.exp(-gate)))) * up
        y_ref[...] = jnp.dot(h.astype(BF16), wdb[...], preferred_element_type=F32)

    @pl.when(kind == 2)
    def _():
        y_ref[...] = jnp.zeros_like(y_ref)


def _experts(xg, wg, wu, wd, idx, steps):
    t, p = EXPERT_TM, EXPERT_PIECES
    rows_in = _piece_rows(D_MODEL, p)
    rows_dn = _piece_rows(D_FF_EXPERT, p)
    up_spec = pl.BlockSpec((None, None, rows_in, D_FF_EXPERT), lambda s, e, pc, tl, kd, nv: (idx, e[s], pc[s], 0))
    row_spec = pl.BlockSpec((t, D_MODEL), lambda s, e, pc, tl, kd, nv: (tl[s], 0))
    return pl.pallas_call(
        _expert_kernel,
        grid_spec=pltpu.PrefetchScalarGridSpec(
            num_scalar_prefetch=5,
            grid=(EXPERT_STEPS,),
            in_specs=[row_spec, up_spec, up_spec,
                      pl.BlockSpec((None, None, rows_dn, D_MODEL),
                                   lambda s, e, pc, tl, kd, nv: (idx, e[s], pc[s], 0))],
            out_specs=row_spec,
            scratch_shapes=[pltpu.VMEM((D_MODEL, D_FF_EXPERT), BF16), pltpu.VMEM((D_MODEL, D_FF_EXPERT), BF16),
                            pltpu.VMEM((D_FF_EXPERT, D_MODEL), BF16)]),
        out_shape=jax.ShapeDtypeStruct((SORTED_ROWS, D_MODEL), F32),
        compiler_params=_params(),
        name="experts",
    )(*steps, xg, wg, wu, wd)


def _combine_kernel(dest_ref, next_ref, x_ref, route_ref, *rest, final_norm):
    if final_norm:
        gf_ref, y_hbm, o_ref, ya, yb, sem = rest
    else:
        y_hbm, o_ref, ya, yb, sem = rest
    step = pl.program_id(0)
    n = x_ref.shape[0]
    slot = step % 2

    def gather(idx_ref, slot):
        def copies_of(t):
            return [pltpu.make_async_copy(y_hbm.at[pl.ds(idx_ref[0, 2 * t + k], 1)], buf.at[slot, pl.ds(t, 1)],
                                          sem.at[slot])
                    for k, buf in enumerate((ya, yb))]
        return copies_of

    @pl.when(step == 0)
    def _():
        _start_rows(n, gather(dest_ref, 0))

    @pl.when(step + 1 < pl.num_programs(0))
    def _():
        _start_rows(n, gather(next_ref, 1 - slot))

    _wait_rows(n, gather(dest_ref, slot))
    out = (x_ref[...] + route_ref[:, ROUTE_G1:ROUTE_G1 + 1] * ya[slot]
           + route_ref[:, ROUTE_G2:ROUTE_G2 + 1] * yb[slot])
    if final_norm:
        out = _rms(out, gf_ref[...])
    o_ref[...] = out


def _combine(x, route, y, dest, g_final):
    m = x.shape[0]
    n_steps = m // ROUTE_TM
    row_spec = lambda width: pl.BlockSpec((ROUTE_TM, width), lambda s: (s, 0))
    idx_spec = lambda ahead: pl.BlockSpec((None, 1, 2 * ROUTE_TM),
                                          lambda s: (jnp.minimum(s + ahead, n_steps - 1), 0, 0),
                                          memory_space=pltpu.SMEM)
    in_specs = [idx_spec(0), idx_spec(1), row_spec(D_MODEL), row_spec(ROUTER_PAD)]
    args = [dest, dest, x, route]
    if g_final is not None:
        in_specs.append(pl.BlockSpec((1, D_MODEL), lambda s: (0, 0)))
        args.append(g_final)
    return pl.pallas_call(
        functools.partial(_combine_kernel, final_norm=g_final is not None),
        grid=(n_steps,),
        in_specs=in_specs + [pl.BlockSpec(memory_space=pl.ANY)],
        out_specs=row_spec(D_MODEL),
        out_shape=jax.ShapeDtypeStruct((m, D_MODEL), F32),
        scratch_shapes=[pltpu.VMEM((2, ROUTE_TM, D_MODEL), F32), pltpu.VMEM((2, ROUTE_TM, D_MODEL), F32),
                        pltpu.SemaphoreType.DMA((2,))],
        compiler_params=_params(),
        name="combine",
    )(*args, y)


def _ffn_kernel(hn_ref, acc_ref, *rest, pieces, final_norm):
    rest = list(rest)
    gf_ref = rest.pop(0) if final_norm else None
    wg_ref, wu_ref, wd_ref, o_ref, wgb, wub, wdb = rest
    step = pl.program_id(0)
    _stash_piece(step, wg_ref, wgb, pieces[0])
    _stash_piece(step, wu_ref, wub, pieces[0])
    _stash_piece(step, wd_ref, wdb, pieces[1])

    @pl.when(step >= max(pieces))
    def _():
        hn = hn_ref[...]
        gate = jnp.dot(hn, wgb[...], preferred_element_type=F32)
        up = jnp.dot(hn, wub[...], preferred_element_type=F32)
        h = (gate * (1.0 / (1.0 + jnp.exp(-gate)))) * up
        out = acc_ref[...] + jnp.dot(h.astype(BF16), wdb[...], preferred_element_type=F32)
        if final_norm:
            out = _rms(out, gf_ref[...])
        o_ref[...] = out


def _ffn_slab(hn, acc, wg, wu, wd, idx, slab, width, *, tm, g_final=None):
    m = hn.shape[0]
    nm = m // tm
    pieces = (8, 8)
    n_pro = max(pieces)
    rows_in = _piece_rows(D_MODEL, pieces[0])
    rows_dn = _piece_rows(width, pieces[1])
    tile = lambda i: jnp.maximum(i - n_pro, 0)
    row_spec = lambda w_: pl.BlockSpec((tm, w_), lambda i: (tile(i), 0))
    in_specs = [row_spec(D_MODEL), row_spec(D_MODEL)]
    args = [hn, acc]
    if g_final is not None:
        in_specs.append(pl.BlockSpec((1, D_MODEL), lambda i: (0, 0)))
        args.append(g_final)
    up_spec = pl.BlockSpec((None, rows_in, width), lambda i: (idx, jnp.minimum(i, pieces[0] - 1), slab))
    in_specs += [up_spec, up_spec,
                 pl.BlockSpec((None, rows_dn, D_MODEL),
                              lambda i: (idx, slab * pieces[1] + jnp.minimum(i, pieces[1] - 1), 0))]
    args += [wg, wu, wd]
    return pl.pallas_call(
        functools.partial(_ffn_kernel, pieces=pieces, final_norm=g_final is not None),
        grid=(n_pro + nm,),
        in_specs=in_specs,
        out_specs=row_spec(D_MODEL),
        out_shape=jax.ShapeDtypeStruct((m, D_MODEL), F32),
        scratch_shapes=[pltpu.VMEM((D_MODEL, width), BF16), pltpu.VMEM((D_MODEL, width), BF16),
                        pltpu.VMEM((width, D_MODEL), BF16)],
        compiler_params=_params(),
        name="ffn_slab",
    )(*args)


def _rope_tables(pos):
    half = ROPE_DIM // 2
    inv = ROPE_THETA ** (-jnp.arange(0, ROPE_DIM, 2, dtype=F32) / ROPE_DIM)
    ang = pos[:, None] * inv[None, :]
    cos, sin = jnp.cos(ang), jnp.sin(ang)
    n = pos.shape[0]
    rest = HEAD_DIM - ROPE_DIM
    a = jnp.concatenate([cos, cos, jnp.ones((n, rest), F32)], axis=1)
    b = jnp.concatenate([jnp.zeros((n, half), F32), sin, jnp.zeros((n, rest), F32)], axis=1)
    c = jnp.concatenate([-sin, jnp.zeros((n, half + rest), F32)], axis=1)
    reps = LANES // HEAD_DIM
    scale = HEAD_DIM ** -0.5
    return tuple(jnp.tile(t, (1, reps)) * scale for t in (a, b, c))


def kernel(x_prompt, x_sample, cache_k, cache_v, state_pool, norm_mix, w_in, attn_sinks, w_pool, pool_scale,
           w_out, norm_ffn, w_gate_dense, w_up_dense, w_down_dense, w_router, b_router, w_gate_exp, w_up_exp,
           w_down_exp, norm_final):
    tm = 512
    xp = x_prompt.reshape(BATCH * SEQ, D_MODEL)
    xs = x_sample.reshape(DEC_BATCH, D_MODEL)
    rope_p = _rope_tables(jnp.arange(SEQ, dtype=F32))
    rope_s = _rope_tables(jnp.full((DEC_BATCH,), PAST_LEN, dtype=F32))
    ck = cache_k.reshape(DEPTH, DEC_BATCH, WINDOW, KV_WIDTH)
    cv = cache_v.reshape(DEPTH, DEC_BATCH, WINDOW, KV_WIDTH)
    g_final = norm_final.reshape(1, D_MODEL)
    slab_w = D_FF // DENSE_SLABS
    heads = (N_KV_HEADS, HEAD_DIM)

    pool_state = state_pool
    nk_p, nv_p, nu_p = [], [], []
    for l in range(DEPTH):
        g_mix = norm_mix[l].reshape(1, D_MODEL)
        g_ffn = norm_ffn[l].reshape(1, D_MODEL)
        scale = pool_scale[l].reshape(1, POOL_WIDTH)
        moe = l % 2 == 1
        i = l // 2
        last = l == DEPTH - 1
        router = None
        if moe:
            router = (jnp.pad(w_router[i], ((0, 0), (0, ROUTER_PAD - N_EXPERTS))),
                      jnp.pad(b_router[i], (0, ROUTER_PAD - N_EXPERTS)).reshape(1, ROUTER_PAD))

        q, k, v, u, kvt, ut = _inproj(xp, g_mix, w_in, l, rope_p, tm=tm)
        mix = _mixer(q, k, v, u, attn_sinks[l], w_pool, scale, l)
        xp, hn, *routing = _outproj(mix, xp, w_out, g_ffn, l, router, tm=tm)
        nk_p.append(kvt[:, :KV_WIDTH].reshape(BATCH, WINDOW, *heads))
        nv_p.append(kvt[:, KV_WIDTH:].reshape(BATCH, WINDOW, *heads))
        nu_p.append(ut.reshape(BATCH, BF16_ROWS, POOL_WIDTH)[:, BF16_ROWS - POOL_STATE:])
        if moe:
            route, counts = routing
            dest, pad_rows, steps = _route_plan(route, counts)
            y = _experts(_dispatch(hn, dest, pad_rows), w_gate_exp, w_up_exp, w_down_exp, i, steps)
            xp = _combine(xp, route, y, dest, g_final if last else None)
        else:
            for s in range(DENSE_SLABS):
                xp = _ffn_slab(hn, xp, w_gate_dense, w_up_dense, w_down_dense, i, s, slab_w, tm=tm,
                               g_final=g_final if last and s == DENSE_SLABS - 1 else None)

        z = _sample_inproj(xs, g_mix, w_in, l, rope_s)
        a_s, ck, cv, pool_state, ps = _sample_mixer(
            z[:, :ATTN_WIDTH], z[:, ATTN_WIDTH:ATTN_WIDTH + KV_WIDTH],
            z[:, ATTN_WIDTH + KV_WIDTH:ATTN_WIDTH + 2 * KV_WIDTH], z[:, ATTN_WIDTH + 2 * KV_WIDTH:],
            ck, cv, pool_state, attn_sinks[l], w_pool, scale, l)
        mix_s = jnp.concatenate([a_s.reshape(DEC_BATCH, ATTN_WIDTH), ps], axis=1)
        xs = _sample_outproj(mix_s, xs, w_out, l)
        if moe:
            xs = _sample_ffn(xs, g_ffn, w_gate_exp, w_up_exp, w_down_exp, i, router=router,
                             g_final=g_final if last else None)
        else:
            xs = _sample_ffn(xs, g_ffn, w_gate_dense, w_up_dense, w_down_dense, i,
                             g_final=g_final if last else None)

    y_prompt = xp.reshape(BATCH, SEQ, D_MODEL)
    y_sample = xs.reshape(DEC_BATCH, 1, D_MODEL)
    return (y_prompt, y_sample, jnp.stack(nk_p), jnp.stack(nv_p), jnp.stack(nu_p),
            ck.reshape(DEPTH, DEC_BATCH, WINDOW, *heads), cv.reshape(DEPTH, DEC_BATCH, WINDOW, *heads), pool_state)
```

```python
import functools

import jax
import jax.numpy as jnp
from jax import lax
from jax.experimental import pallas as pl
from jax.experimental.pallas import tpu as pltpu

F32 = jnp.float32
BF16 = jnp.bfloat16

D_MODEL = 2048
BATCH = 2
SEQ = 4096
DEPTH = 4
DEC_BATCH = 32
PAST_LEN = 16384
N_HEADS = 16
N_KV_HEADS = 4
HEAD_DIM = 64
GROUP = N_HEADS // N_KV_HEADS
ATTN_WIDTH = N_HEADS * HEAD_DIM
KV_WIDTH = N_KV_HEADS * HEAD_DIM
WINDOW = 128
ROPE_DIM = HEAD_DIM // 4
ROPE_THETA = 500000.0
POOL_WINDOWS = (2, 4, 8, 16)
POOL_WIDTH = D_MODEL - ATTN_WIDTH
POOL_GROUP_DIM = POOL_WIDTH // len(POOL_WINDOWS)
POOL_STATE = max(POOL_WINDOWS) - 1
IN_WIDTH = ATTN_WIDTH + 2 * KV_WIDTH + POOL_WIDTH
D_FF = 5632
N_EXPERTS = 8
D_FF_EXPERT = D_MODEL // 2
EPS = 1e-5
NEG_INF = -1e30

LANES = 128
BF16_ROWS = 16
VMEM_LIMIT = 56 * 1024 * 1024
DENSE_SLABS = 4
ROUTER_PAD = LANES


def _params(n_axes=1, vmem=VMEM_LIMIT):
    return pltpu.CompilerParams(dimension_semantics=("arbitrary",) * n_axes, vmem_limit_bytes=vmem)


def _rms(x, g):
    ms = jnp.mean(x * x, axis=-1, keepdims=True)
    return (x * lax.rsqrt(ms + EPS)) * g


def _rope_mix(zc, ra, rb, rc):
    return zc * ra + pltpu.roll(zc, ROPE_DIM // 2, 1) * rb + pltpu.roll(zc, LANES - ROPE_DIM // 2, 1) * rc


def _split_bf16(a):
    hi = a.astype(BF16)
    return hi, (a - hi.astype(F32)).astype(BF16)


def _dot3(a, b, dims=(((1,), (0,)), ((), ()))):
    m = a.shape[0]
    ah, al = _split_bf16(a)
    bh, bl = _split_bf16(b)
    t = lax.dot_general(jnp.concatenate([ah, al], axis=0), bh, dims, preferred_element_type=F32)
    return t[:m] + t[m:] + lax.dot_general(ah, bl, dims, preferred_element_type=F32)


def _stash_piece(step, w_ref, wbf, n_pieces):
    rows = w_ref.shape[0]

    @pl.when(step < n_pieces)
    def _():
        r0 = pl.multiple_of(step * rows, rows)
        wbf[pl.ds(r0, rows), :] = w_ref[...].astype(BF16)


def _piece_rows(total_rows, n_pieces):
    rows, rem = divmod(total_rows, n_pieces)
    assert rem == 0 and rows % BF16_ROWS == 0, (total_rows, n_pieces)
    return rows


def _inproj_kernel(x_ref, g_ref, w_ref, ra_ref, rb_ref, rc_ref, q_ref, k_ref, v_ref, u_ref, kvt_ref, ut_ref,
                   wbf, *, n_pieces):
    step = pl.program_id(0)
    _stash_piece(step, w_ref, wbf, n_pieces)

    @pl.when(step >= n_pieces)
    def _():
        tm = x_ref.shape[0]
        h = _rms(x_ref[...], g_ref[...])
        z = jnp.dot(h.astype(BF16), wbf[...], preferred_element_type=F32)
        ra, rb, rc = ra_ref[...], rb_ref[...], rc_ref[...]
        n_rot = (ATTN_WIDTH + KV_WIDTH) // LANES
        rot = []
        for c in range(n_rot):
            r = _rope_mix(z[:, c * LANES:(c + 1) * LANES], ra, rb, rc)
            if c >= ATTN_WIDTH // LANES:
                r = r * (HEAD_DIM ** 0.5)
            rot.append(r)
        nq = ATTN_WIDTH // LANES
        for c in range(nq):
            q_ref[:, c * LANES:(c + 1) * LANES] = rot[c].astype(q_ref.dtype)
        for c in range(nq, n_rot):
            k_ref[:, (c - nq) * LANES:(c - nq + 1) * LANES] = rot[c].astype(k_ref.dtype)
        v = z[:, ATTN_WIDTH + KV_WIDTH:ATTN_WIDTH + 2 * KV_WIDTH]
        u = z[:, ATTN_WIDTH + 2 * KV_WIDTH:]
        v_ref[...] = v.astype(v_ref.dtype)
        u_ref[...] = u.astype(u_ref.dtype)
        for c in range(nq, n_rot):
            kvt_ref[:, (c - nq) * LANES:(c - nq + 1) * LANES] = rot[c][tm - WINDOW:, :]
        kvt_ref[:, KV_WIDTH:] = v[tm - WINDOW:, :]
        ut_ref[...] = u[tm - BF16_ROWS:, :]


def _inproj(x, g, w_in, layer, rope, *, tm):
    m = x.shape[0]
    nm = m // tm
    n_pieces = 8
    rows = _piece_rows(D_MODEL, n_pieces)
    tiles_per_seq = SEQ // tm
    tile = lambda i: jnp.maximum(i - n_pieces, 0)
    row_spec = lambda width: pl.BlockSpec((tm, width), lambda i: (tile(i), 0))
    rope_spec = pl.BlockSpec((tm, LANES), lambda i: (tile(i) % tiles_per_seq, 0))
    out_shape = [jax.ShapeDtypeStruct((m, ATTN_WIDTH), BF16),
                 jax.ShapeDtypeStruct((m, KV_WIDTH), BF16),
                 jax.ShapeDtypeStruct((m, KV_WIDTH), BF16),
                 jax.ShapeDtypeStruct((m, POOL_WIDTH), BF16),
                 jax.ShapeDtypeStruct((BATCH * WINDOW, 2 * KV_WIDTH), F32),
                 jax.ShapeDtypeStruct((BATCH * BF16_ROWS, POOL_WIDTH), F32)]
    out_specs = [row_spec(ATTN_WIDTH), row_spec(KV_WIDTH), row_spec(KV_WIDTH), row_spec(POOL_WIDTH),
                 pl.BlockSpec((WINDOW, 2 * KV_WIDTH), lambda i: (tile(i) // tiles_per_seq, 0)),
                 pl.BlockSpec((BF16_ROWS, POOL_WIDTH), lambda i: (tile(i) // tiles_per_seq, 0))]
    return pl.pallas_call(
        functools.partial(_inproj_kernel, n_pieces=n_pieces),
        grid=(n_pieces + nm,),
        in_specs=[row_spec(D_MODEL),
                  pl.BlockSpec((1, D_MODEL), lambda i: (0, 0)),
                  pl.BlockSpec((None, rows, IN_WIDTH), lambda i: (layer, jnp.minimum(i, n_pieces - 1), 0)),
                  rope_spec, rope_spec, rope_spec],
        out_specs=out_specs,
        out_shape=out_shape,
        scratch_shapes=[pltpu.VMEM((D_MODEL, IN_WIDTH), BF16)],
        compiler_params=_params(),
        name="inproj",
    )(x, g, w_in, *rope)


def _sink_softmax(s, sink, axis):
    mx = jnp.maximum(jnp.max(s, axis=axis, keepdims=True), sink)
    p = jnp.exp(s - mx)
    den = jnp.sum(p, axis=axis, keepdims=True) + jnp.exp(sink - mx)
    return p * (1.0 / den)


def _pool_group(d, wpb_g, scale_g):
    return jnp.dot(d.astype(BF16), wpb_g, preferred_element_type=F32) * scale_g


def _mixer_kernel(sink_ref, q_ref, kc_ref, kp_ref, vc_ref, vp_ref, uc_ref, up_ref, wp_ref, sc_ref,
                  o_ref, wpb, band):
    b = pl.program_id(0)
    n = pl.program_id(1)
    blk = WINDOW

    @pl.when((b == 0) & (n == 0))
    def _():
        wpb[...] = wp_ref[...].astype(BF16)
        tok = lax.broadcasted_iota(jnp.int32, (blk, 2 * blk), 0) + blk
        src = lax.broadcasted_iota(jnp.int32, (blk, 2 * blk), 1)
        for g, w in enumerate(POOL_WINDOWS):
            band[g] = jnp.where((src <= tok) & (src > tok - w), 1.0, 0.0).astype(BF16)

    half = LANES // 2
    kk = jnp.concatenate([kp_ref[...], kc_ref[...]], axis=0).astype(F32)
    vv = jnp.concatenate([vp_ref[...], vc_ref[...]], axis=0).astype(F32)
    lane = lax.broadcasted_iota(jnp.int32, (2 * blk, LANES), 1)
    low = lane < half
    key = lax.broadcasted_iota(jnp.int32, (4 * blk, blk), 0) & (2 * blk - 1)
    qry = lax.broadcasted_iota(jnp.int32, (4 * blk, blk), 1)
    valid = (key > qry) & (key <= qry + blk) & ((n > 0) | (key >= blk))

    for h in range(N_KV_HEADS):
        g0 = (h // 2) * LANES
        kg, vg = kk[:, g0:g0 + LANES], vv[:, g0:g0 + LANES]
        if h % 2 == 0:
            klo, vlo = jnp.where(low, kg, 0.0), jnp.where(low, vg, 0.0)
            khi, vhi = pltpu.roll(klo, half, 1), pltpu.roll(vlo, half, 1)
        else:
            khi, vhi = jnp.where(low, 0.0, kg), jnp.where(low, 0.0, vg)
            klo, vlo = pltpu.roll(khi, half, 1), pltpu.roll(vhi, half, 1)
        kcat = jnp.concatenate([klo, khi], axis=0).astype(BF16)
        vcat = jnp.concatenate([vlo, vhi], axis=0).astype(BF16)
        for pair in range(h * GROUP // 2, (h + 1) * GROUP // 2):
            qp = q_ref[:, pair * LANES:(pair + 1) * LANES]
            st = lax.dot_general(kcat, qp, (((1,), (1,)), ((), ())), preferred_element_type=F32)
            st = jnp.where(valid, st, NEG_INF)
            pt = [_sink_softmax(st[t * 2 * blk:(t + 1) * 2 * blk], sink_ref[2 * pair + t], 0).astype(BF16)
                  for t in range(2)]
            o = lax.dot_general(jnp.concatenate(pt, axis=0), vcat, (((0,), (0,)), ((), ())),
                                preferred_element_type=F32)
            o_ref[:, pair * LANES:(pair + 1) * LANES] = o.astype(o_ref.dtype)

    ext = jnp.concatenate([jnp.where(n > 0, up_ref[...], jnp.zeros_like(up_ref)), uc_ref[...]], axis=0)
    pos = (n * blk + lax.broadcasted_iota(jnp.int32, (blk, 1), 0)).astype(F32)
    for g, w in enumerate(POOL_WINDOWS):
        c0, c1 = g * POOL_GROUP_DIM, (g + 1) * POOL_GROUP_DIM
        wsum = jnp.dot(band[g], ext[:, c0:c1], preferred_element_type=F32)
        d = wsum * (1.0 / jnp.minimum(float(w), pos + 1.0)) - uc_ref[:, c0:c1].astype(F32)
        y = _pool_group(d, wpb[g], sc_ref[:, c0:c1])
        o_ref[:, ATTN_WIDTH + c0:ATTN_WIDTH + c1] = y.astype(o_ref.dtype)


def _mixer(q, k, v, u, sinks, w_pool, scale, layer):
    assert u.dtype == BF16
    nb = SEQ // WINDOW
    cur = lambda b, n: (b * nb + n, 0)
    prev = lambda b, n: (b * nb + jnp.maximum(n - 1, 0), 0)
    n_grp = len(POOL_WINDOWS)
    return pl.pallas_call(
        _mixer_kernel,
        grid=(BATCH, nb),
        in_specs=[pl.BlockSpec(memory_space=pltpu.SMEM),
                  pl.BlockSpec((WINDOW, ATTN_WIDTH), cur),
                  pl.BlockSpec((WINDOW, KV_WIDTH), cur), pl.BlockSpec((WINDOW, KV_WIDTH), prev),
                  pl.BlockSpec((WINDOW, KV_WIDTH), cur), pl.BlockSpec((WINDOW, KV_WIDTH), prev),
                  pl.BlockSpec((WINDOW, POOL_WIDTH), cur), pl.BlockSpec((WINDOW, POOL_WIDTH), prev),
                  pl.BlockSpec((None, n_grp, POOL_GROUP_DIM, POOL_GROUP_DIM), lambda b, n: (layer, 0, 0, 0)),
                  pl.BlockSpec((1, POOL_WIDTH), lambda b, n: (0, 0))],
        out_specs=pl.BlockSpec((WINDOW, D_MODEL), cur),
        out_shape=jax.ShapeDtypeStruct((BATCH * SEQ, D_MODEL), BF16),
        scratch_shapes=[pltpu.VMEM((n_grp, POOL_GROUP_DIM, POOL_GROUP_DIM), BF16),
                        pltpu.VMEM((n_grp, WINDOW, 2 * WINDOW), BF16)],
        compiler_params=_params(2),
        name="mixer",
    )(sinks, q, k, k, v, v, u, u, w_pool, scale)


SAMPLE_SB = 4


def _sample_mixer_kernel(sink_ref, q_ref, kn_ref, vn_ref, un_ref, ck_ref, cv_ref, st_ref, wp_ref, sc_ref,
                         a_ref, nk_ref, nv_ref, nu_ref, po_ref, dscr):
    step = pl.program_id(0)
    sink = sink_ref[...]
    own = (lax.broadcasted_iota(jnp.int32, (N_HEADS, KV_WIDTH), 0) // GROUP
           == lax.broadcasted_iota(jnp.int32, (N_HEADS, KV_WIDTH), 1) // HEAD_DIM)
    for j in range(SAMPLE_SB):
        nk_ref[j, 0:WINDOW - 1, :] = ck_ref[j, 1:WINDOW, :]
        nk_ref[j, WINDOW - 1:WINDOW, :] = kn_ref[j]
        nv_ref[j, 0:WINDOW - 1, :] = cv_ref[j, 1:WINDOW, :]
        nv_ref[j, WINDOW - 1:WINDOW, :] = vn_ref[j]

        q = q_ref[j]
        qe = jnp.where(own, jnp.concatenate([q] * N_KV_HEADS, axis=1), 0.0)
        s = _dot3(qe, nk_ref[j], (((1,), (1,)), ((), ())))
        r = _dot3(_sink_softmax(s, sink, 1), nv_ref[j])
        r = jnp.where(own, r, 0.0)
        o = r[:, 0:HEAD_DIM]
        for h in range(1, N_KV_HEADS):
            o = o + r[:, h * HEAD_DIM:(h + 1) * HEAD_DIM]
        a_ref[j] = o

        st = st_ref[j]
        un = un_ref[j]
        nu_ref[j, 0:POOL_STATE - 1, :] = st[1:POOL_STATE, :]
        nu_ref[j, POOL_STATE - 1:POOL_STATE, :] = un
        for g, w in enumerate(POOL_WINDOWS):
            c0, c1 = g * POOL_GROUP_DIM, (g + 1) * POOL_GROUP_DIM
            tok = un[:, c0:c1]
            wsum = tok + jnp.sum(st[POOL_STATE - (w - 1):POOL_STATE, c0:c1], axis=0, keepdims=True)
            dscr[pl.ds(step * SAMPLE_SB + j, 1), c0:c1] = wsum / float(min(w, PAST_LEN + 1)) - tok

    @pl.when(step == pl.num_programs(0) - 1)
    def _():
        for g in range(len(POOL_WINDOWS)):
            c0, c1 = g * POOL_GROUP_DIM, (g + 1) * POOL_GROUP_DIM
            po_ref[:, c0:c1] = _dot3(dscr[:, c0:c1], wp_ref[g]) * sc_ref[:, c0:c1]


def _sample_mixer(q, k, v, u, cache_k, cache_v, state, sinks, w_pool, scale, layer):
    nb, sb = DEC_BATCH, SAMPLE_SB
    per_b = lambda *shape: pl.BlockSpec((sb,) + shape, lambda b: (b,) + (0,) * len(shape))
    per_lb = lambda *shape: pl.BlockSpec((None, sb) + shape, lambda b: (layer, b) + (0,) * len(shape))
    return pl.pallas_call(
        _sample_mixer_kernel,
        grid=(nb // sb,),
        in_specs=[pl.BlockSpec((N_HEADS, 1), lambda b: (0, 0)),
                  per_b(N_HEADS, HEAD_DIM), per_b(1, KV_WIDTH), per_b(1, KV_WIDTH), per_b(1, POOL_WIDTH),
                  per_lb(WINDOW, KV_WIDTH), per_lb(WINDOW, KV_WIDTH), per_lb(POOL_STATE, POOL_WIDTH),
                  pl.BlockSpec((None, len(POOL_WINDOWS), POOL_GROUP_DIM, POOL_GROUP_DIM),
                               lambda b: (layer, 0, 0, 0)),
                  pl.BlockSpec((1, POOL_WIDTH), lambda b: (0, 0))],
        out_specs=[per_b(N_HEADS, HEAD_DIM), per_lb(WINDOW, KV_WIDTH), per_lb(WINDOW, KV_WIDTH),
                   per_lb(POOL_STATE, POOL_WIDTH), pl.BlockSpec((nb, POOL_WIDTH), lambda b: (0, 0))],
        out_shape=[jax.ShapeDtypeStruct((nb, N_HEADS, HEAD_DIM), F32),
                   jax.ShapeDtypeStruct(cache_k.shape, F32),
                   jax.ShapeDtypeStruct(cache_v.shape, F32),
                   jax.ShapeDtypeStruct(state.shape, F32),
                   jax.ShapeDtypeStruct((nb, POOL_WIDTH), F32)],
        scratch_shapes=[pltpu.VMEM((nb, POOL_WIDTH), F32)],
        input_output_aliases={5: 1, 6: 2, 7: 3},
        compiler_params=_params(),
        name="sample_mixer",
    )(sinks.reshape(N_HEADS, 1), q.reshape(nb, N_HEADS, HEAD_DIM), k.reshape(nb, 1, KV_WIDTH),
      v.reshape(nb, 1, KV_WIDTH), u.reshape(nb, 1, POOL_WIDTH), cache_k, cache_v, state, w_pool, scale)


SAMPLE_TN = 512


def _sample_inproj_kernel(x_ref, g_ref, w_ref, ra_ref, rb_ref, rc_ref, z_ref):
    j = pl.program_id(0)
    z = _dot3(_rms(x_ref[...], g_ref[...]), w_ref[...])
    ra, rb, rc = ra_ref[...], rb_ref[...], rc_ref[...]
    per_tile = SAMPLE_TN // LANES
    for c in range(per_tile):
        zc = z[:, c * LANES:(c + 1) * LANES]
        grp = j * per_tile + c
        r = _rope_mix(zc, ra, rb, rc)
        r = jnp.where(grp >= ATTN_WIDTH // LANES, r * (HEAD_DIM ** 0.5), r)
        z_ref[:, c * LANES:(c + 1) * LANES] = jnp.where(grp < (ATTN_WIDTH + KV_WIDTH) // LANES, r, zc)


def _sample_inproj(x, g, w_in, layer, rope):
    m = x.shape[0]
    const = lambda r, c: pl.BlockSpec((r, c), lambda j: (0, 0))
    return pl.pallas_call(
        _sample_inproj_kernel,
        grid=(IN_WIDTH // SAMPLE_TN,),
        in_specs=[const(m, D_MODEL), const(1, D_MODEL),
                  pl.BlockSpec((None, D_MODEL, SAMPLE_TN), lambda j: (layer, 0, j)),
                  const(m, LANES), const(m, LANES), const(m, LANES)],
        out_specs=pl.BlockSpec((m, SAMPLE_TN), lambda j: (0, j)),
        out_shape=jax.ShapeDtypeStruct((m, IN_WIDTH), F32),
        compiler_params=_params(),
        name="sample_inproj",
    )(x, g, w_in, *rope)


def _sample_outproj_kernel(mix_ref, x_ref, w_ref, o_ref):
    o_ref[...] = x_ref[...] + _dot3(mix_ref[...], w_ref[...])


def _sample_outproj(mix, x, w_out, layer):
    m = x.shape[0]
    col = pl.BlockSpec((m, SAMPLE_TN), lambda j: (0, j))
    return pl.pallas_call(
        _sample_outproj_kernel,
        grid=(D_MODEL // SAMPLE_TN,),
        in_specs=[pl.BlockSpec((m, D_MODEL), lambda j: (0, 0)), col,
                  pl.BlockSpec((None, D_MODEL, SAMPLE_TN), lambda j: (layer, 0, j))],
        out_specs=col,
        out_shape=jax.ShapeDtypeStruct((m, D_MODEL), F32),
        compiler_params=_params(),
        name="sample_outproj",
    )(mix, x, w_out)


def _sample_ffn_kernel(x_ref, g_ref, *rest, moe, final_norm):
    rest = list(rest)
    wr_ref, br_ref = (rest.pop(0), rest.pop(0)) if moe else (None, None)
    gf_ref = rest.pop(0) if final_norm else None
    wg_ref, wu_ref, wd_ref, o_ref, acc, hn_scr, cmb = rest
    e, f = pl.program_id(0), pl.program_id(1)

    @pl.when((e == 0) & (f == 0))
    def _():
        hn = _rms(x_ref[...], g_ref[...])
        hn_scr[...] = hn
        acc[...] = jnp.zeros_like(acc)
        if moe:
            cmb[...] = _top2_gates(_dot3(hn, wr_ref[...]) + br_ref[...])

    hn = hn_scr[...]
    gate = _dot3(hn, wg_ref[...])
    up = _dot3(hn, wu_ref[...])
    h = (gate * (1.0 / (1.0 + jnp.exp(-gate)))) * up
    if moe:
        lane = lax.broadcasted_iota(jnp.int32, cmb.shape, 1)
        h = h * jnp.sum(jnp.where(lane == e, cmb[...], 0.0), axis=1, keepdims=True)
    acc[...] += _dot3(h, wd_ref[...])

    @pl.when((e == pl.num_programs(0) - 1) & (f == pl.num_programs(1) - 1))
    def _():
        out = x_ref[...] + acc[...]
        if final_norm:
            out = _rms(out, gf_ref[...])
        o_ref[...] = out


def _sample_ffn(x, g, wg, wu, wd, idx, *, router=None, g_final=None):
    m = x.shape[0]
    moe = router is not None
    n_e = N_EXPERTS if moe else 1
    width = wg.shape[-1]
    const = lambda r, c: pl.BlockSpec((r, c), lambda e, f: (0, 0))
    in_specs = [const(m, D_MODEL), const(1, D_MODEL)]
    args = [x, g]
    if moe:
        in_specs += [const(D_MODEL, ROUTER_PAD), const(1, ROUTER_PAD)]
        args += list(router)
        up_spec = pl.BlockSpec((None, None, D_MODEL, SAMPLE_TN), lambda e, f: (idx, e, 0, f))
        dn_spec = pl.BlockSpec((None, None, SAMPLE_TN, D_MODEL), lambda e, f: (idx, e, f, 0))
    else:
        up_spec = pl.BlockSpec((None, D_MODEL, SAMPLE_TN), lambda e, f: (idx, 0, f))
        dn_spec = pl.BlockSpec((None, SAMPLE_TN, D_MODEL), lambda e, f: (idx, f, 0))
    if g_final is not None:
        in_specs.append(const(1, D_MODEL))
        args.append(g_final)
    return pl.pallas_call(
        functools.partial(_sample_ffn_kernel, moe=moe, final_norm=g_final is not None),
        grid=(n_e, width // SAMPLE_TN),
        in_specs=in_specs + [up_spec, up_spec, dn_spec],
        out_specs=const(m, D_MODEL),
        out_shape=jax.ShapeDtypeStruct((m, D_MODEL), F32),
        scratch_shapes=[pltpu.VMEM((m, D_MODEL), F32), pltpu.VMEM((m, D_MODEL), F32),
                        pltpu.VMEM((m, ROUTER_PAD), F32)],
        compiler_params=_params(2),
        name="sample_ffn",
    )(*args, wg, wu, wd)


def _top2(logits):
    lane = lax.broadcasted_iota(jnp.int32, logits.shape, 1)
    lg = jnp.where(lane < N_EXPERTS, logits, -jnp.inf)
    m1 = jnp.max(lg, axis=1, keepdims=True)
    i1 = jnp.min(jnp.where(lg == m1, lane, ROUTER_PAD), axis=1, keepdims=True)
    lg2 = jnp.where(lane == i1, -jnp.inf, lg)
    m2 = jnp.max(lg2, axis=1, keepdims=True)
    i2 = jnp.min(jnp.where(lg2 == m2, lane, ROUTER_PAD), axis=1, keepdims=True)
    e2 = jnp.exp(m2 - m1)
    den = 1.0 + e2
    return i1, i2, 1.0 / den, e2 / den


def _top2_gates(logits):
    i1, i2, g1, g2 = _top2(logits)
    lane = lax.broadcasted_iota(jnp.int32, logits.shape, 1)
    return jnp.where(lane == i1, g1, 0.0) + jnp.where(lane == i2, g2, 0.0)


ROUTE_I1, ROUTE_I2, ROUTE_G1, ROUTE_G2, ROUTE_P1, ROUTE_P2 = range(6)


def _outproj_kernel(mix_ref, x_ref, w_ref, g_ref, *rest, n_pieces, with_router):
    if with_router:
        wr_ref, br_ref, xo_ref, hn_ref, route_ref, cnt_ref, wbf, tri, carry = rest
    else:
        xo_ref, hn_ref, wbf = rest
    step = pl.program_id(0)
    _stash_piece(step, w_ref, wbf, n_pieces)

    if with_router:
        @pl.when(step == 0)
        def _():
            row = lax.broadcasted_iota(jnp.int32, tri.shape, 0)
            col = lax.broadcasted_iota(jnp.int32, tri.shape, 1)
            tri[...] = jnp.where(col < row, 1.0, 0.0).astype(BF16)
            carry[...] = jnp.zeros_like(carry)

    @pl.when(step >= n_pieces)
    def _():
        y = jnp.dot(mix_ref[...], wbf[...], preferred_element_type=F32)
        xo = x_ref[...] + y
        xo_ref[...] = xo
        hn = _rms(xo, g_ref[...])
        if not with_router:
            hn_ref[...] = hn.astype(BF16)
            return
        hn_ref[...] = hn
        logits = jnp.dot(hn.astype(BF16), wr_ref[...].astype(BF16), preferred_element_type=F32) + br_ref[...]
        i1, i2, g1, g2 = _top2(logits)
        lane = lax.broadcasted_iota(jnp.int32, logits.shape, 1)
        picked = jnp.where((lane == i1) | (lane == i2), 1.0, 0.0)
        ahead = jnp.dot(tri[...], picked.astype(BF16), preferred_element_type=F32) + carry[...]
        p1 = jnp.sum(jnp.where(lane == i1, ahead, 0.0), axis=1, keepdims=True)
        p2 = jnp.sum(jnp.where(lane == i2, ahead, 0.0), axis=1, keepdims=True)
        carry[...] += jnp.sum(picked, axis=0, keepdims=True)
        cnt_ref[...] = carry[...]
        rec = jnp.zeros(logits.shape, F32)
        for k, val in ((ROUTE_I1, i1.astype(F32)), (ROUTE_I2, i2.astype(F32)), (ROUTE_G1, g1), (ROUTE_G2, g2),
                       (ROUTE_P1, p1), (ROUTE_P2, p2)):
            rec = jnp.where(lane == k, val, rec)
        route_ref[...] = rec


def _outproj(mix, x, w_out, g, layer, router, *, tm):
    m = x.shape[0]
    nm = m // tm
    n_pieces = 8
    rows = _piece_rows(D_MODEL, n_pieces)
    tile = lambda i: jnp.maximum(i - n_pieces, 0)
    row_spec = lambda width: pl.BlockSpec((tm, width), lambda i: (tile(i), 0))
    const = lambda r, c: pl.BlockSpec((r, c), lambda i: (0, 0))
    in_specs = [row_spec(D_MODEL), row_spec(D_MODEL),
                pl.BlockSpec((None, rows, D_MODEL), lambda i: (layer, jnp.minimum(i, n_pieces - 1), 0)),
                const(1, D_MODEL)]
    args = [mix, x, w_out, g]
    scratch = [pltpu.VMEM((D_MODEL, D_MODEL), BF16)]
    if router is None:
        out_shape = [jax.ShapeDtypeStruct((m, D_MODEL), F32), jax.ShapeDtypeStruct((m, D_MODEL), BF16)]
        out_specs = [row_spec(D_MODEL), row_spec(D_MODEL)]
    else:
        in_specs += [const(D_MODEL, ROUTER_PAD), const(1, ROUTER_PAD)]
        args += list(router)
        out_shape = [jax.ShapeDtypeStruct((m, D_MODEL), F32), jax.ShapeDtypeStruct((m, D_MODEL), F32),
                     jax.ShapeDtypeStruct((m, ROUTER_PAD), F32), jax.ShapeDtypeStruct((1, ROUTER_PAD), F32)]
        out_specs = [row_spec(D_MODEL), row_spec(D_MODEL), row_spec(ROUTER_PAD), const(1, ROUTER_PAD)]
        scratch += [pltpu.VMEM((tm, tm), BF16), pltpu.VMEM((1, ROUTER_PAD), F32)]
    return pl.pallas_call(
        functools.partial(_outproj_kernel, n_pieces=n_pieces, with_router=router is not None),
        grid=(n_pieces + nm,),
        in_specs=in_specs,
        out_specs=out_specs,
        out_shape=out_shape,
        scratch_shapes=scratch,
        compiler_params=_params(),
        name="outproj",
    )(*args)


EXPERT_TM = 512
EXPERT_PIECES = 4
ROUTE_TM = 256
SORTED_ROWS = BATCH * SEQ * 2 + N_EXPERTS * EXPERT_TM
EXPERT_STEPS = N_EXPERTS * EXPERT_PIECES + SORTED_ROWS // EXPERT_TM
EXPERT_VMEM_LIMIT = 60 * 1024 * 1024


def _take(table, idx):
    hot = idx[..., None] == jnp.arange(table.shape[0], dtype=idx.dtype)
    return jnp.sum(jnp.where(hot, table, 0), axis=-1)


def _route_plan(route, counts):
    i32 = jnp.int32
    t, p = EXPERT_TM, EXPERT_PIECES
    cnt = counts[0, :N_EXPERTS].astype(i32)
    ntile = (cnt + t - 1) // t
    tile0 = jnp.cumsum(ntile) - ntile
    experts = route[:, ROUTE_I1:ROUTE_I2 + 1].astype(i32)
    dest = _take(tile0, experts) * t + route[:, ROUTE_P1:ROUTE_P2 + 1].astype(i32)
    n_all = SORTED_ROWS // t
    last = N_EXPERTS - 1
    ids = jnp.arange(N_EXPERTS, dtype=i32)
    blen = jnp.where(ids < last, jnp.maximum(ntile, p), ntile)
    bstart = p + jnp.cumsum(blen) - blen
    total = p + jnp.sum(blen)
    n_tiles = jnp.sum(ntile)
    s = jnp.arange(EXPERT_STEPS, dtype=i32)
    pro = s < p
    e = jnp.clip(jnp.sum((s[:, None] >= bstart[None, :]).astype(i32), axis=1) - 1, 0, last)
    k = s - _take(bstart, e)
    live = jnp.logical_not(pro) & (s < total)
    comp = live & (k < _take(ntile, e))
    stage = pro | (live & (k < p) & (e < last))
    w_expert = jnp.where(pro, 0, jnp.minimum(e + 1, last))
    w_piece = jnp.where(pro, s, jnp.where(stage, k, p - 1))
    first = _take(tile0, e)
    spare = n_tiles + s - total
    tile = jnp.where(comp, first + k,
                     jnp.where(pro, 0, jnp.where(live, first + _take(ntile, e), spare)))
    tile = jnp.minimum(tile, n_all - 1)
    act = jnp.where(comp, 1, jnp.where((s >= total) & (spare < n_all), 2, 0))
    nvalid = jnp.clip(_take(cnt, e) - (tile - first) * t, 0, t)
    steps = (w_expert, w_piece, stage.astype(i32), tile, act, nvalid, e % 2)
    dest = dest.reshape(dest.shape[0] // ROUTE_TM, 1, 2 * ROUTE_TM)
    partial = jnp.where(ntile > 0, tile0 + ntile - 1, -1)
    unused = n_tiles + jnp.arange(N_EXPERTS, dtype=i32)
    pad_rows = jnp.concatenate([partial, jnp.where(unused < n_all, unused, -1)]) * t
    return dest, pad_rows, steps


def _start_rows(n, copies_of):
    def start(t, c):
        for cp in copies_of(t):
            cp.start()
        return c

    lax.fori_loop(0, n, start, 0, unroll=8)


def _wait_rows(n, copies_of):
    def wait(t, c):
        for cp in copies_of(t):
            cp.wait()
        return c

    lax.fori_loop(0, n, wait, 0, unroll=8)


def _dispatch_kernel(dest_ref, pad_ref, hn_ref, xg_hbm, zeros, sem):
    step = pl.program_id(0)

    @pl.when(step == 0)
    def _():
        zeros[...] = jnp.zeros_like(zeros)

        def fill(j):
            row = pl.multiple_of(jnp.maximum(pad_ref[j], 0), EXPERT_TM)
            return pltpu.make_async_copy(zeros, xg_hbm.at[pl.ds(row, EXPERT_TM)], sem)

        for j in range(pad_ref.shape[0]):
            pl.when(pad_ref[j] >= 0)(lambda j=j: fill(j).start())
        for j in range(pad_ref.shape[0]):
            pl.when(pad_ref[j] >= 0)(lambda j=j: fill(j).wait())

    def copies_of(t):
        src = hn_ref.at[pl.ds(t, 1)]
        return [pltpu.make_async_copy(src, xg_hbm.at[pl.ds(dest_ref[0, 2 * t + k], 1)], sem) for k in range(2)]

    _start_rows(ROUTE_TM, copies_of)
    _wait_rows(ROUTE_TM, copies_of)


def _dispatch(hn, dest, pad_rows):
    return pl.pallas_call(
        _dispatch_kernel,
        grid=(hn.shape[0] // ROUTE_TM,),
        in_specs=[pl.BlockSpec((None, 1, 2 * ROUTE_TM), lambda s: (s, 0, 0), memory_space=pltpu.SMEM),
                  pl.BlockSpec(memory_space=pltpu.SMEM),
                  pl.BlockSpec((ROUTE_TM, D_MODEL), lambda s: (s, 0))],
        out_specs=pl.BlockSpec(memory_space=pl.ANY),
        out_shape=jax.ShapeDtypeStruct((SORTED_ROWS, D_MODEL), F32),
        scratch_shapes=[pltpu.VMEM((EXPERT_TM, D_MODEL), F32), pltpu.SemaphoreType.DMA],
        compiler_params=_params(),
        name="dispatch",
    )(dest, pad_rows, hn)


def _expert_kernel(st_we, st_wp, st_stage, st_t, st_act, st_nv, st_slot, x_ref, wg_ref, wu_ref, wd_ref, y_ref,
                   wgb, wub, wdb):
    s = pl.program_id(0)

    @pl.when(st_stage[s] == 1)
    def _():
        slot = st_we[s] & 1
        for w_ref, wbf in ((wg_ref, wgb), (wu_ref, wub), (wd_ref, wdb)):
            rows = w_ref.shape[0]
            r0 = pl.multiple_of(st_wp[s] * rows, rows)
            wbf[slot, pl.ds(r0, rows), :] = w_ref[...].astype(BF16)

    act = st_act[s]

    @pl.when(act == 1)
    def _():
        slot = st_slot[s]
        row = lax.broadcasted_iota(jnp.int32, (x_ref.shape[0], 1), 0)
        x = jnp.where(row < st_nv[s], x_ref[...], 0.0).astype(BF16)
        gate = jnp.dot(x, wgb[slot], preferred_element_type=F32)
        up = jnp.dot(x, wub[slot], preferred_element_type=F32)
        h = (gate * (1.0 / (1.0 + jnp.exp(-gate)))) * up
        y_ref[...] = jnp.dot(h.astype(BF16), wdb[slot], preferred_element_type=F32)

    @pl.when(act == 2)
    def _():
        y_ref[...] = jnp.zeros_like(y_ref)


def _experts(xg, wg, wu, wd, idx, steps):
    t, p = EXPERT_TM, EXPERT_PIECES
    rows_in = _piece_rows(D_MODEL, p)
    rows_dn = _piece_rows(D_FF_EXPERT, p)
    w_map = lambda s, we, wp, *_: (idx, we[s], wp[s], 0)
    row_spec = pl.BlockSpec((t, D_MODEL), lambda s, we, wp, sg, tl, *_: (tl[s], 0))
    up_spec = pl.BlockSpec((None, None, rows_in, D_FF_EXPERT), w_map)
    return pl.pallas_call(
        _expert_kernel,
        grid_spec=pltpu.PrefetchScalarGridSpec(
            num_scalar_prefetch=len(steps),
            grid=(EXPERT_STEPS,),
            in_specs=[row_spec, up_spec, up_spec, pl.BlockSpec((None, None, rows_dn, D_MODEL), w_map)],
            out_specs=row_spec,
            scratch_shapes=[pltpu.VMEM((2, D_MODEL, D_FF_EXPERT), BF16), pltpu.VMEM((2, D_MODEL, D_FF_EXPERT), BF16),
                            pltpu.VMEM((2, D_FF_EXPERT, D_MODEL), BF16)]),
        out_shape=jax.ShapeDtypeStruct((SORTED_ROWS, D_MODEL), F32),
        compiler_params=_params(vmem=EXPERT_VMEM_LIMIT),
        name="experts",
    )(*steps, xg, wg, wu, wd)


def _combine_kernel(dest_ref, next_ref, x_ref, route_ref, *rest, final_norm):
    if final_norm:
        gf_ref, y_hbm, o_ref, ya, yb, sem = rest
    else:
        y_hbm, o_ref, ya, yb, sem = rest
    step = pl.program_id(0)
    n = x_ref.shape[0]
    slot = step % 2

    def gather(idx_ref, slot):
        def copies_of(t):
            return [pltpu.make_async_copy(y_hbm.at[pl.ds(idx_ref[0, 2 * t + k], 1)], buf.at[slot, pl.ds(t, 1)],
                                          sem.at[slot])
                    for k, buf in enumerate((ya, yb))]
        return copies_of

    @pl.when(step == 0)
    def _():
        _start_rows(n, gather(dest_ref, 0))

    @pl.when(step + 1 < pl.num_programs(0))
    def _():
        _start_rows(n, gather(next_ref, 1 - slot))

    _wait_rows(n, gather(dest_ref, slot))
    out = (x_ref[...] + route_ref[:, ROUTE_G1:ROUTE_G1 + 1] * ya[slot]
           + route_ref[:, ROUTE_G2:ROUTE_G2 + 1] * yb[slot])
    if final_norm:
        out = _rms(out, gf_ref[...])
    o_ref[...] = out


def _combine(x, route, y, dest, g_final):
    m = x.shape[0]
    n_steps = m // ROUTE_TM
    row_spec = lambda width: pl.BlockSpec((ROUTE_TM, width), lambda s: (s, 0))
    idx_spec = lambda ahead: pl.BlockSpec((None, 1, 2 * ROUTE_TM),
                                          lambda s: (jnp.minimum(s + ahead, n_steps - 1), 0, 0),
                                          memory_space=pltpu.SMEM)
    in_specs = [idx_spec(0), idx_spec(1), row_spec(D_MODEL), row_spec(ROUTER_PAD)]
    args = [dest, dest, x, route]
    if g_final is not None:
        in_specs.append(pl.BlockSpec((1, D_MODEL), lambda s: (0, 0)))
        args.append(g_final)
    return pl.pallas_call(
        functools.partial(_combine_kernel, final_norm=g_final is not None),
        grid=(n_steps,),
        in_specs=in_specs + [pl.BlockSpec(memory_space=pl.ANY)],
        out_specs=row_spec(D_MODEL),
        out_shape=jax.ShapeDtypeStruct((m, D_MODEL), F32),
        scratch_shapes=[pltpu.VMEM((2, ROUTE_TM, D_MODEL), F32), pltpu.VMEM((2, ROUTE_TM, D_MODEL), F32),
                        pltpu.SemaphoreType.DMA((2,))],
        compiler_params=_params(),
        name="combine",
    )(*args, y)


def _ffn_kernel(hn_ref, acc_ref, *rest, pieces, final_norm):
    rest = list(rest)
    gf_ref = rest.pop(0) if final_norm else None
    wg_ref, wu_ref, wd_ref, o_ref, wgb, wub, wdb = rest
    step = pl.program_id(0)
    _stash_piece(step, wg_ref, wgb, pieces[0])
    _stash_piece(step, wu_ref, wub, pieces[0])
    _stash_piece(step, wd_ref, wdb, pieces[1])

    @pl.when(step >= max(pieces))
    def _():
        hn = hn_ref[...]
        gate = jnp.dot(hn, wgb[...], preferred_element_type=F32)
        up = jnp.dot(hn, wub[...], preferred_element_type=F32)
        h = (gate * (1.0 / (1.0 + jnp.exp(-gate)))) * up
        out = acc_ref[...] + jnp.dot(h.astype(BF16), wdb[...], preferred_element_type=F32)
        if final_norm:
            out = _rms(out, gf_ref[...])
        o_ref[...] = out


def _ffn_slab(hn, acc, wg, wu, wd, idx, slab, width, *, tm, g_final=None):
    m = hn.shape[0]
    nm = m // tm
    pieces = (8, 8)
    n_pro = max(pieces)
    rows_in = _piece_rows(D_MODEL, pieces[0])
    rows_dn = _piece_rows(width, pieces[1])
    tile = lambda i: jnp.maximum(i - n_pro, 0)
    row_spec = lambda w_: pl.BlockSpec((tm, w_), lambda i: (tile(i), 0))
    in_specs = [row_spec(D_MODEL), row_spec(D_MODEL)]
    args = [hn, acc]
    if g_final is not None:
        in_specs.append(pl.BlockSpec((1, D_MODEL), lambda i: (0, 0)))
        args.append(g_final)
    up_spec = pl.BlockSpec((None, rows_in, width), lambda i: (idx, jnp.minimum(i, pieces[0] - 1), slab))
    in_specs += [up_spec, up_spec,
                 pl.BlockSpec((None, rows_dn, D_MODEL),
                              lambda i: (idx, slab * pieces[1] + jnp.minimum(i, pieces[1] - 1), 0))]
    args += [wg, wu, wd]
    return pl.pallas_call(
        functools.partial(_ffn_kernel, pieces=pieces, final_norm=g_final is not None),
        grid=(n_pro + nm,),
        in_specs=in_specs,
        out_specs=row_spec(D_MODEL),
        out_shape=jax.ShapeDtypeStruct((m, D_MODEL), F32),
        scratch_shapes=[pltpu.VMEM((D_MODEL, width), BF16), pltpu.VMEM((D_MODEL, width), BF16),
                        pltpu.VMEM((width, D_MODEL), BF16)],
        compiler_params=_params(),
        name="ffn_slab",
    )(*args)


def _rope_tables(pos):
    half = ROPE_DIM // 2
    inv = ROPE_THETA ** (-jnp.arange(0, ROPE_DIM, 2, dtype=F32) / ROPE_DIM)
    ang = pos[:, None] * inv[None, :]
    cos, sin = jnp.cos(ang), jnp.sin(ang)
    n = pos.shape[0]
    rest = HEAD_DIM - ROPE_DIM
    a = jnp.concatenate([cos, cos, jnp.ones((n, rest), F32)], axis=1)
    b = jnp.concatenate([jnp.zeros((n, half), F32), sin, jnp.zeros((n, rest), F32)], axis=1)
    c = jnp.concatenate([-sin, jnp.zeros((n, half + rest), F32)], axis=1)
    reps = LANES // HEAD_DIM
    scale = HEAD_DIM ** -0.5
    return tuple(jnp.tile(t, (1, reps)) * scale for t in (a, b, c))


def kernel(x_prompt, x_sample, cache_k, cache_v, state_pool, norm_mix, w_in, attn_sinks, w_pool, pool_scale,
           w_out, norm_ffn, w_gate_dense, w_up_dense, w_down_dense, w_router, b_router, w_gate_exp, w_up_exp,
           w_down_exp, norm_final):
    tm = 512
    xp = x_prompt.reshape(BATCH * SEQ, D_MODEL)
    xs = x_sample.reshape(DEC_BATCH, D_MODEL)
    rope_p = _rope_tables(jnp.arange(SEQ, dtype=F32))
    rope_s = _rope_tables(jnp.full((DEC_BATCH,), PAST_LEN, dtype=F32))
    ck = cache_k.reshape(DEPTH, DEC_BATCH, WINDOW, KV_WIDTH)
    cv = cache_v.reshape(DEPTH, DEC_BATCH, WINDOW, KV_WIDTH)
    g_final = norm_final.reshape(1, D_MODEL)
    slab_w = D_FF // DENSE_SLABS
    heads = (N_KV_HEADS, HEAD_DIM)

    pool_state = state_pool
    nk_p, nv_p, nu_p = [], [], []
    for l in range(DEPTH):
        g_mix = norm_mix[l].reshape(1, D_MODEL)
        g_ffn = norm_ffn[l].reshape(1, D_MODEL)
        scale = pool_scale[l].reshape(1, POOL_WIDTH)
        moe = l % 2 == 1
        i = l // 2
        last = l == DEPTH - 1
        router = None
        if moe:
            router = (jnp.pad(w_router[i], ((0, 0), (0, ROUTER_PAD - N_EXPERTS))),
                      jnp.pad(b_router[i], (0, ROUTER_PAD - N_EXPERTS)).reshape(1, ROUTER_PAD))

        q, k, v, u, kvt, ut = _inproj(xp, g_mix, w_in, l, rope_p, tm=tm)
        mix = _mixer(q, k, v, u, attn_sinks[l], w_pool, scale, l)
        xp, hn, *routing = _outproj(mix, xp, w_out, g_ffn, l, router, tm=tm)
        nk_p.append(kvt[:, :KV_WIDTH].reshape(BATCH, WINDOW, *heads))
        nv_p.append(kvt[:, KV_WIDTH:].reshape(BATCH, WINDOW, *heads))
        nu_p.append(ut.reshape(BATCH, BF16_ROWS, POOL_WIDTH)[:, BF16_ROWS - POOL_STATE:])
        if moe:
            route, counts = routing
            dest, pad_rows, steps = _route_plan(route, counts)
            y = _experts(_dispatch(hn, dest, pad_rows), w_gate_exp, w_up_exp, w_down_exp, i, steps)
            xp = _combine(xp, route, y, dest, g_final if last else None)
        else:
            for s in range(DENSE_SLABS):
                xp = _ffn_slab(hn, xp, w_gate_dense, w_up_dense, w_down_dense, i, s, slab_w, tm=tm,
                               g_final=g_final if last and s == DENSE_SLABS - 1 else None)

        z = _sample_inproj(xs, g_mix, w_in, l, rope_s)
        a_s, ck, cv, pool_state, ps = _sample_mixer(
            z[:, :ATTN_WIDTH], z[:, ATTN_WIDTH:ATTN_WIDTH + KV_WIDTH],
            z[:, ATTN_WIDTH + KV_WIDTH:ATTN_WIDTH + 2 * KV_WIDTH], z[:, ATTN_WIDTH + 2 * KV_WIDTH:],
            ck, cv, pool_state, attn_sinks[l], w_pool, scale, l)
        mix_s = jnp.concatenate([a_s.reshape(DEC_BATCH, ATTN_WIDTH), ps], axis=1)
        xs = _sample_outproj(mix_s, xs, w_out, l)
        if moe:
            xs = _sample_ffn(xs, g_ffn, w_gate_exp, w_up_exp, w_down_exp, i, router=router,
                             g_final=g_final if last else None)
        else:
            xs = _sample_ffn(xs, g_ffn, w_gate_dense, w_up_dense, w_down_dense, i,
                             g_final=g_final if last else None)

    y_prompt = xp.reshape(BATCH, SEQ, D_MODEL)
    y_sample = xs.reshape(DEC_BATCH, 1, D_MODEL)
    return (y_prompt, y_sample, jnp.stack(nk_p), jnp.stack(nv_p), jnp.stack(nu_p),
            ck.reshape(DEPTH, DEC_BATCH, WINDOW, *heads), cv.reshape(DEPTH, DEC_BATCH, WINDOW, *heads), pool_state)
```

```python
import functools

import jax
import jax.numpy as jnp
from jax import lax
from jax.experimental import pallas as pl
from jax.experimental.pallas import tpu as pltpu

F32 = jnp.float32
BF16 = jnp.bfloat16

D_MODEL = 2048
BATCH = 2
SEQ = 4096
DEPTH = 4
DEC_BATCH = 32
PAST_LEN = 16384
N_HEADS = 16
N_KV_HEADS = 4
HEAD_DIM = 64
GROUP = N_HEADS // N_KV_HEADS
ATTN_WIDTH = N_HEADS * HEAD_DIM
KV_WIDTH = N_KV_HEADS * HEAD_DIM
WINDOW = 128
ROPE_DIM = HEAD_DIM // 4
ROPE_THETA = 500000.0
POOL_WINDOWS = (2, 4, 8, 16)
POOL_WIDTH = D_MODEL - ATTN_WIDTH
POOL_GROUP_DIM = POOL_WIDTH // len(POOL_WINDOWS)
POOL_STATE = max(POOL_WINDOWS) - 1
IN_WIDTH = ATTN_WIDTH + 2 * KV_WIDTH + POOL_WIDTH
D_FF = 5632
N_EXPERTS = 8
D_FF_EXPERT = D_MODEL // 2
EPS = 1e-5
NEG_INF = -1e30

LANES = 128
BF16_ROWS = 16
VMEM_LIMIT = 56 * 1024 * 1024
DENSE_SLABS = 4
ROUTER_PAD = LANES


def _params(n_axes=1, vmem=VMEM_LIMIT):
    return pltpu.CompilerParams(dimension_semantics=("arbitrary",) * n_axes, vmem_limit_bytes=vmem)


def _rms(x, g):
    ms = jnp.mean(x * x, axis=-1, keepdims=True)
    return (x * lax.rsqrt(ms + EPS)) * g


def _rope_mix(zc, ra, rb, rc):
    return zc * ra + pltpu.roll(zc, ROPE_DIM // 2, 1) * rb + pltpu.roll(zc, LANES - ROPE_DIM // 2, 1) * rc


def _split_bf16(a):
    hi = a.astype(BF16)
    return hi, (a - hi.astype(F32)).astype(BF16)


def _dot3(a, b, dims=(((1,), (0,)), ((), ()))):
    m = a.shape[0]
    ah, al = _split_bf16(a)
    bh, bl = _split_bf16(b)
    t = lax.dot_general(jnp.concatenate([ah, al], axis=0), bh, dims, preferred_element_type=F32)
    return t[:m] + t[m:] + lax.dot_general(ah, bl, dims, preferred_element_type=F32)


def _stash_piece(step, w_ref, wbf, n_pieces):
    rows = w_ref.shape[0]

    @pl.when(step < n_pieces)
    def _():
        r0 = pl.multiple_of(step * rows, rows)
        wbf[pl.ds(r0, rows), :] = w_ref[...].astype(BF16)


def _piece_rows(total_rows, n_pieces):
    rows, rem = divmod(total_rows, n_pieces)
    assert rem == 0 and rows % BF16_ROWS == 0, (total_rows, n_pieces)
    return rows


def _inproj_kernel(x_ref, g_ref, w_ref, ra_ref, rb_ref, rc_ref, q_ref, k_ref, v_ref, u_ref, kvt_ref, ut_ref,
                   wbf, *, n_pieces):
    step = pl.program_id(0)
    _stash_piece(step, w_ref, wbf, n_pieces)

    @pl.when(step >= n_pieces)
    def _():
        tm = x_ref.shape[0]
        h = _rms(x_ref[...], g_ref[...])
        z = jnp.dot(h.astype(BF16), wbf[...], preferred_element_type=F32)
        ra, rb, rc = ra_ref[...], rb_ref[...], rc_ref[...]
        n_rot = (ATTN_WIDTH + KV_WIDTH) // LANES
        rot = []
        for c in range(n_rot):
            r = _rope_mix(z[:, c * LANES:(c + 1) * LANES], ra, rb, rc)
            if c >= ATTN_WIDTH // LANES:
                r = r * (HEAD_DIM ** 0.5)
            rot.append(r)
        nq = ATTN_WIDTH // LANES
        for c in range(nq):
            q_ref[:, c * LANES:(c + 1) * LANES] = rot[c].astype(q_ref.dtype)
        for c in range(nq, n_rot):
            k_ref[:, (c - nq) * LANES:(c - nq + 1) * LANES] = rot[c].astype(k_ref.dtype)
        v = z[:, ATTN_WIDTH + KV_WIDTH:ATTN_WIDTH + 2 * KV_WIDTH]
        u = z[:, ATTN_WIDTH + 2 * KV_WIDTH:]
        v_ref[...] = v.astype(v_ref.dtype)
        u_ref[...] = u.astype(u_ref.dtype)
        for c in range(nq, n_rot):
            kvt_ref[:, (c - nq) * LANES:(c - nq + 1) * LANES] = rot[c][tm - WINDOW:, :]
        kvt_ref[:, KV_WIDTH:] = v[tm - WINDOW:, :]
        ut_ref[...] = u[tm - BF16_ROWS:, :]


def _inproj(x, g, w_in, layer, rope, *, tm):
    m = x.shape[0]
    nm = m // tm
    n_pieces = 8
    rows = _piece_rows(D_MODEL, n_pieces)
    tiles_per_seq = SEQ // tm
    tile = lambda i: jnp.maximum(i - n_pieces, 0)
    row_spec = lambda width: pl.BlockSpec((tm, width), lambda i: (tile(i), 0))
    rope_spec = pl.BlockSpec((tm, LANES), lambda i: (tile(i) % tiles_per_seq, 0))
    out_shape = [jax.ShapeDtypeStruct((m, ATTN_WIDTH), BF16),
                 jax.ShapeDtypeStruct((m, KV_WIDTH), BF16),
                 jax.ShapeDtypeStruct((m, KV_WIDTH), BF16),
                 jax.ShapeDtypeStruct((m, POOL_WIDTH), BF16),
                 jax.ShapeDtypeStruct((BATCH * WINDOW, 2 * KV_WIDTH), F32),
                 jax.ShapeDtypeStruct((BATCH * BF16_ROWS, POOL_WIDTH), F32)]
    out_specs = [row_spec(ATTN_WIDTH), row_spec(KV_WIDTH), row_spec(KV_WIDTH), row_spec(POOL_WIDTH),
                 pl.BlockSpec((WINDOW, 2 * KV_WIDTH), lambda i: (tile(i) // tiles_per_seq, 0)),
                 pl.BlockSpec((BF16_ROWS, POOL_WIDTH), lambda i: (tile(i) // tiles_per_seq, 0))]
    return pl.pallas_call(
        functools.partial(_inproj_kernel, n_pieces=n_pieces),
        grid=(n_pieces + nm,),
        in_specs=[row_spec(D_MODEL),
                  pl.BlockSpec((1, D_MODEL), lambda i: (0, 0)),
                  pl.BlockSpec((None, rows, IN_WIDTH), lambda i: (layer, jnp.minimum(i, n_pieces - 1), 0)),
                  rope_spec, rope_spec, rope_spec],
        out_specs=out_specs,
        out_shape=out_shape,
        scratch_shapes=[pltpu.VMEM((D_MODEL, IN_WIDTH), BF16)],
        compiler_params=_params(),
        name="inproj",
    )(x, g, w_in, *rope)


def _sink_softmax(s, sink, axis):
    mx = jnp.maximum(jnp.max(s, axis=axis, keepdims=True), sink)
    p = jnp.exp(s - mx)
    den = jnp.sum(p, axis=axis, keepdims=True) + jnp.exp(sink - mx)
    return p * (1.0 / den)


def _pool_group(d, wpb_g, scale_g):
    return jnp.dot(d.astype(BF16), wpb_g, preferred_element_type=F32) * scale_g


MIXER_QB = 2


def _mixer_kernel(sink_ref, q_ref, kc_ref, kp_ref, vc_ref, vp_ref, uc_ref, up_ref, wp_ref, sc_ref,
                  o_ref, wpb, band):
    b = pl.program_id(0)
    m = pl.program_id(1)
    blk = WINDOW

    @pl.when((b == 0) & (m == 0))
    def _():
        wpb[...] = wp_ref[...].astype(BF16)
        tok = lax.broadcasted_iota(jnp.int32, (blk, 2 * blk), 0) + blk
        src = lax.broadcasted_iota(jnp.int32, (blk, 2 * blk), 1)
        for g, w in enumerate(POOL_WINDOWS):
            band[g] = jnp.where((src <= tok) & (src > tok - w), 1.0, 0.0).astype(BF16)

    half = LANES // 2
    kall = jnp.concatenate([kp_ref[...], kc_ref[...]], axis=0).astype(F32)
    vall = jnp.concatenate([vp_ref[...], vc_ref[...]], axis=0).astype(F32)
    uall = jnp.concatenate([jnp.where(m > 0, up_ref[...], jnp.zeros_like(up_ref)), uc_ref[...]], axis=0)
    lane = lax.broadcasted_iota(jnp.int32, (2 * blk, LANES), 1)
    low = lane < half
    key = lax.broadcasted_iota(jnp.int32, (4 * blk, blk), 0) & (2 * blk - 1)
    qry = lax.broadcasted_iota(jnp.int32, (4 * blk, blk), 1)
    in_window = (key > qry) & (key <= qry + blk)

    for qb in range(MIXER_QB):
        r0 = qb * blk
        kk, vv = kall[r0:r0 + 2 * blk], vall[r0:r0 + 2 * blk]
        valid = in_window & ((m > 0) | (key >= blk)) if qb == 0 else in_window
        for h in range(N_KV_HEADS):
            g0 = (h // 2) * LANES
            kg, vg = kk[:, g0:g0 + LANES], vv[:, g0:g0 + LANES]
            if h % 2 == 0:
                klo, vlo = jnp.where(low, kg, 0.0), jnp.where(low, vg, 0.0)
                khi, vhi = pltpu.roll(klo, half, 1), pltpu.roll(vlo, half, 1)
            else:
                khi, vhi = jnp.where(low, 0.0, kg), jnp.where(low, 0.0, vg)
                klo, vlo = pltpu.roll(khi, half, 1), pltpu.roll(vhi, half, 1)
            kcat = jnp.concatenate([klo, khi], axis=0).astype(BF16)
            vcat = jnp.concatenate([vlo, vhi], axis=0).astype(BF16)
            for pair in range(h * GROUP // 2, (h + 1) * GROUP // 2):
                qp = q_ref[r0:r0 + blk, pair * LANES:(pair + 1) * LANES]
                st = lax.dot_general(kcat, qp, (((1,), (1,)), ((), ())), preferred_element_type=F32)
                st = jnp.where(valid, st, NEG_INF)
                pt = [_sink_softmax(st[t * 2 * blk:(t + 1) * 2 * blk], sink_ref[2 * pair + t], 0).astype(BF16)
                      for t in range(2)]
                o = lax.dot_general(jnp.concatenate(pt, axis=0), vcat, (((0,), (0,)), ((), ())),
                                    preferred_element_type=F32)
                o_ref[r0:r0 + blk, pair * LANES:(pair + 1) * LANES] = o.astype(o_ref.dtype)

        ext = uall[r0:r0 + 2 * blk]
        pos = ((m * MIXER_QB + qb) * blk + lax.broadcasted_iota(jnp.int32, (blk, 1), 0)).astype(F32)
        for g, w in enumerate(POOL_WINDOWS):
            c0, c1 = g * POOL_GROUP_DIM, (g + 1) * POOL_GROUP_DIM
            wsum = jnp.dot(band[g], ext[:, c0:c1], preferred_element_type=F32)
            d = wsum * (1.0 / jnp.minimum(float(w), pos + 1.0)) - uc_ref[r0:r0 + blk, c0:c1].astype(F32)
            y = _pool_group(d, wpb[g], sc_ref[:, c0:c1])
            o_ref[r0:r0 + blk, ATTN_WIDTH + c0:ATTN_WIDTH + c1] = y.astype(o_ref.dtype)


def _mixer(q, k, v, u, sinks, w_pool, scale, layer):
    assert u.dtype == BF16
    qb = MIXER_QB
    nb = SEQ // WINDOW
    ns = nb // qb
    cur = lambda b, m: (b * ns + m, 0)
    prev = lambda b, m: (b * nb + jnp.maximum(m * qb - 1, 0), 0)
    n_grp = len(POOL_WINDOWS)
    return pl.pallas_call(
        _mixer_kernel,
        grid=(BATCH, ns),
        in_specs=[pl.BlockSpec(memory_space=pltpu.SMEM),
                  pl.BlockSpec((qb * WINDOW, ATTN_WIDTH), cur),
                  pl.BlockSpec((qb * WINDOW, KV_WIDTH), cur), pl.BlockSpec((WINDOW, KV_WIDTH), prev),
                  pl.BlockSpec((qb * WINDOW, KV_WIDTH), cur), pl.BlockSpec((WINDOW, KV_WIDTH), prev),
                  pl.BlockSpec((qb * WINDOW, POOL_WIDTH), cur), pl.BlockSpec((WINDOW, POOL_WIDTH), prev),
                  pl.BlockSpec((None, n_grp, POOL_GROUP_DIM, POOL_GROUP_DIM), lambda b, m: (layer, 0, 0, 0)),
                  pl.BlockSpec((1, POOL_WIDTH), lambda b, m: (0, 0))],
        out_specs=pl.BlockSpec((qb * WINDOW, D_MODEL), cur),
        out_shape=jax.ShapeDtypeStruct((BATCH * SEQ, D_MODEL), BF16),
        scratch_shapes=[pltpu.VMEM((n_grp, POOL_GROUP_DIM, POOL_GROUP_DIM), BF16),
                        pltpu.VMEM((n_grp, WINDOW, 2 * WINDOW), BF16)],
        compiler_params=_params(2),
        name="mixer",
    )(sinks, q, k, k, v, v, u, u, w_pool, scale)


SAMPLE_SB = 4


def _sample_mixer_kernel(sink_ref, q_ref, kn_ref, vn_ref, un_ref, ck_ref, cv_ref, st_ref, wp_ref, sc_ref,
                         a_ref, nk_ref, nv_ref, nu_ref, po_ref, dscr):
    step = pl.program_id(0)
    sink = sink_ref[...]
    own = (lax.broadcasted_iota(jnp.int32, (N_HEADS, KV_WIDTH), 0) // GROUP
           == lax.broadcasted_iota(jnp.int32, (N_HEADS, KV_WIDTH), 1) // HEAD_DIM)
    for j in range(SAMPLE_SB):
        nk_ref[j, 0:WINDOW - 1, :] = ck_ref[j, 1:WINDOW, :]
        nk_ref[j, WINDOW - 1:WINDOW, :] = kn_ref[j]
        nv_ref[j, 0:WINDOW - 1, :] = cv_ref[j, 1:WINDOW, :]
        nv_ref[j, WINDOW - 1:WINDOW, :] = vn_ref[j]

        q = q_ref[j]
        qe = jnp.where(own, jnp.concatenate([q] * N_KV_HEADS, axis=1), 0.0)
        s = _dot3(qe, nk_ref[j], (((1,), (1,)), ((), ())))
        r = _dot3(_sink_softmax(s, sink, 1), nv_ref[j])
        r = jnp.where(own, r, 0.0)
        o = r[:, 0:HEAD_DIM]
        for h in range(1, N_KV_HEADS):
            o = o + r[:, h * HEAD_DIM:(h + 1) * HEAD_DIM]
        a_ref[j] = o

        st = st_ref[j]
        un = un_ref[j]
        nu_ref[j, 0:POOL_STATE - 1, :] = st[1:POOL_STATE, :]
        nu_ref[j, POOL_STATE - 1:POOL_STATE, :] = un
        for g, w in enumerate(POOL_WINDOWS):
            c0, c1 = g * POOL_GROUP_DIM, (g + 1) * POOL_GROUP_DIM
            tok = un[:, c0:c1]
            wsum = tok + jnp.sum(st[POOL_STATE - (w - 1):POOL_STATE, c0:c1], axis=0, keepdims=True)
            dscr[pl.ds(step * SAMPLE_SB + j, 1), c0:c1] = wsum / float(min(w, PAST_LEN + 1)) - tok

    @pl.when(step == pl.num_programs(0) - 1)
    def _():
        for g in range(len(POOL_WINDOWS)):
            c0, c1 = g * POOL_GROUP_DIM, (g + 1) * POOL_GROUP_DIM
            po_ref[:, c0:c1] = _dot3(dscr[:, c0:c1], wp_ref[g]) * sc_ref[:, c0:c1]


def _sample_mixer(q, k, v, u, cache_k, cache_v, state, sinks, w_pool, scale, layer):
    nb, sb = DEC_BATCH, SAMPLE_SB
    per_b = lambda *shape: pl.BlockSpec((sb,) + shape, lambda b: (b,) + (0,) * len(shape))
    per_lb = lambda *shape: pl.BlockSpec((None, sb) + shape, lambda b: (layer, b) + (0,) * len(shape))
    return pl.pallas_call(
        _sample_mixer_kernel,
        grid=(nb // sb,),
        in_specs=[pl.BlockSpec((N_HEADS, 1), lambda b: (0, 0)),
                  per_b(N_HEADS, HEAD_DIM), per_b(1, KV_WIDTH), per_b(1, KV_WIDTH), per_b(1, POOL_WIDTH),
                  per_lb(WINDOW, KV_WIDTH), per_lb(WINDOW, KV_WIDTH), per_lb(POOL_STATE, POOL_WIDTH),
                  pl.BlockSpec((None, len(POOL_WINDOWS), POOL_GROUP_DIM, POOL_GROUP_DIM),
                               lambda b: (layer, 0, 0, 0)),
                  pl.BlockSpec((1, POOL_WIDTH), lambda b: (0, 0))],
        out_specs=[per_b(N_HEADS, HEAD_DIM), per_lb(WINDOW, KV_WIDTH), per_lb(WINDOW, KV_WIDTH),
                   per_lb(POOL_STATE, POOL_WIDTH), pl.BlockSpec((nb, POOL_WIDTH), lambda b: (0, 0))],
        out_shape=[jax.ShapeDtypeStruct((nb, N_HEADS, HEAD_DIM), F32),
                   jax.ShapeDtypeStruct(cache_k.shape, F32),
                   jax.ShapeDtypeStruct(cache_v.shape, F32),
                   jax.ShapeDtypeStruct(state.shape, F32),
                   jax.ShapeDtypeStruct((nb, POOL_WIDTH), F32)],
        scratch_shapes=[pltpu.VMEM((nb, POOL_WIDTH), F32)],
        input_output_aliases={5: 1, 6: 2, 7: 3},
        compiler_params=_params(),
        name="sample_mixer",
    )(sinks.reshape(N_HEADS, 1), q.reshape(nb, N_HEADS, HEAD_DIM), k.reshape(nb, 1, KV_WIDTH),
      v.reshape(nb, 1, KV_WIDTH), u.reshape(nb, 1, POOL_WIDTH), cache_k, cache_v, state, w_pool, scale)


SAMPLE_TN = 512


def _sample_inproj_kernel(x_ref, g_ref, w_ref, ra_ref, rb_ref, rc_ref, z_ref):
    j = pl.program_id(0)
    z = _dot3(_rms(x_ref[...], g_ref[...]), w_ref[...])
    ra, rb, rc = ra_ref[...], rb_ref[...], rc_ref[...]
    per_tile = SAMPLE_TN // LANES
    for c in range(per_tile):
        zc = z[:, c * LANES:(c + 1) * LANES]
        grp = j * per_tile + c
        r = _rope_mix(zc, ra, rb, rc)
        r = jnp.where(grp >= ATTN_WIDTH // LANES, r * (HEAD_DIM ** 0.5), r)
        z_ref[:, c * LANES:(c + 1) * LANES] = jnp.where(grp < (ATTN_WIDTH + KV_WIDTH) // LANES, r, zc)


def _sample_inproj(x, g, w_in, layer, rope):
    m = x.shape[0]
    const = lambda r, c: pl.BlockSpec((r, c), lambda j: (0, 0))
    return pl.pallas_call(
        _sample_inproj_kernel,
        grid=(IN_WIDTH // SAMPLE_TN,),
        in_specs=[const(m, D_MODEL), const(1, D_MODEL),
                  pl.BlockSpec((None, D_MODEL, SAMPLE_TN), lambda j: (layer, 0, j)),
                  const(m, LANES), const(m, LANES), const(m, LANES)],
        out_specs=pl.BlockSpec((m, SAMPLE_TN), lambda j: (0, j)),
        out_shape=jax.ShapeDtypeStruct((m, IN_WIDTH), F32),
        compiler_params=_params(),
        name="sample_inproj",
    )(x, g, w_in, *rope)


def _sample_outproj_kernel(mix_ref, x_ref, w_ref, o_ref):
    o_ref[...] = x_ref[...] + _dot3(mix_ref[...], w_ref[...])


def _sample_outproj(mix, x, w_out, layer):
    m = x.shape[0]
    col = pl.BlockSpec((m, SAMPLE_TN), lambda j: (0, j))
    return pl.pallas_call(
        _sample_outproj_kernel,
        grid=(D_MODEL // SAMPLE_TN,),
        in_specs=[pl.BlockSpec((m, D_MODEL), lambda j: (0, 0)), col,
                  pl.BlockSpec((None, D_MODEL, SAMPLE_TN), lambda j: (layer, 0, j))],
        out_specs=col,
        out_shape=jax.ShapeDtypeStruct((m, D_MODEL), F32),
        compiler_params=_params(),
        name="sample_outproj",
    )(mix, x, w_out)


def _sample_ffn_kernel(x_ref, g_ref, *rest, moe, final_norm):
    rest = list(rest)
    wr_ref, br_ref = (rest.pop(0), rest.pop(0)) if moe else (None, None)
    gf_ref = rest.pop(0) if final_norm else None
    wg_ref, wu_ref, wd_ref, o_ref, acc, hn_scr, cmb = rest
    e, f = pl.program_id(0), pl.program_id(1)

    @pl.when((e == 0) & (f == 0))
    def _():
        hn = _rms(x_ref[...], g_ref[...])
        hn_scr[...] = hn
        acc[...] = jnp.zeros_like(acc)
        if moe:
            cmb[...] = _top2_gates(_dot3(hn, wr_ref[...]) + br_ref[...])

    hn = hn_scr[...]
    gate = _dot3(hn, wg_ref[...])
    up = _dot3(hn, wu_ref[...])
    h = (gate * (1.0 / (1.0 + jnp.exp(-gate)))) * up
    if moe:
        lane = lax.broadcasted_iota(jnp.int32, cmb.shape, 1)
        h = h * jnp.sum(jnp.where(lane == e, cmb[...], 0.0), axis=1, keepdims=True)
    acc[...] += _dot3(h, wd_ref[...])

    @pl.when((e == pl.num_programs(0) - 1) & (f == pl.num_programs(1) - 1))
    def _():
        out = x_ref[...] + acc[...]
        if final_norm:
            out = _rms(out, gf_ref[...])
        o_ref[...] = out


def _sample_ffn(x, g, wg, wu, wd, idx, *, router=None, g_final=None):
    m = x.shape[0]
    moe = router is not None
    n_e = N_EXPERTS if moe else 1
    width = wg.shape[-1]
    const = lambda r, c: pl.BlockSpec((r, c), lambda e, f: (0, 0))
    in_specs = [const(m, D_MODEL), const(1, D_MODEL)]
    args = [x, g]
    if moe:
        in_specs += [const(D_MODEL, ROUTER_PAD), const(1, ROUTER_PAD)]
        args += list(router)
        up_spec = pl.BlockSpec((None, None, D_MODEL, SAMPLE_TN), lambda e, f: (idx, e, 0, f))
        dn_spec = pl.BlockSpec((None, None, SAMPLE_TN, D_MODEL), lambda e, f: (idx, e, f, 0))
    else:
        up_spec = pl.BlockSpec((None, D_MODEL, SAMPLE_TN), lambda e, f: (idx, 0, f))
        dn_spec = pl.BlockSpec((None, SAMPLE_TN, D_MODEL), lambda e, f: (idx, f, 0))
    if g_final is not None:
        in_specs.append(const(1, D_MODEL))
        args.append(g_final)
    return pl.pallas_call(
        functools.partial(_sample_ffn_kernel, moe=moe, final_norm=g_final is not None),
        grid=(n_e, width // SAMPLE_TN),
        in_specs=in_specs + [up_spec, up_spec, dn_spec],
        out_specs=const(m, D_MODEL),
        out_shape=jax.ShapeDtypeStruct((m, D_MODEL), F32),
        scratch_shapes=[pltpu.VMEM((m, D_MODEL), F32), pltpu.VMEM((m, D_MODEL), F32),
                        pltpu.VMEM((m, ROUTER_PAD), F32)],
        compiler_params=_params(2),
        name="sample_ffn",
    )(*args, wg, wu, wd)


def _top2(logits):
    lane = lax.broadcasted_iota(jnp.int32, logits.shape, 1)
    lg = jnp.where(lane < N_EXPERTS, logits, -jnp.inf)
    m1 = jnp.max(lg, axis=1, keepdims=True)
    i1 = jnp.min(jnp.where(lg == m1, lane, ROUTER_PAD), axis=1, keepdims=True)
    lg2 = jnp.where(lane == i1, -jnp.inf, lg)
    m2 = jnp.max(lg2, axis=1, keepdims=True)
    i2 = jnp.min(jnp.where(lg2 == m2, lane, ROUTER_PAD), axis=1, keepdims=True)
    e2 = jnp.exp(m2 - m1)
    den = 1.0 + e2
    return i1, i2, 1.0 / den, e2 / den


def _top2_gates(logits):
    i1, i2, g1, g2 = _top2(logits)
    lane = lax.broadcasted_iota(jnp.int32, logits.shape, 1)
    return jnp.where(lane == i1, g1, 0.0) + jnp.where(lane == i2, g2, 0.0)


ROUTE_I1, ROUTE_I2, ROUTE_G1, ROUTE_G2, ROUTE_P1, ROUTE_P2 = range(6)


def _outproj_kernel(mix_ref, x_ref, w_ref, g_ref, *rest, n_pieces, with_router):
    if with_router:
        wr_ref, br_ref, xo_ref, hn_ref, route_ref, cnt_ref, wbf, tri, carry = rest
    else:
        xo_ref, hn_ref, wbf = rest
    step = pl.program_id(0)
    _stash_piece(step, w_ref, wbf, n_pieces)

    if with_router:
        @pl.when(step == 0)
        def _():
            row = lax.broadcasted_iota(jnp.int32, tri.shape, 0)
            col = lax.broadcasted_iota(jnp.int32, tri.shape, 1)
            tri[...] = jnp.where(col < row, 1.0, 0.0).astype(BF16)
            carry[...] = jnp.zeros_like(carry)

    @pl.when(step >= n_pieces)
    def _():
        y = jnp.dot(mix_ref[...], wbf[...], preferred_element_type=F32)
        xo = x_ref[...] + y
        xo_ref[...] = xo
        hn = _rms(xo, g_ref[...])
        if not with_router:
            hn_ref[...] = hn.astype(BF16)
            return
        hn_ref[...] = hn
        logits = jnp.dot(hn.astype(BF16), wr_ref[...].astype(BF16), preferred_element_type=F32) + br_ref[...]
        i1, i2, g1, g2 = _top2(logits)
        lane = lax.broadcasted_iota(jnp.int32, logits.shape, 1)
        picked = jnp.where((lane == i1) | (lane == i2), 1.0, 0.0)
        ahead = jnp.dot(tri[...], picked.astype(BF16), preferred_element_type=F32) + carry[...]
        p1 = jnp.sum(jnp.where(lane == i1, ahead, 0.0), axis=1, keepdims=True)
        p2 = jnp.sum(jnp.where(lane == i2, ahead, 0.0), axis=1, keepdims=True)
        carry[...] += jnp.sum(picked, axis=0, keepdims=True)
        cnt_ref[...] = carry[...]
        rec = jnp.zeros(logits.shape, F32)
        for k, val in ((ROUTE_I1, i1.astype(F32)), (ROUTE_I2, i2.astype(F32)), (ROUTE_G1, g1), (ROUTE_G2, g2),
                       (ROUTE_P1, p1), (ROUTE_P2, p2)):
            rec = jnp.where(lane == k, val, rec)
        route_ref[...] = rec


def _outproj(mix, x, w_out, g, layer, router, *, tm):
    m = x.shape[0]
    nm = m // tm
    n_pieces = 8
    rows = _piece_rows(D_MODEL, n_pieces)
    tile = lambda i: jnp.maximum(i - n_pieces, 0)
    row_spec = lambda width: pl.BlockSpec((tm, width), lambda i: (tile(i), 0))
    const = lambda r, c: pl.BlockSpec((r, c), lambda i: (0, 0))
    in_specs = [row_spec(D_MODEL), row_spec(D_MODEL),
                pl.BlockSpec((None, rows, D_MODEL), lambda i: (layer, jnp.minimum(i, n_pieces - 1), 0)),
                const(1, D_MODEL)]
    args = [mix, x, w_out, g]
    scratch = [pltpu.VMEM((D_MODEL, D_MODEL), BF16)]
    if router is None:
        out_shape = [jax.ShapeDtypeStruct((m, D_MODEL), F32), jax.ShapeDtypeStruct((m, D_MODEL), BF16)]
        out_specs = [row_spec(D_MODEL), row_spec(D_MODEL)]
    else:
        in_specs += [const(D_MODEL, ROUTER_PAD), const(1, ROUTER_PAD)]
        args += list(router)
        out_shape = [jax.ShapeDtypeStruct((m, D_MODEL), F32), jax.ShapeDtypeStruct((m, D_MODEL), F32),
                     jax.ShapeDtypeStruct((m, ROUTER_PAD), F32), jax.ShapeDtypeStruct((1, ROUTER_PAD), F32)]
        out_specs = [row_spec(D_MODEL), row_spec(D_MODEL), row_spec(ROUTER_PAD), const(1, ROUTER_PAD)]
        scratch += [pltpu.VMEM((tm, tm), BF16), pltpu.VMEM((1, ROUTER_PAD), F32)]
    return pl.pallas_call(
        functools.partial(_outproj_kernel, n_pieces=n_pieces, with_router=router is not None),
        grid=(n_pieces + nm,),
        in_specs=in_specs,
        out_specs=out_specs,
        out_shape=out_shape,
        scratch_shapes=scratch,
        compiler_params=_params(),
        name="outproj",
    )(*args)


EXPERT_TM = 512
EXPERT_PIECES = 4
ROUTE_TM = 256
SORTED_ROWS = BATCH * SEQ * 2 + N_EXPERTS * EXPERT_TM
EXPERT_STEPS = N_EXPERTS * EXPERT_PIECES + SORTED_ROWS // EXPERT_TM
EXPERT_VMEM_LIMIT = 60 * 1024 * 1024


def _take(table, idx):
    hot = idx[..., None] == jnp.arange(table.shape[0], dtype=idx.dtype)
    return jnp.sum(jnp.where(hot, table, 0), axis=-1)


def _route_plan(route, counts):
    i32 = jnp.int32
    t, p = EXPERT_TM, EXPERT_PIECES
    cnt = counts[0, :N_EXPERTS].astype(i32)
    ntile = (cnt + t - 1) // t
    tile0 = jnp.cumsum(ntile) - ntile
    experts = route[:, ROUTE_I1:ROUTE_I2 + 1].astype(i32)
    dest = _take(tile0, experts) * t + route[:, ROUTE_P1:ROUTE_P2 + 1].astype(i32)
    n_all = SORTED_ROWS // t
    last = N_EXPERTS - 1
    ids = jnp.arange(N_EXPERTS, dtype=i32)
    blen = jnp.where(ids < last, jnp.maximum(ntile, p), ntile)
    bstart = p + jnp.cumsum(blen) - blen
    total = p + jnp.sum(blen)
    n_tiles = jnp.sum(ntile)
    s = jnp.arange(EXPERT_STEPS, dtype=i32)
    pro = s < p
    e = jnp.clip(jnp.sum((s[:, None] >= bstart[None, :]).astype(i32), axis=1) - 1, 0, last)
    k = s - _take(bstart, e)
    live = jnp.logical_not(pro) & (s < total)
    comp = live & (k < _take(ntile, e))
    stage = pro | (live & (k < p) & (e < last))
    w_expert = jnp.where(pro, 0, jnp.minimum(e + 1, last))
    w_piece = jnp.where(pro, s, jnp.where(stage, k, p - 1))
    first = _take(tile0, e)
    spare = n_tiles + s - total
    tile = jnp.where(comp, first + k,
                     jnp.where(pro, 0, jnp.where(live, first + _take(ntile, e), spare)))
    tile = jnp.minimum(tile, n_all - 1)
    act = jnp.where(comp, 1, jnp.where((s >= total) & (spare < n_all), 2, 0))
    nvalid = jnp.clip(_take(cnt, e) - (tile - first) * t, 0, t)
    steps = (w_expert, w_piece, stage.astype(i32), tile, act, nvalid, e % 2)
    dest = dest.reshape(dest.shape[0] // ROUTE_TM, 1, 2 * ROUTE_TM)
    partial = jnp.where(ntile > 0, tile0 + ntile - 1, -1)
    unused = n_tiles + jnp.arange(N_EXPERTS, dtype=i32)
    pad_rows = jnp.concatenate([partial, jnp.where(unused < n_all, unused, -1)]) * t
    return dest, pad_rows, steps


def _start_rows(n, copies_of):
    def start(t, c):
        for cp in copies_of(t):
            cp.start()
        return c

    lax.fori_loop(0, n, start, 0, unroll=8)


def _wait_rows(n, copies_of):
    def wait(t, c):
        for cp in copies_of(t):
            cp.wait()
        return c

    lax.fori_loop(0, n, wait, 0, unroll=8)


def _dispatch_kernel(dest_ref, pad_ref, hn_ref, xg_hbm, zeros, sem):
    step = pl.program_id(0)

    @pl.when(step == 0)
    def _():
        zeros[...] = jnp.zeros_like(zeros)

        def fill(j):
            row = pl.multiple_of(jnp.maximum(pad_ref[j], 0), EXPERT_TM)
            return pltpu.make_async_copy(zeros, xg_hbm.at[pl.ds(row, EXPERT_TM)], sem)

        for j in range(pad_ref.shape[0]):
            pl.when(pad_ref[j] >= 0)(lambda j=j: fill(j).start())
        for j in range(pad_ref.shape[0]):
            pl.when(pad_ref[j] >= 0)(lambda j=j: fill(j).wait())

    def copies_of(t):
        src = hn_ref.at[pl.ds(t, 1)]
        return [pltpu.make_async_copy(src, xg_hbm.at[pl.ds(dest_ref[0, 2 * t + k], 1)], sem) for k in range(2)]

    _start_rows(ROUTE_TM, copies_of)
    _wait_rows(ROUTE_TM, copies_of)


def _dispatch(hn, dest, pad_rows):
    return pl.pallas_call(
        _dispatch_kernel,
        grid=(hn.shape[0] // ROUTE_TM,),
        in_specs=[pl.BlockSpec((None, 1, 2 * ROUTE_TM), lambda s: (s, 0, 0), memory_space=pltpu.SMEM),
                  pl.BlockSpec(memory_space=pltpu.SMEM),
                  pl.BlockSpec((ROUTE_TM, D_MODEL), lambda s: (s, 0))],
        out_specs=pl.BlockSpec(memory_space=pl.ANY),
        out_shape=jax.ShapeDtypeStruct((SORTED_ROWS, D_MODEL), F32),
        scratch_shapes=[pltpu.VMEM((EXPERT_TM, D_MODEL), F32), pltpu.SemaphoreType.DMA],
        compiler_params=_params(),
        name="dispatch",
    )(dest, pad_rows, hn)


def _expert_kernel(st_we, st_wp, st_stage, st_t, st_act, st_nv, st_slot, x_ref, wg_ref, wu_ref, wd_ref, y_ref,
                   wgb, wub, wdb):
    s = pl.program_id(0)

    @pl.when(st_stage[s] == 1)
    def _():
        slot = st_we[s] & 1
        for w_ref, wbf in ((wg_ref, wgb), (wu_ref, wub), (wd_ref, wdb)):
            rows = w_ref.shape[0]
            r0 = pl.multiple_of(st_wp[s] * rows, rows)
            wbf[slot, pl.ds(r0, rows), :] = w_ref[...].astype(BF16)

    act = st_act[s]

    @pl.when(act == 1)
    def _():
        slot = st_slot[s]
        row = lax.broadcasted_iota(jnp.int32, (x_ref.shape[0], 1), 0)
        x = jnp.where(row < st_nv[s], x_ref[...], 0.0).astype(BF16)
        gate = jnp.dot(x, wgb[slot], preferred_element_type=F32)
        up = jnp.dot(x, wub[slot], preferred_element_type=F32)
        h = (gate * (1.0 / (1.0 + jnp.exp(-gate)))) * up
        y_ref[...] = jnp.dot(h.astype(BF16), wdb[slot], preferred_element_type=F32)

    @pl.when(act == 2)
    def _():
        y_ref[...] = jnp.zeros_like(y_ref)


def _experts(xg, wg, wu, wd, idx, steps):
    t, p = EXPERT_TM, EXPERT_PIECES
    rows_in = _piece_rows(D_MODEL, p)
    rows_dn = _piece_rows(D_FF_EXPERT, p)
    w_map = lambda s, we, wp, *_: (idx, we[s], wp[s], 0)
    row_spec = pl.BlockSpec((t, D_MODEL), lambda s, we, wp, sg, tl, *_: (tl[s], 0))
    up_spec = pl.BlockSpec((None, None, rows_in, D_FF_EXPERT), w_map)
    return pl.pallas_call(
        _expert_kernel,
        grid_spec=pltpu.PrefetchScalarGridSpec(
            num_scalar_prefetch=len(steps),
            grid=(EXPERT_STEPS,),
            in_specs=[row_spec, up_spec, up_spec, pl.BlockSpec((None, None, rows_dn, D_MODEL), w_map)],
            out_specs=row_spec,
            scratch_shapes=[pltpu.VMEM((2, D_MODEL, D_FF_EXPERT), BF16), pltpu.VMEM((2, D_MODEL, D_FF_EXPERT), BF16),
                            pltpu.VMEM((2, D_FF_EXPERT, D_MODEL), BF16)]),
        out_shape=jax.ShapeDtypeStruct((SORTED_ROWS, D_MODEL), F32),
        compiler_params=_params(vmem=EXPERT_VMEM_LIMIT),
        name="experts",
    )(*steps, xg, wg, wu, wd)


def _combine_kernel(dest_ref, next_ref, x_ref, route_ref, *rest, final_norm):
    if final_norm:
        gf_ref, y_hbm, o_ref, ya, yb, sem = rest
    else:
        y_hbm, o_ref, ya, yb, sem = rest
    step = pl.program_id(0)
    n = x_ref.shape[0]
    slot = step % 2

    def gather(idx_ref, slot):
        def copies_of(t):
            return [pltpu.make_async_copy(y_hbm.at[pl.ds(idx_ref[0, 2 * t + k], 1)], buf.at[slot, pl.ds(t, 1)],
                                          sem.at[slot])
                    for k, buf in enumerate((ya, yb))]
        return copies_of

    @pl.when(step == 0)
    def _():
        _start_rows(n, gather(dest_ref, 0))

    @pl.when(step + 1 < pl.num_programs(0))
    def _():
        _start_rows(n, gather(next_ref, 1 - slot))

    _wait_rows(n, gather(dest_ref, slot))
    out = (x_ref[...] + route_ref[:, ROUTE_G1:ROUTE_G1 + 1] * ya[slot]
           + route_ref[:, ROUTE_G2:ROUTE_G2 + 1] * yb[slot])
    if final_norm:
        out = _rms(out, gf_ref[...])
    o_ref[...] = out


def _combine(x, route, y, dest, g_final):
    m = x.shape[0]
    n_steps = m // ROUTE_TM
    row_spec = lambda width: pl.BlockSpec((ROUTE_TM, width), lambda s: (s, 0))
    idx_spec = lambda ahead: pl.BlockSpec((None, 1, 2 * ROUTE_TM),
                                          lambda s: (jnp.minimum(s + ahead, n_steps - 1), 0, 0),
                                          memory_space=pltpu.SMEM)
    in_specs = [idx_spec(0), idx_spec(1), row_spec(D_MODEL), row_spec(ROUTER_PAD)]
    args = [dest, dest, x, route]
    if g_final is not None:
        in_specs.append(pl.BlockSpec((1, D_MODEL), lambda s: (0, 0)))
        args.append(g_final)
    return pl.pallas_call(
        functools.partial(_combine_kernel, final_norm=g_final is not None),
        grid=(n_steps,),
        in_specs=in_specs + [pl.BlockSpec(memory_space=pl.ANY)],
        out_specs=row_spec(D_MODEL),
        out_shape=jax.ShapeDtypeStruct((m, D_MODEL), F32),
        scratch_shapes=[pltpu.VMEM((2, ROUTE_TM, D_MODEL), F32), pltpu.VMEM((2, ROUTE_TM, D_MODEL), F32),
                        pltpu.SemaphoreType.DMA((2,))],
        compiler_params=_params(),
        name="combine",
    )(*args, y)


def _ffn_kernel(hn_ref, acc_ref, *rest, pieces, final_norm):
    rest = list(rest)
    gf_ref = rest.pop(0) if final_norm else None
    wg_ref, wu_ref, wd_ref, o_ref, wgb, wub, wdb = rest
    step = pl.program_id(0)
    _stash_piece(step, wg_ref, wgb, pieces[0])
    _stash_piece(step, wu_ref, wub, pieces[0])
    _stash_piece(step, wd_ref, wdb, pieces[1])

    @pl.when(step >= max(pieces))
    def _():
        hn = hn_ref[...]
        gate = jnp.dot(hn, wgb[...], preferred_element_type=F32)
        up = jnp.dot(hn, wub[...], preferred_element_type=F32)
        h = (gate * (1.0 / (1.0 + jnp.exp(-gate)))) * up
        out = acc_ref[...] + jnp.dot(h.astype(BF16), wdb[...], preferred_element_type=F32)
        if final_norm:
            out = _rms(out, gf_ref[...])
        o_ref[...] = out


def _ffn_slab(hn, acc, wg, wu, wd, idx, slab, width, *, tm, g_final=None):
    m = hn.shape[0]
    nm = m // tm
    pieces = (8, 8)
    n_pro = max(pieces)
    rows_in = _piece_rows(D_MODEL, pieces[0])
    rows_dn = _piece_rows(width, pieces[1])
    tile = lambda i: jnp.maximum(i - n_pro, 0)
    row_spec = lambda w_: pl.BlockSpec((tm, w_), lambda i: (tile(i), 0))
    in_specs = [row_spec(D_MODEL), row_spec(D_MODEL)]
    args = [hn, acc]
    if g_final is not None:
        in_specs.append(pl.BlockSpec((1, D_MODEL), lambda i: (0, 0)))
        args.append(g_final)
    up_spec = pl.BlockSpec((None, rows_in, width), lambda i: (idx, jnp.minimum(i, pieces[0] - 1), slab))
    in_specs += [up_spec, up_spec,
                 pl.BlockSpec((None, rows_dn, D_MODEL),
                              lambda i: (idx, slab * pieces[1] + jnp.minimum(i, pieces[1] - 1), 0))]
    args += [wg, wu, wd]
    return pl.pallas_call(
        functools.partial(_ffn_kernel, pieces=pieces, final_norm=g_final is not None),
        grid=(n_pro + nm,),
        in_specs=in_specs,
        out_specs=row_spec(D_MODEL),
        out_shape=jax.ShapeDtypeStruct((m, D_MODEL), F32),
        scratch_shapes=[pltpu.VMEM((D_MODEL, width), BF16), pltpu.VMEM((D_MODEL, width), BF16),
                        pltpu.VMEM((width, D_MODEL), BF16)],
        compiler_params=_params(),
        name="ffn_slab",
    )(*args)


def _rope_tables(pos):
    half = ROPE_DIM // 2
    inv = ROPE_THETA ** (-jnp.arange(0, ROPE_DIM, 2, dtype=F32) / ROPE_DIM)
    ang = pos[:, None] * inv[None, :]
    cos, sin = jnp.cos(ang), jnp.sin(ang)
    n = pos.shape[0]
    rest = HEAD_DIM - ROPE_DIM
    a = jnp.concatenate([cos, cos, jnp.ones((n, rest), F32)], axis=1)
    b = jnp.concatenate([jnp.zeros((n, half), F32), sin, jnp.zeros((n, rest), F32)], axis=1)
    c = jnp.concatenate([-sin, jnp.zeros((n, half + rest), F32)], axis=1)
    reps = LANES // HEAD_DIM
    scale = HEAD_DIM ** -0.5
    return tuple(jnp.tile(t, (1, reps)) * scale for t in (a, b, c))


def kernel(x_prompt, x_sample, cache_k, cache_v, state_pool, norm_mix, w_in, attn_sinks, w_pool, pool_scale,
           w_out, norm_ffn, w_gate_dense, w_up_dense, w_down_dense, w_router, b_router, w_gate_exp, w_up_exp,
           w_down_exp, norm_final):
    tm = 512
    xp = x_prompt.reshape(BATCH * SEQ, D_MODEL)
    xs = x_sample.reshape(DEC_BATCH, D_MODEL)
    rope_p = _rope_tables(jnp.arange(SEQ, dtype=F32))
    rope_s = _rope_tables(jnp.full((DEC_BATCH,), PAST_LEN, dtype=F32))
    ck = cache_k.reshape(DEPTH, DEC_BATCH, WINDOW, KV_WIDTH)
    cv = cache_v.reshape(DEPTH, DEC_BATCH, WINDOW, KV_WIDTH)
    g_final = norm_final.reshape(1, D_MODEL)
    slab_w = D_FF // DENSE_SLABS
    heads = (N_KV_HEADS, HEAD_DIM)

    pool_state = state_pool
    nk_p, nv_p, nu_p = [], [], []
    for l in range(DEPTH):
        g_mix = norm_mix[l].reshape(1, D_MODEL)
        g_ffn = norm_ffn[l].reshape(1, D_MODEL)
        scale = pool_scale[l].reshape(1, POOL_WIDTH)
        moe = l % 2 == 1
        i = l // 2
        last = l == DEPTH - 1
        router = None
        if moe:
            router = (jnp.pad(w_router[i], ((0, 0), (0, ROUTER_PAD - N_EXPERTS))),
                      jnp.pad(b_router[i], (0, ROUTER_PAD - N_EXPERTS)).reshape(1, ROUTER_PAD))

        q, k, v, u, kvt, ut = _inproj(xp, g_mix, w_in, l, rope_p, tm=tm)
        mix = _mixer(q, k, v, u, attn_sinks[l], w_pool, scale, l)
        xp, hn, *routing = _outproj(mix, xp, w_out, g_ffn, l, router, tm=tm)
        nk_p.append(kvt[:, :KV_WIDTH].reshape(BATCH, WINDOW, *heads))
        nv_p.append(kvt[:, KV_WIDTH:].reshape(BATCH, WINDOW, *heads))
        nu_p.append(ut.reshape(BATCH, BF16_ROWS, POOL_WIDTH)[:, BF16_ROWS - POOL_STATE:])
        if moe:
            route, counts = routing
            dest, pad_rows, steps = _route_plan(route, counts)
            y = _experts(_dispatch(hn, dest, pad_rows), w_gate_exp, w_up_exp, w_down_exp, i, steps)
            xp = _combine(xp, route, y, dest, g_final if last else None)
        else:
            for s in range(DENSE_SLABS):
                xp = _ffn_slab(hn, xp, w_gate_dense, w_up_dense, w_down_dense, i, s, slab_w, tm=tm,
                               g_final=g_final if last and s == DENSE_SLABS - 1 else None)

        z = _sample_inproj(xs, g_mix, w_in, l, rope_s)
        a_s, ck, cv, pool_state, ps = _sample_mixer(
            z[:, :ATTN_WIDTH], z[:, ATTN_WIDTH:ATTN_WIDTH + KV_WIDTH],
            z[:, ATTN_WIDTH + KV_WIDTH:ATTN_WIDTH + 2 * KV_WIDTH], z[:, ATTN_WIDTH + 2 * KV_WIDTH:],
            ck, cv, pool_state, attn_sinks[l], w_pool, scale, l)
        mix_s = jnp.concatenate([a_s.reshape(DEC_BATCH, ATTN_WIDTH), ps], axis=1)
        xs = _sample_outproj(mix_s, xs, w_out, l)
        if moe:
            xs = _sample_ffn(xs, g_ffn, w_gate_exp, w_up_exp, w_down_exp, i, router=router,
                             g_final=g_final if last else None)
        else:
            xs = _sample_ffn(xs, g_ffn, w_gate_dense, w_up_dense, w_down_dense, i,
                             g_final=g_final if last else None)

    y_prompt = xp.reshape(BATCH, SEQ, D_MODEL)
    y_sample = xs.reshape(DEC_BATCH, 1, D_MODEL)
    return (y_prompt, y_sample, jnp.stack(nk_p), jnp.stack(nv_p), jnp.stack(nu_p),
            ck.reshape(DEPTH, DEC_BATCH, WINDOW, *heads), cv.reshape(DEPTH, DEC_BATCH, WINDOW, *heads), pool_state)
```

```python
import functools

import jax
import jax.numpy as jnp
from jax import lax
from jax.experimental import pallas as pl
from jax.experimental.pallas import tpu as pltpu

F32 = jnp.float32
BF16 = jnp.bfloat16

D_MODEL = 2048
BATCH = 2
SEQ = 4096
DEPTH = 4
DEC_BATCH = 32
PAST_LEN = 16384
N_HEADS = 16
N_KV_HEADS = 4
HEAD_DIM = 64
GROUP = N_HEADS // N_KV_HEADS
ATTN_WIDTH = N_HEADS * HEAD_DIM
KV_WIDTH = N_KV_HEADS * HEAD_DIM
WINDOW = 128
ROPE_DIM = HEAD_DIM // 4
ROPE_THETA = 500000.0
POOL_WINDOWS = (2, 4, 8, 16)
POOL_WIDTH = D_MODEL - ATTN_WIDTH
POOL_GROUP_DIM = POOL_WIDTH // len(POOL_WINDOWS)
POOL_STATE = max(POOL_WINDOWS) - 1
IN_WIDTH = ATTN_WIDTH + 2 * KV_WIDTH + POOL_WIDTH
D_FF = 5632
N_EXPERTS = 8
D_FF_EXPERT = D_MODEL // 2
EPS = 1e-5
NEG_INF = -1e30

LANES = 128
BF16_ROWS = 16
VMEM_LIMIT = 56 * 1024 * 1024
DENSE_SLABS = 4
ROUTER_PAD = LANES


def _params(n_axes=1, vmem=VMEM_LIMIT):
    return pltpu.CompilerParams(dimension_semantics=("arbitrary",) * n_axes, vmem_limit_bytes=vmem)


def _rms(x, g):
    ms = jnp.mean(x * x, axis=-1, keepdims=True)
    return (x * lax.rsqrt(ms + EPS)) * g


def _rope_mix(zc, ra, rb, rc):
    return zc * ra + pltpu.roll(zc, ROPE_DIM // 2, 1) * rb + pltpu.roll(zc, LANES - ROPE_DIM // 2, 1) * rc


def _split_bf16(a):
    hi = a.astype(BF16)
    return hi, (a - hi.astype(F32)).astype(BF16)


def _dot3(a, b, dims=(((1,), (0,)), ((), ()))):
    m = a.shape[0]
    ah, al = _split_bf16(a)
    bh, bl = _split_bf16(b)
    t = lax.dot_general(jnp.concatenate([ah, al], axis=0), bh, dims, preferred_element_type=F32)
    return t[:m] + t[m:] + lax.dot_general(ah, bl, dims, preferred_element_type=F32)


def _stash_piece(step, w_ref, wbf, n_pieces):
    rows = w_ref.shape[0]

    @pl.when(step < n_pieces)
    def _():
        r0 = pl.multiple_of(step * rows, rows)
        wbf[pl.ds(r0, rows), :] = w_ref[...].astype(BF16)


def _piece_rows(total_rows, n_pieces):
    rows, rem = divmod(total_rows, n_pieces)
    assert rem == 0 and rows % BF16_ROWS == 0, (total_rows, n_pieces)
    return rows


def _inproj_kernel(x_ref, g_ref, w_ref, ra_ref, rb_ref, rc_ref, q_ref, k_ref, v_ref, u_ref, kvt_ref, ut_ref,
                   wbf, *, n_pieces):
    step = pl.program_id(0)
    _stash_piece(step, w_ref, wbf, n_pieces)

    @pl.when(step >= n_pieces)
    def _():
        tm = x_ref.shape[0]
        h = _rms(x_ref[...], g_ref[...])
        z = jnp.dot(h.astype(BF16), wbf[...], preferred_element_type=F32)
        ra, rb, rc = ra_ref[...], rb_ref[...], rc_ref[...]
        n_rot = (ATTN_WIDTH + KV_WIDTH) // LANES
        rot = []
        for c in range(n_rot):
            r = _rope_mix(z[:, c * LANES:(c + 1) * LANES], ra, rb, rc)
            if c >= ATTN_WIDTH // LANES:
                r = r * (HEAD_DIM ** 0.5)
            rot.append(r)
        nq = ATTN_WIDTH // LANES
        for c in range(nq):
            q_ref[:, c * LANES:(c + 1) * LANES] = rot[c].astype(q_ref.dtype)
        for c in range(nq, n_rot):
            k_ref[:, (c - nq) * LANES:(c - nq + 1) * LANES] = rot[c].astype(k_ref.dtype)
        v = z[:, ATTN_WIDTH + KV_WIDTH:ATTN_WIDTH + 2 * KV_WIDTH]
        u = z[:, ATTN_WIDTH + 2 * KV_WIDTH:]
        v_ref[...] = v.astype(v_ref.dtype)
        u_ref[...] = u.astype(u_ref.dtype)
        for c in range(nq, n_rot):
            kvt_ref[:, (c - nq) * LANES:(c - nq + 1) * LANES] = rot[c][tm - WINDOW:, :]
        kvt_ref[:, KV_WIDTH:] = v[tm - WINDOW:, :]
        ut_ref[...] = u[tm - BF16_ROWS:, :]


def _inproj(x, g, w_in, layer, rope, *, tm):
    m = x.shape[0]
    nm = m // tm
    n_pieces = 4
    rows = _piece_rows(D_MODEL, n_pieces)
    tiles_per_seq = SEQ // tm
    tile = lambda i: jnp.maximum(i - n_pieces, 0)
    row_spec = lambda width: pl.BlockSpec((tm, width), lambda i: (tile(i), 0))
    rope_spec = pl.BlockSpec((tm, LANES), lambda i: (tile(i) % tiles_per_seq, 0))
    out_shape = [jax.ShapeDtypeStruct((m, ATTN_WIDTH), BF16),
                 jax.ShapeDtypeStruct((m, KV_WIDTH), BF16),
                 jax.ShapeDtypeStruct((m, KV_WIDTH), BF16),
                 jax.ShapeDtypeStruct((m, POOL_WIDTH), BF16),
                 jax.ShapeDtypeStruct((BATCH * WINDOW, 2 * KV_WIDTH), F32),
                 jax.ShapeDtypeStruct((BATCH * BF16_ROWS, POOL_WIDTH), F32)]
    out_specs = [row_spec(ATTN_WIDTH), row_spec(KV_WIDTH), row_spec(KV_WIDTH), row_spec(POOL_WIDTH),
                 pl.BlockSpec((WINDOW, 2 * KV_WIDTH), lambda i: (tile(i) // tiles_per_seq, 0)),
                 pl.BlockSpec((BF16_ROWS, POOL_WIDTH), lambda i: (tile(i) // tiles_per_seq, 0))]
    return pl.pallas_call(
        functools.partial(_inproj_kernel, n_pieces=n_pieces),
        grid=(n_pieces + nm,),
        in_specs=[row_spec(D_MODEL),
                  pl.BlockSpec((1, D_MODEL), lambda i: (0, 0)),
                  pl.BlockSpec((None, rows, IN_WIDTH), lambda i: (layer, jnp.minimum(i, n_pieces - 1), 0)),
                  rope_spec, rope_spec, rope_spec],
        out_specs=out_specs,
        out_shape=out_shape,
        scratch_shapes=[pltpu.VMEM((D_MODEL, IN_WIDTH), BF16)],
        compiler_params=_params(),
        name="inproj",
    )(x, g, w_in, *rope)


def _sink_softmax(s, sink, axis):
    mx = jnp.maximum(jnp.max(s, axis=axis, keepdims=True), sink)
    p = jnp.exp(s - mx)
    den = jnp.sum(p, axis=axis, keepdims=True) + jnp.exp(sink - mx)
    return p * (1.0 / den)


def _pool_group(d, wpb_g, scale_g):
    return jnp.dot(d.astype(BF16), wpb_g, preferred_element_type=F32) * scale_g


MIXER_QB = 4


def _mixer_kernel(sink_ref, q_ref, kc_ref, kp_ref, vc_ref, vp_ref, uc_ref, up_ref, wp_ref, sc_ref,
                  o_ref, wpb, band):
    b = pl.program_id(0)
    m = pl.program_id(1)
    blk = WINDOW

    @pl.when((b == 0) & (m == 0))
    def _():
        wpb[...] = wp_ref[...].astype(BF16)
        tok = lax.broadcasted_iota(jnp.int32, (blk, 2 * blk), 0) + blk
        src = lax.broadcasted_iota(jnp.int32, (blk, 2 * blk), 1)
        for g, w in enumerate(POOL_WINDOWS):
            band[g] = jnp.where((src <= tok) & (src > tok - w), 1.0, 0.0).astype(BF16)

    half = LANES // 2
    kall = jnp.concatenate([kp_ref[...], kc_ref[...]], axis=0).astype(F32)
    vall = jnp.concatenate([vp_ref[...], vc_ref[...]], axis=0).astype(F32)
    uall = jnp.concatenate([jnp.where(m > 0, up_ref[...], jnp.zeros_like(up_ref)), uc_ref[...]], axis=0)
    lane = lax.broadcasted_iota(jnp.int32, (2 * blk, LANES), 1)
    low = lane < half
    key = lax.broadcasted_iota(jnp.int32, (4 * blk, blk), 0) & (2 * blk - 1)
    qry = lax.broadcasted_iota(jnp.int32, (4 * blk, blk), 1)
    in_window = (key > qry) & (key <= qry + blk)

    for qb in range(MIXER_QB):
        r0 = qb * blk
        kk, vv = kall[r0:r0 + 2 * blk], vall[r0:r0 + 2 * blk]
        valid = in_window & ((m > 0) | (key >= blk)) if qb == 0 else in_window
        for h in range(N_KV_HEADS):
            g0 = (h // 2) * LANES
            kg, vg = kk[:, g0:g0 + LANES], vv[:, g0:g0 + LANES]
            if h % 2 == 0:
                klo, vlo = jnp.where(low, kg, 0.0), jnp.where(low, vg, 0.0)
                khi, vhi = pltpu.roll(klo, half, 1), pltpu.roll(vlo, half, 1)
            else:
                khi, vhi = jnp.where(low, 0.0, kg), jnp.where(low, 0.0, vg)
                klo, vlo = pltpu.roll(khi, half, 1), pltpu.roll(vhi, half, 1)
            kcat = jnp.concatenate([klo, khi], axis=0).astype(BF16)
            vcat = jnp.concatenate([vlo, vhi], axis=0).astype(BF16)
            for pair in range(h * GROUP // 2, (h + 1) * GROUP // 2):
                qp = q_ref[r0:r0 + blk, pair * LANES:(pair + 1) * LANES]
                st = lax.dot_general(kcat, qp, (((1,), (1,)), ((), ())), preferred_element_type=F32)
                st = jnp.where(valid, st, NEG_INF)
                pt = [_sink_softmax(st[t * 2 * blk:(t + 1) * 2 * blk], sink_ref[2 * pair + t], 0).astype(BF16)
                      for t in range(2)]
                o = lax.dot_general(jnp.concatenate(pt, axis=0), vcat, (((0,), (0,)), ((), ())),
                                    preferred_element_type=F32)
                o_ref[r0:r0 + blk, pair * LANES:(pair + 1) * LANES] = o.astype(o_ref.dtype)

        ext = uall[r0:r0 + 2 * blk]
        pos = ((m * MIXER_QB + qb) * blk + lax.broadcasted_iota(jnp.int32, (blk, 1), 0)).astype(F32)
        for g, w in enumerate(POOL_WINDOWS):
            c0, c1 = g * POOL_GROUP_DIM, (g + 1) * POOL_GROUP_DIM
            wsum = jnp.dot(band[g], ext[:, c0:c1], preferred_element_type=F32)
            d = wsum * (1.0 / jnp.minimum(float(w), pos + 1.0)) - uc_ref[r0:r0 + blk, c0:c1].astype(F32)
            y = _pool_group(d, wpb[g], sc_ref[:, c0:c1])
            o_ref[r0:r0 + blk, ATTN_WIDTH + c0:ATTN_WIDTH + c1] = y.astype(o_ref.dtype)


def _mixer(q, k, v, u, sinks, w_pool, scale, layer):
    assert u.dtype == BF16
    qb = MIXER_QB
    nb = SEQ // WINDOW
    ns = nb // qb
    cur = lambda b, m: (b * ns + m, 0)
    prev = lambda b, m: (b * nb + jnp.maximum(m * qb - 1, 0), 0)
    n_grp = len(POOL_WINDOWS)
    return pl.pallas_call(
        _mixer_kernel,
        grid=(BATCH, ns),
        in_specs=[pl.BlockSpec(memory_space=pltpu.SMEM),
                  pl.BlockSpec((qb * WINDOW, ATTN_WIDTH), cur),
                  pl.BlockSpec((qb * WINDOW, KV_WIDTH), cur), pl.BlockSpec((WINDOW, KV_WIDTH), prev),
                  pl.BlockSpec((qb * WINDOW, KV_WIDTH), cur), pl.BlockSpec((WINDOW, KV_WIDTH), prev),
                  pl.BlockSpec((qb * WINDOW, POOL_WIDTH), cur), pl.BlockSpec((WINDOW, POOL_WIDTH), prev),
                  pl.BlockSpec((None, n_grp, POOL_GROUP_DIM, POOL_GROUP_DIM), lambda b, m: (layer, 0, 0, 0)),
                  pl.BlockSpec((1, POOL_WIDTH), lambda b, m: (0, 0))],
        out_specs=pl.BlockSpec((qb * WINDOW, D_MODEL), cur),
        out_shape=jax.ShapeDtypeStruct((BATCH * SEQ, D_MODEL), BF16),
        scratch_shapes=[pltpu.VMEM((n_grp, POOL_GROUP_DIM, POOL_GROUP_DIM), BF16),
                        pltpu.VMEM((n_grp, WINDOW, 2 * WINDOW), BF16)],
        compiler_params=_params(2),
        name="mixer",
    )(sinks, q, k, k, v, v, u, u, w_pool, scale)


SAMPLE_SB = 4


def _sample_mixer_kernel(sink_ref, q_ref, kn_ref, vn_ref, un_ref, ck_ref, cv_ref, st_ref, wp_ref, sc_ref,
                         a_ref, nk_ref, nv_ref, nu_ref, po_ref, dscr):
    step = pl.program_id(0)
    sink = sink_ref[...]
    own = (lax.broadcasted_iota(jnp.int32, (N_HEADS, KV_WIDTH), 0) // GROUP
           == lax.broadcasted_iota(jnp.int32, (N_HEADS, KV_WIDTH), 1) // HEAD_DIM)
    for j in range(SAMPLE_SB):
        nk_ref[j, 0:WINDOW - 1, :] = ck_ref[j, 1:WINDOW, :]
        nk_ref[j, WINDOW - 1:WINDOW, :] = kn_ref[j]
        nv_ref[j, 0:WINDOW - 1, :] = cv_ref[j, 1:WINDOW, :]
        nv_ref[j, WINDOW - 1:WINDOW, :] = vn_ref[j]

        q = q_ref[j]
        qe = jnp.where(own, jnp.concatenate([q] * N_KV_HEADS, axis=1), 0.0)
        s = _dot3(qe, nk_ref[j], (((1,), (1,)), ((), ())))
        r = _dot3(_sink_softmax(s, sink, 1), nv_ref[j])
        r = jnp.where(own, r, 0.0)
        o = r[:, 0:HEAD_DIM]
        for h in range(1, N_KV_HEADS):
            o = o + r[:, h * HEAD_DIM:(h + 1) * HEAD_DIM]
        a_ref[j] = o

        st = st_ref[j]
        un = un_ref[j]
        nu_ref[j, 0:POOL_STATE - 1, :] = st[1:POOL_STATE, :]
        nu_ref[j, POOL_STATE - 1:POOL_STATE, :] = un
        for g, w in enumerate(POOL_WINDOWS):
            c0, c1 = g * POOL_GROUP_DIM, (g + 1) * POOL_GROUP_DIM
            tok = un[:, c0:c1]
            wsum = tok + jnp.sum(st[POOL_STATE - (w - 1):POOL_STATE, c0:c1], axis=0, keepdims=True)
            dscr[pl.ds(step * SAMPLE_SB + j, 1), c0:c1] = wsum / float(min(w, PAST_LEN + 1)) - tok

    @pl.when(step == pl.num_programs(0) - 1)
    def _():
        for g in range(len(POOL_WINDOWS)):
            c0, c1 = g * POOL_GROUP_DIM, (g + 1) * POOL_GROUP_DIM
            po_ref[:, c0:c1] = _dot3(dscr[:, c0:c1], wp_ref[g]) * sc_ref[:, c0:c1]


def _sample_mixer(q, k, v, u, cache_k, cache_v, state, sinks, w_pool, scale, layer):
    nb, sb = DEC_BATCH, SAMPLE_SB
    per_b = lambda *shape: pl.BlockSpec((sb,) + shape, lambda b: (b,) + (0,) * len(shape))
    per_lb = lambda *shape: pl.BlockSpec((None, sb) + shape, lambda b: (layer, b) + (0,) * len(shape))
    return pl.pallas_call(
        _sample_mixer_kernel,
        grid=(nb // sb,),
        in_specs=[pl.BlockSpec((N_HEADS, 1), lambda b: (0, 0)),
                  per_b(N_HEADS, HEAD_DIM), per_b(1, KV_WIDTH), per_b(1, KV_WIDTH), per_b(1, POOL_WIDTH),
                  per_lb(WINDOW, KV_WIDTH), per_lb(WINDOW, KV_WIDTH), per_lb(POOL_STATE, POOL_WIDTH),
                  pl.BlockSpec((None, len(POOL_WINDOWS), POOL_GROUP_DIM, POOL_GROUP_DIM),
                               lambda b: (layer, 0, 0, 0)),
                  pl.BlockSpec((1, POOL_WIDTH), lambda b: (0, 0))],
        out_specs=[per_b(N_HEADS, HEAD_DIM), per_lb(WINDOW, KV_WIDTH), per_lb(WINDOW, KV_WIDTH),
                   per_lb(POOL_STATE, POOL_WIDTH), pl.BlockSpec((nb, POOL_WIDTH), lambda b: (0, 0))],
        out_shape=[jax.ShapeDtypeStruct((nb, N_HEADS, HEAD_DIM), F32),
                   jax.ShapeDtypeStruct(cache_k.shape, F32),
                   jax.ShapeDtypeStruct(cache_v.shape, F32),
                   jax.ShapeDtypeStruct(state.shape, F32),
                   jax.ShapeDtypeStruct((nb, POOL_WIDTH), F32)],
        scratch_shapes=[pltpu.VMEM((nb, POOL_WIDTH), F32)],
        input_output_aliases={5: 1, 6: 2, 7: 3},
        compiler_params=_params(),
        name="sample_mixer",
    )(sinks.reshape(N_HEADS, 1), q.reshape(nb, N_HEADS, HEAD_DIM), k.reshape(nb, 1, KV_WIDTH),
      v.reshape(nb, 1, KV_WIDTH), u.reshape(nb, 1, POOL_WIDTH), cache_k, cache_v, state, w_pool, scale)


SAMPLE_TN = 512


def _sample_inproj_kernel(x_ref, g_ref, w_ref, ra_ref, rb_ref, rc_ref, z_ref):
    j = pl.program_id(0)
    z = _dot3(_rms(x_ref[...], g_ref[...]), w_ref[...])
    ra, rb, rc = ra_ref[...], rb_ref[...], rc_ref[...]
    per_tile = SAMPLE_TN // LANES
    for c in range(per_tile):
        zc = z[:, c * LANES:(c + 1) * LANES]
        grp = j * per_tile + c
        r = _rope_mix(zc, ra, rb, rc)
        r = jnp.where(grp >= ATTN_WIDTH // LANES, r * (HEAD_DIM ** 0.5), r)
        z_ref[:, c * LANES:(c + 1) * LANES] = jnp.where(grp < (ATTN_WIDTH + KV_WIDTH) // LANES, r, zc)


def _sample_inproj(x, g, w_in, layer, rope):
    m = x.shape[0]
    const = lambda r, c: pl.BlockSpec((r, c), lambda j: (0, 0))
    return pl.pallas_call(
        _sample_inproj_kernel,
        grid=(IN_WIDTH // SAMPLE_TN,),
        in_specs=[const(m, D_MODEL), const(1, D_MODEL),
                  pl.BlockSpec((None, D_MODEL, SAMPLE_TN), lambda j: (layer, 0, j)),
                  const(m, LANES), const(m, LANES), const(m, LANES)],
        out_specs=pl.BlockSpec((m, SAMPLE_TN), lambda j: (0, j)),
        out_shape=jax.ShapeDtypeStruct((m, IN_WIDTH), F32),
        compiler_params=_params(),
        name="sample_inproj",
    )(x, g, w_in, *rope)


def _sample_outproj_kernel(mix_ref, x_ref, w_ref, o_ref):
    o_ref[...] = x_ref[...] + _dot3(mix_ref[...], w_ref[...])


def _sample_outproj(mix, x, w_out, layer):
    m = x.shape[0]
    col = pl.BlockSpec((m, SAMPLE_TN), lambda j: (0, j))
    return pl.pallas_call(
        _sample_outproj_kernel,
        grid=(D_MODEL // SAMPLE_TN,),
        in_specs=[pl.BlockSpec((m, D_MODEL), lambda j: (0, 0)), col,
                  pl.BlockSpec((None, D_MODEL, SAMPLE_TN), lambda j: (layer, 0, j))],
        out_specs=col,
        out_shape=jax.ShapeDtypeStruct((m, D_MODEL), F32),
        compiler_params=_params(),
        name="sample_outproj",
    )(mix, x, w_out)


def _sample_ffn_kernel(x_ref, g_ref, *rest, moe, final_norm):
    rest = list(rest)
    wr_ref, br_ref = (rest.pop(0), rest.pop(0)) if moe else (None, None)
    gf_ref = rest.pop(0) if final_norm else None
    wg_ref, wu_ref, wd_ref, o_ref, acc, hn_scr, cmb = rest
    e, f = pl.program_id(0), pl.program_id(1)

    @pl.when((e == 0) & (f == 0))
    def _():
        hn = _rms(x_ref[...], g_ref[...])
        hn_scr[...] = hn
        acc[...] = jnp.zeros_like(acc)
        if moe:
            cmb[...] = _top2_gates(_dot3(hn, wr_ref[...]) + br_ref[...])

    hn = hn_scr[...]
    gate = _dot3(hn, wg_ref[...])
    up = _dot3(hn, wu_ref[...])
    h = (gate * (1.0 / (1.0 + jnp.exp(-gate)))) * up
    if moe:
        lane = lax.broadcasted_iota(jnp.int32, cmb.shape, 1)
        h = h * jnp.sum(jnp.where(lane == e, cmb[...], 0.0), axis=1, keepdims=True)
    acc[...] += _dot3(h, wd_ref[...])

    @pl.when((e == pl.num_programs(0) - 1) & (f == pl.num_programs(1) - 1))
    def _():
        out = x_ref[...] + acc[...]
        if final_norm:
            out = _rms(out, gf_ref[...])
        o_ref[...] = out


def _sample_ffn(x, g, wg, wu, wd, idx, *, router=None, g_final=None):
    m = x.shape[0]
    moe = router is not None
    n_e = N_EXPERTS if moe else 1
    width = wg.shape[-1]
    const = lambda r, c: pl.BlockSpec((r, c), lambda e, f: (0, 0))
    in_specs = [const(m, D_MODEL), const(1, D_MODEL)]
    args = [x, g]
    if moe:
        in_specs += [const(D_MODEL, ROUTER_PAD), const(1, ROUTER_PAD)]
        args += list(router)
        up_spec = pl.BlockSpec((None, None, D_MODEL, SAMPLE_TN), lambda e, f: (idx, e, 0, f))
        dn_spec = pl.BlockSpec((None, None, SAMPLE_TN, D_MODEL), lambda e, f: (idx, e, f, 0))
    else:
        up_spec = pl.BlockSpec((None, D_MODEL, SAMPLE_TN), lambda e, f: (idx, 0, f))
        dn_spec = pl.BlockSpec((None, SAMPLE_TN, D_MODEL), lambda e, f: (idx, f, 0))
    if g_final is not None:
        in_specs.append(const(1, D_MODEL))
        args.append(g_final)
    return pl.pallas_call(
        functools.partial(_sample_ffn_kernel, moe=moe, final_norm=g_final is not None),
        grid=(n_e, width // SAMPLE_TN),
        in_specs=in_specs + [up_spec, up_spec, dn_spec],
        out_specs=const(m, D_MODEL),
        out_shape=jax.ShapeDtypeStruct((m, D_MODEL), F32),
        scratch_shapes=[pltpu.VMEM((m, D_MODEL), F32), pltpu.VMEM((m, D_MODEL), F32),
                        pltpu.VMEM((m, ROUTER_PAD), F32)],
        compiler_params=_params(2),
        name="sample_ffn",
    )(*args, wg, wu, wd)


def _top2(logits):
    lane = lax.broadcasted_iota(jnp.int32, logits.shape, 1)
    lg = jnp.where(lane < N_EXPERTS, logits, -jnp.inf)
    m1 = jnp.max(lg, axis=1, keepdims=True)
    i1 = jnp.min(jnp.where(lg == m1, lane, ROUTER_PAD), axis=1, keepdims=True)
    lg2 = jnp.where(lane == i1, -jnp.inf, lg)
    m2 = jnp.max(lg2, axis=1, keepdims=True)
    i2 = jnp.min(jnp.where(lg2 == m2, lane, ROUTER_PAD), axis=1, keepdims=True)
    e2 = jnp.exp(m2 - m1)
    den = 1.0 + e2
    return i1, i2, 1.0 / den, e2 / den


def _top2_gates(logits):
    i1, i2, g1, g2 = _top2(logits)
    lane = lax.broadcasted_iota(jnp.int32, logits.shape, 1)
    return jnp.where(lane == i1, g1, 0.0) + jnp.where(lane == i2, g2, 0.0)


ROUTE_I1, ROUTE_I2, ROUTE_G1, ROUTE_G2, ROUTE_P1, ROUTE_P2 = range(6)


def _outproj_kernel(mix_ref, x_ref, w_ref, g_ref, *rest, n_pieces, with_router):
    if with_router:
        wr_ref, br_ref, xo_ref, hn_ref, route_ref, cnt_ref, wbf, tri, carry = rest
    else:
        xo_ref, hn_ref, wbf = rest
    step = pl.program_id(0)
    _stash_piece(step, w_ref, wbf, n_pieces)

    if with_router:
        @pl.when(step == 0)
        def _():
            row = lax.broadcasted_iota(jnp.int32, tri.shape, 0)
            col = lax.broadcasted_iota(jnp.int32, tri.shape, 1)
            tri[...] = jnp.where(col < row, 1.0, 0.0).astype(BF16)
            carry[...] = jnp.zeros_like(carry)

    @pl.when(step >= n_pieces)
    def _():
        y = jnp.dot(mix_ref[...], wbf[...], preferred_element_type=F32)
        xo = x_ref[...] + y
        xo_ref[...] = xo
        hn = _rms(xo, g_ref[...])
        if not with_router:
            hn_ref[...] = hn.astype(BF16)
            return
        hn_ref[...] = hn
        logits = jnp.dot(hn.astype(BF16), wr_ref[...].astype(BF16), preferred_element_type=F32) + br_ref[...]
        i1, i2, g1, g2 = _top2(logits)
        lane = lax.broadcasted_iota(jnp.int32, logits.shape, 1)
        picked = jnp.where((lane == i1) | (lane == i2), 1.0, 0.0)
        ahead = jnp.dot(tri[...], picked.astype(BF16), preferred_element_type=F32) + carry[...]
        p1 = jnp.sum(jnp.where(lane == i1, ahead, 0.0), axis=1, keepdims=True)
        p2 = jnp.sum(jnp.where(lane == i2, ahead, 0.0), axis=1, keepdims=True)
        carry[...] += jnp.sum(picked, axis=0, keepdims=True)
        cnt_ref[...] = carry[...]
        rec = jnp.zeros(logits.shape, F32)
        for k, val in ((ROUTE_I1, i1.astype(F32)), (ROUTE_I2, i2.astype(F32)), (ROUTE_G1, g1), (ROUTE_G2, g2),
                       (ROUTE_P1, p1), (ROUTE_P2, p2)):
            rec = jnp.where(lane == k, val, rec)
        route_ref[...] = rec


def _outproj(mix, x, w_out, g, layer, router, *, tm):
    m = x.shape[0]
    nm = m // tm
    n_pieces = 4
    rows = _piece_rows(D_MODEL, n_pieces)
    tile = lambda i: jnp.maximum(i - n_pieces, 0)
    row_spec = lambda width: pl.BlockSpec((tm, width), lambda i: (tile(i), 0))
    const = lambda r, c: pl.BlockSpec((r, c), lambda i: (0, 0))
    in_specs = [row_spec(D_MODEL), row_spec(D_MODEL),
                pl.BlockSpec((None, rows, D_MODEL), lambda i: (layer, jnp.minimum(i, n_pieces - 1), 0)),
                const(1, D_MODEL)]
    args = [mix, x, w_out, g]
    scratch = [pltpu.VMEM((D_MODEL, D_MODEL), BF16)]
    if router is None:
        out_shape = [jax.ShapeDtypeStruct((m, D_MODEL), F32), jax.ShapeDtypeStruct((m, D_MODEL), BF16)]
        out_specs = [row_spec(D_MODEL), row_spec(D_MODEL)]
    else:
        in_specs += [const(D_MODEL, ROUTER_PAD), const(1, ROUTER_PAD)]
        args += list(router)
        out_shape = [jax.ShapeDtypeStruct((m, D_MODEL), F32), jax.ShapeDtypeStruct((m, D_MODEL), F32),
                     jax.ShapeDtypeStruct((m, ROUTER_PAD), F32), jax.ShapeDtypeStruct((1, ROUTER_PAD), F32)]
        out_specs = [row_spec(D_MODEL), row_spec(D_MODEL), row_spec(ROUTER_PAD), const(1, ROUTER_PAD)]
        scratch += [pltpu.VMEM((tm, tm), BF16), pltpu.VMEM((1, ROUTER_PAD), F32)]
    return pl.pallas_call(
        functools.partial(_outproj_kernel, n_pieces=n_pieces, with_router=router is not None),
        grid=(n_pieces + nm,),
        in_specs=in_specs,
        out_specs=out_specs,
        out_shape=out_shape,
        scratch_shapes=scratch,
        compiler_params=_params(),
        name="outproj",
    )(*args)


EXPERT_TM = 512
EXPERT_PIECES = 4
ROUTE_TM = 512
SORTED_ROWS = BATCH * SEQ * 2 + N_EXPERTS * EXPERT_TM
EXPERT_STEPS = N_EXPERTS * EXPERT_PIECES + SORTED_ROWS // EXPERT_TM
EXPERT_VMEM_LIMIT = 60 * 1024 * 1024


def _take(table, idx):
    hot = idx[..., None] == jnp.arange(table.shape[0], dtype=idx.dtype)
    return jnp.sum(jnp.where(hot, table, 0), axis=-1)


def _route_plan(route, counts):
    i32 = jnp.int32
    t, p = EXPERT_TM, EXPERT_PIECES
    cnt = counts[0, :N_EXPERTS].astype(i32)
    ntile = (cnt + t - 1) // t
    tile0 = jnp.cumsum(ntile) - ntile
    experts = route[:, ROUTE_I1:ROUTE_I2 + 1].astype(i32)
    dest = _take(tile0, experts) * t + route[:, ROUTE_P1:ROUTE_P2 + 1].astype(i32)
    n_all = SORTED_ROWS // t
    last = N_EXPERTS - 1
    ids = jnp.arange(N_EXPERTS, dtype=i32)
    blen = jnp.where(ids < last, jnp.maximum(ntile, p), ntile)
    bstart = p + jnp.cumsum(blen) - blen
    total = p + jnp.sum(blen)
    n_tiles = jnp.sum(ntile)
    s = jnp.arange(EXPERT_STEPS, dtype=i32)
    pro = s < p
    e = jnp.clip(jnp.sum((s[:, None] >= bstart[None, :]).astype(i32), axis=1) - 1, 0, last)
    k = s - _take(bstart, e)
    live = jnp.logical_not(pro) & (s < total)
    comp = live & (k < _take(ntile, e))
    stage = pro | (live & (k < p) & (e < last))
    w_expert = jnp.where(pro, 0, jnp.minimum(e + 1, last))
    w_piece = jnp.where(pro, s, jnp.where(stage, k, p - 1))
    first = _take(tile0, e)
    spare = n_tiles + s - total
    tile = jnp.where(comp, first + k,
                     jnp.where(pro, 0, jnp.where(live, first + _take(ntile, e), spare)))
    tile = jnp.minimum(tile, n_all - 1)
    act = jnp.where(comp, 1, jnp.where((s >= total) & (spare < n_all), 2, 0))
    nvalid = jnp.clip(_take(cnt, e) - (tile - first) * t, 0, t)
    steps = (w_expert, w_piece, stage.astype(i32), tile, act, nvalid, e % 2)
    dest = dest.reshape(dest.shape[0] // ROUTE_TM, 1, 2 * ROUTE_TM)
    partial = jnp.where(ntile > 0, tile0 + ntile - 1, -1)
    unused = n_tiles + jnp.arange(N_EXPERTS, dtype=i32)
    pad_rows = jnp.concatenate([partial, jnp.where(unused < n_all, unused, -1)]) * t
    return dest, pad_rows, steps


def _start_rows(n, copies_of):
    def start(t, c):
        for cp in copies_of(t):
            cp.start()
        return c

    lax.fori_loop(0, n, start, 0, unroll=8)


def _wait_rows(n, copies_of):
    def wait(t, c):
        for cp in copies_of(t):
            cp.wait()
        return c

    lax.fori_loop(0, n, wait, 0, unroll=8)


def _dispatch_kernel(dest_ref, pad_ref, hn_ref, xg_hbm, zeros, sem):
    step = pl.program_id(0)

    @pl.when(step == 0)
    def _():
        zeros[...] = jnp.zeros_like(zeros)

        def fill(j):
            row = pl.multiple_of(jnp.maximum(pad_ref[j], 0), EXPERT_TM)
            return pltpu.make_async_copy(zeros, xg_hbm.at[pl.ds(row, EXPERT_TM)], sem)

        for j in range(pad_ref.shape[0]):
            pl.when(pad_ref[j] >= 0)(lambda j=j: fill(j).start())
        for j in range(pad_ref.shape[0]):
            pl.when(pad_ref[j] >= 0)(lambda j=j: fill(j).wait())

    def copies_of(t):
        src = hn_ref.at[pl.ds(t, 1)]
        return [pltpu.make_async_copy(src, xg_hbm.at[pl.ds(dest_ref[0, 2 * t + k], 1)], sem) for k in range(2)]

    _start_rows(ROUTE_TM, copies_of)
    _wait_rows(ROUTE_TM, copies_of)


def _dispatch(hn, dest, pad_rows):
    return pl.pallas_call(
        _dispatch_kernel,
        grid=(hn.shape[0] // ROUTE_TM,),
        in_specs=[pl.BlockSpec((None, 1, 2 * ROUTE_TM), lambda s: (s, 0, 0), memory_space=pltpu.SMEM),
                  pl.BlockSpec(memory_space=pltpu.SMEM),
                  pl.BlockSpec((ROUTE_TM, D_MODEL), lambda s: (s, 0))],
        out_specs=pl.BlockSpec(memory_space=pl.ANY),
        out_shape=jax.ShapeDtypeStruct((SORTED_ROWS, D_MODEL), F32),
        scratch_shapes=[pltpu.VMEM((EXPERT_TM, D_MODEL), F32), pltpu.SemaphoreType.DMA],
        compiler_params=_params(),
        name="dispatch",
    )(dest, pad_rows, hn)


def _expert_kernel(st_we, st_wp, st_stage, st_t, st_act, st_nv, st_slot, x_ref, wg_ref, wu_ref, wd_ref, y_ref,
                   wgb, wub, wdb):
    s = pl.program_id(0)

    @pl.when(st_stage[s] == 1)
    def _():
        slot = st_we[s] & 1
        for w_ref, wbf in ((wg_ref, wgb), (wu_ref, wub), (wd_ref, wdb)):
            rows = w_ref.shape[0]
            r0 = pl.multiple_of(st_wp[s] * rows, rows)
            wbf[slot, pl.ds(r0, rows), :] = w_ref[...].astype(BF16)

    act = st_act[s]

    @pl.when(act == 1)
    def _():
        slot = st_slot[s]
        row = lax.broadcasted_iota(jnp.int32, (x_ref.shape[0], 1), 0)
        x = jnp.where(row < st_nv[s], x_ref[...], 0.0).astype(BF16)
        gate = jnp.dot(x, wgb[slot], preferred_element_type=F32)
        up = jnp.dot(x, wub[slot], preferred_element_type=F32)
        h = (gate * (1.0 / (1.0 + jnp.exp(-gate)))) * up
        y_ref[...] = jnp.dot(h.astype(BF16), wdb[slot], preferred_element_type=F32)

    @pl.when(act == 2)
    def _():
        y_ref[...] = jnp.zeros_like(y_ref)


def _experts(xg, wg, wu, wd, idx, steps):
    t, p = EXPERT_TM, EXPERT_PIECES
    rows_in = _piece_rows(D_MODEL, p)
    rows_dn = _piece_rows(D_FF_EXPERT, p)
    w_map = lambda s, we, wp, *_: (idx, we[s], wp[s], 0)
    row_spec = pl.BlockSpec((t, D_MODEL), lambda s, we, wp, sg, tl, *_: (tl[s], 0))
    up_spec = pl.BlockSpec((None, None, rows_in, D_FF_EXPERT), w_map)
    return pl.pallas_call(
        _expert_kernel,
        grid_spec=pltpu.PrefetchScalarGridSpec(
            num_scalar_prefetch=len(steps),
            grid=(EXPERT_STEPS,),
            in_specs=[row_spec, up_spec, up_spec, pl.BlockSpec((None, None, rows_dn, D_MODEL), w_map)],
            out_specs=row_spec,
            scratch_shapes=[pltpu.VMEM((2, D_MODEL, D_FF_EXPERT), BF16), pltpu.VMEM((2, D_MODEL, D_FF_EXPERT), BF16),
                            pltpu.VMEM((2, D_FF_EXPERT, D_MODEL), BF16)]),
        out_shape=jax.ShapeDtypeStruct((SORTED_ROWS, D_MODEL), F32),
        compiler_params=_params(vmem=EXPERT_VMEM_LIMIT),
        name="experts",
    )(*steps, xg, wg, wu, wd)


def _combine_kernel(dest_ref, next_ref, x_ref, route_ref, *rest, final_norm):
    if final_norm:
        gf_ref, y_hbm, o_ref, ya, yb, sem = rest
    else:
        y_hbm, o_ref, ya, yb, sem = rest
    step = pl.program_id(0)
    n = x_ref.shape[0]
    slot = step % 2

    def gather(idx_ref, slot):
        def copies_of(t):
            return [pltpu.make_async_copy(y_hbm.at[pl.ds(idx_ref[0, 2 * t + k], 1)], buf.at[slot, pl.ds(t, 1)],
                                          sem.at[slot])
                    for k, buf in enumerate((ya, yb))]
        return copies_of

    @pl.when(step == 0)
    def _():
        _start_rows(n, gather(dest_ref, 0))

    @pl.when(step + 1 < pl.num_programs(0))
    def _():
        _start_rows(n, gather(next_ref, 1 - slot))

    _wait_rows(n, gather(dest_ref, slot))
    out = (x_ref[...] + route_ref[:, ROUTE_G1:ROUTE_G1 + 1] * ya[slot]
           + route_ref[:, ROUTE_G2:ROUTE_G2 + 1] * yb[slot])
    if final_norm:
        out = _rms(out, gf_ref[...])
    o_ref[...] = out


def _combine(x, route, y, dest, g_final):
    m = x.shape[0]
    n_steps = m // ROUTE_TM
    row_spec = lambda width: pl.BlockSpec((ROUTE_TM, width), lambda s: (s, 0))
    idx_spec = lambda ahead: pl.BlockSpec((None, 1, 2 * ROUTE_TM),
                                          lambda s: (jnp.minimum(s + ahead, n_steps - 1), 0, 0),
                                          memory_space=pltpu.SMEM)
    in_specs = [idx_spec(0), idx_spec(1), row_spec(D_MODEL), row_spec(ROUTER_PAD)]
    args = [dest, dest, x, route]
    if g_final is not None:
        in_specs.append(pl.BlockSpec((1, D_MODEL), lambda s: (0, 0)))
        args.append(g_final)
    return pl.pallas_call(
        functools.partial(_combine_kernel, final_norm=g_final is not None),
        grid=(n_steps,),
        in_specs=in_specs + [pl.BlockSpec(memory_space=pl.ANY)],
        out_specs=row_spec(D_MODEL),
        out_shape=jax.ShapeDtypeStruct((m, D_MODEL), F32),
        scratch_shapes=[pltpu.VMEM((2, ROUTE_TM, D_MODEL), F32), pltpu.VMEM((2, ROUTE_TM, D_MODEL), F32),
                        pltpu.SemaphoreType.DMA((2,))],
        compiler_params=_params(),
        name="combine",
    )(*args, y)


def _ffn_kernel(hn_ref, acc_ref, *rest, pieces, final_norm):
    rest = list(rest)
    gf_ref = rest.pop(0) if final_norm else None
    wg_ref, wu_ref, wd_ref, o_ref, wgb, wub, wdb = rest
    step = pl.program_id(0)
    _stash_piece(step, wg_ref, wgb, pieces[0])
    _stash_piece(step, wu_ref, wub, pieces[0])
    _stash_piece(step, wd_ref, wdb, pieces[1])

    @pl.when(step >= max(pieces))
    def _():
        hn = hn_ref[...]
        gate = jnp.dot(hn, wgb[...], preferred_element_type=F32)
        up = jnp.dot(hn, wub[...], preferred_element_type=F32)
        h = (gate * (1.0 / (1.0 + jnp.exp(-gate)))) * up
        out = acc_ref[...] + jnp.dot(h.astype(BF16), wdb[...], preferred_element_type=F32)
        if final_norm:
            out = _rms(out, gf_ref[...])
        o_ref[...] = out


def _ffn_slab(hn, acc, wg, wu, wd, idx, slab, width, *, tm, g_final=None):
    m = hn.shape[0]
    nm = m // tm
    pieces = (8, 8)
    n_pro = max(pieces)
    rows_in = _piece_rows(D_MODEL, pieces[0])
    rows_dn = _piece_rows(width, pieces[1])
    tile = lambda i: jnp.maximum(i - n_pro, 0)
    row_spec = lambda w_: pl.BlockSpec((tm, w_), lambda i: (tile(i), 0))
    in_specs = [row_spec(D_MODEL), row_spec(D_MODEL)]
    args = [hn, acc]
    if g_final is not None:
        in_specs.append(pl.BlockSpec((1, D_MODEL), lambda i: (0, 0)))
        args.append(g_final)
    up_spec = pl.BlockSpec((None, rows_in, width), lambda i: (idx, jnp.minimum(i, pieces[0] - 1), slab))
    in_specs += [up_spec, up_spec,
                 pl.BlockSpec((None, rows_dn, D_MODEL),
                              lambda i: (idx, slab * pieces[1] + jnp.minimum(i, pieces[1] - 1), 0))]
    args += [wg, wu, wd]
    return pl.pallas_call(
        functools.partial(_ffn_kernel, pieces=pieces, final_norm=g_final is not None),
        grid=(n_pro + nm,),
        in_specs=in_specs,
        out_specs=row_spec(D_MODEL),
        out_shape=jax.ShapeDtypeStruct((m, D_MODEL), F32),
        scratch_shapes=[pltpu.VMEM((D_MODEL, width), BF16), pltpu.VMEM((D_MODEL, width), BF16),
                        pltpu.VMEM((width, D_MODEL), BF16)],
        compiler_params=_params(),
        name="ffn_slab",
    )(*args)


def _rope_tables(pos):
    half = ROPE_DIM // 2
    inv = ROPE_THETA ** (-jnp.arange(0, ROPE_DIM, 2, dtype=F32) / ROPE_DIM)
    ang = pos[:, None] * inv[None, :]
    cos, sin = jnp.cos(ang), jnp.sin(ang)
    n = pos.shape[0]
    rest = HEAD_DIM - ROPE_DIM
    a = jnp.concatenate([cos, cos, jnp.ones((n, rest), F32)], axis=1)
    b = jnp.concatenate([jnp.zeros((n, half), F32), sin, jnp.zeros((n, rest), F32)], axis=1)
    c = jnp.concatenate([-sin, jnp.zeros((n, half + rest), F32)], axis=1)
    reps = LANES // HEAD_DIM
    scale = HEAD_DIM ** -0.5
    return tuple(jnp.tile(t, (1, reps)) * scale for t in (a, b, c))


def kernel(x_prompt, x_sample, cache_k, cache_v, state_pool, norm_mix, w_in, attn_sinks, w_pool, pool_scale,
           w_out, norm_ffn, w_gate_dense, w_up_dense, w_down_dense, w_router, b_router, w_gate_exp, w_up_exp,
           w_down_exp, norm_final):
    tm = 512
    xp = x_prompt.reshape(BATCH * SEQ, D_MODEL)
    xs = x_sample.reshape(DEC_BATCH, D_MODEL)
    rope_p = _rope_tables(jnp.arange(SEQ, dtype=F32))
    rope_s = _rope_tables(jnp.full((DEC_BATCH,), PAST_LEN, dtype=F32))
    ck = cache_k.reshape(DEPTH, DEC_BATCH, WINDOW, KV_WIDTH)
    cv = cache_v.reshape(DEPTH, DEC_BATCH, WINDOW, KV_WIDTH)
    g_final = norm_final.reshape(1, D_MODEL)
    slab_w = D_FF // DENSE_SLABS
    heads = (N_KV_HEADS, HEAD_DIM)

    pool_state = state_pool
    nk_p, nv_p, nu_p = [], [], []
    for l in range(DEPTH):
        g_mix = norm_mix[l].reshape(1, D_MODEL)
        g_ffn = norm_ffn[l].reshape(1, D_MODEL)
        scale = pool_scale[l].reshape(1, POOL_WIDTH)
        moe = l % 2 == 1
        i = l // 2
        last = l == DEPTH - 1
        router = None
        if moe:
            router = (jnp.pad(w_router[i], ((0, 0), (0, ROUTER_PAD - N_EXPERTS))),
                      jnp.pad(b_router[i], (0, ROUTER_PAD - N_EXPERTS)).reshape(1, ROUTER_PAD))

        q, k, v, u, kvt, ut = _inproj(xp, g_mix, w_in, l, rope_p, tm=tm)
        mix = _mixer(q, k, v, u, attn_sinks[l], w_pool, scale, l)
        xp, hn, *routing = _outproj(mix, xp, w_out, g_ffn, l, router, tm=tm)
        nk_p.append(kvt[:, :KV_WIDTH].reshape(BATCH, WINDOW, *heads))
        nv_p.append(kvt[:, KV_WIDTH:].reshape(BATCH, WINDOW, *heads))
        nu_p.append(ut.reshape(BATCH, BF16_ROWS, POOL_WIDTH)[:, BF16_ROWS - POOL_STATE:])
        if moe:
            route, counts = routing
            dest, pad_rows, steps = _route_plan(route, counts)
            y = _experts(_dispatch(hn, dest, pad_rows), w_gate_exp, w_up_exp, w_down_exp, i, steps)
            xp = _combine(xp, route, y, dest, g_final if last else None)
        else:
            for s in range(DENSE_SLABS):
                xp = _ffn_slab(hn, xp, w_gate_dense, w_up_dense, w_down_dense, i, s, slab_w, tm=tm,
                               g_final=g_final if last and s == DENSE_SLABS - 1 else None)

        z = _sample_inproj(xs, g_mix, w_in, l, rope_s)
        a_s, ck, cv, pool_state, ps = _sample_mixer(
            z[:, :ATTN_WIDTH], z[:, ATTN_WIDTH:ATTN_WIDTH + KV_WIDTH],
            z[:, ATTN_WIDTH + KV_WIDTH:ATTN_WIDTH + 2 * KV_WIDTH], z[:, ATTN_WIDTH + 2 * KV_WIDTH:],
            ck, cv, pool_state, attn_sinks[l], w_pool, scale, l)
        mix_s = jnp.concatenate([a_s.reshape(DEC_BATCH, ATTN_WIDTH), ps], axis=1)
        xs = _sample_outproj(mix_s, xs, w_out, l)
        if moe:
            xs = _sample_ffn(xs, g_ffn, w_gate_exp, w_up_exp, w_down_exp, i, router=router,
                             g_final=g_final if last else None)
        else:
            xs = _sample_ffn(xs, g_ffn, w_gate_dense, w_up_dense, w_down_dense, i,
                             g_final=g_final if last else None)

    y_prompt = xp.reshape(BATCH, SEQ, D_MODEL)
    y_sample = xs.reshape(DEC_BATCH, 1, D_MODEL)
    return (y_prompt, y_sample, jnp.stack(nk_p), jnp.stack(nv_p), jnp.stack(nu_p),
            ck.reshape(DEPTH, DEC_BATCH, WINDOW, *heads), cv.reshape(DEPTH, DEC_BATCH, WINDOW, *heads), pool_state)
```

```python
import functools

import jax
import jax.numpy as jnp
from jax import lax
from jax.experimental import pallas as pl
from jax.experimental.pallas import tpu as pltpu

F32 = jnp.float32
BF16 = jnp.bfloat16

D_MODEL = 2048
BATCH = 2
SEQ = 4096
DEPTH = 4
DEC_BATCH = 32
PAST_LEN = 16384
N_HEADS = 16
N_KV_HEADS = 4
HEAD_DIM = 64
GROUP = N_HEADS // N_KV_HEADS
ATTN_WIDTH = N_HEADS * HEAD_DIM
KV_WIDTH = N_KV_HEADS * HEAD_DIM
WINDOW = 128
ROPE_DIM = HEAD_DIM // 4
ROPE_THETA = 500000.0
POOL_WINDOWS = (2, 4, 8, 16)
POOL_WIDTH = D_MODEL - ATTN_WIDTH
POOL_GROUP_DIM = POOL_WIDTH // len(POOL_WINDOWS)
POOL_STATE = max(POOL_WINDOWS) - 1
IN_WIDTH = ATTN_WIDTH + 2 * KV_WIDTH + POOL_WIDTH
D_FF = 5632
N_EXPERTS = 8
D_FF_EXPERT = D_MODEL // 2
EPS = 1e-5
NEG_INF = -1e30

LANES = 128
BF16_ROWS = 16
VMEM_LIMIT = 56 * 1024 * 1024
DENSE_SLABS = 4
ROUTER_PAD = LANES


def _params(n_axes=1, vmem=VMEM_LIMIT):
    return pltpu.CompilerParams(dimension_semantics=("arbitrary",) * n_axes, vmem_limit_bytes=vmem)


def _rms(x, g):
    ms = jnp.mean(x * x, axis=-1, keepdims=True)
    return (x * lax.rsqrt(ms + EPS)) * g


def _rope_mix(zc, ra, rb, rc):
    return zc * ra + pltpu.roll(zc, ROPE_DIM // 2, 1) * rb + pltpu.roll(zc, LANES - ROPE_DIM // 2, 1) * rc


def _split_bf16(a):
    hi = a.astype(BF16)
    return hi, (a - hi.astype(F32)).astype(BF16)


def _dot3(a, b, dims=(((1,), (0,)), ((), ()))):
    m = a.shape[0]
    ah, al = _split_bf16(a)
    bh, bl = _split_bf16(b)
    t = lax.dot_general(jnp.concatenate([ah, al], axis=0), bh, dims, preferred_element_type=F32)
    return t[:m] + t[m:] + lax.dot_general(ah, bl, dims, preferred_element_type=F32)


def _stash_piece(step, w_ref, wbf, n_pieces):
    rows = w_ref.shape[0]

    @pl.when(step < n_pieces)
    def _():
        r0 = pl.multiple_of(step * rows, rows)
        wbf[pl.ds(r0, rows), :] = w_ref[...].astype(BF16)


def _piece_rows(total_rows, n_pieces):
    rows, rem = divmod(total_rows, n_pieces)
    assert rem == 0 and rows % BF16_ROWS == 0, (total_rows, n_pieces)
    return rows


def _inproj_kernel(x_ref, g_ref, w_ref, ra_ref, rb_ref, rc_ref, q_ref, k_ref, v_ref, u_ref, kvt_ref, ut_ref,
                   wbf, *, n_pieces):
    step = pl.program_id(0)
    _stash_piece(step, w_ref, wbf, n_pieces)

    @pl.when(step >= n_pieces)
    def _():
        tm = x_ref.shape[0]
        h = _rms(x_ref[...], g_ref[...])
        z = jnp.dot(h.astype(BF16), wbf[...], preferred_element_type=F32)
        ra, rb, rc = ra_ref[...], rb_ref[...], rc_ref[...]
        n_rot = (ATTN_WIDTH + KV_WIDTH) // LANES
        rot = []
        for c in range(n_rot):
            r = _rope_mix(z[:, c * LANES:(c + 1) * LANES], ra, rb, rc)
            if c >= ATTN_WIDTH // LANES:
                r = r * (HEAD_DIM ** 0.5)
            rot.append(r)
        nq = ATTN_WIDTH // LANES
        for c in range(nq):
            q_ref[:, c * LANES:(c + 1) * LANES] = rot[c].astype(q_ref.dtype)
        for c in range(nq, n_rot):
            k_ref[:, (c - nq) * LANES:(c - nq + 1) * LANES] = rot[c].astype(k_ref.dtype)
        v = z[:, ATTN_WIDTH + KV_WIDTH:ATTN_WIDTH + 2 * KV_WIDTH]
        u = z[:, ATTN_WIDTH + 2 * KV_WIDTH:]
        v_ref[...] = v.astype(v_ref.dtype)
        u_ref[...] = u.astype(u_ref.dtype)
        for c in range(nq, n_rot):
            kvt_ref[:, (c - nq) * LANES:(c - nq + 1) * LANES] = rot[c][tm - WINDOW:, :]
        kvt_ref[:, KV_WIDTH:] = v[tm - WINDOW:, :]
        ut_ref[...] = u[tm - BF16_ROWS:, :]


def _inproj(x, g, w_in, layer, rope, *, tm):
    m = x.shape[0]
    nm = m // tm
    n_pieces = 4
    rows = _piece_rows(D_MODEL, n_pieces)
    tiles_per_seq = SEQ // tm
    tile = lambda i: jnp.maximum(i - n_pieces, 0)
    row_spec = lambda width: pl.BlockSpec((tm, width), lambda i: (tile(i), 0))
    rope_spec = pl.BlockSpec((tm, LANES), lambda i: (tile(i) % tiles_per_seq, 0))
    out_shape = [jax.ShapeDtypeStruct((m, ATTN_WIDTH), BF16),
                 jax.ShapeDtypeStruct((m, KV_WIDTH), BF16),
                 jax.ShapeDtypeStruct((m, KV_WIDTH), BF16),
                 jax.ShapeDtypeStruct((m, POOL_WIDTH), BF16),
                 jax.ShapeDtypeStruct((BATCH * WINDOW, 2 * KV_WIDTH), F32),
                 jax.ShapeDtypeStruct((BATCH * BF16_ROWS, POOL_WIDTH), F32)]
    out_specs = [row_spec(ATTN_WIDTH), row_spec(KV_WIDTH), row_spec(KV_WIDTH), row_spec(POOL_WIDTH),
                 pl.BlockSpec((WINDOW, 2 * KV_WIDTH), lambda i: (tile(i) // tiles_per_seq, 0)),
                 pl.BlockSpec((BF16_ROWS, POOL_WIDTH), lambda i: (tile(i) // tiles_per_seq, 0))]
    return pl.pallas_call(
        functools.partial(_inproj_kernel, n_pieces=n_pieces),
        grid=(n_pieces + nm,),
        in_specs=[row_spec(D_MODEL),
                  pl.BlockSpec((1, D_MODEL), lambda i: (0, 0)),
                  pl.BlockSpec((None, rows, IN_WIDTH), lambda i: (layer, jnp.minimum(i, n_pieces - 1), 0)),
                  rope_spec, rope_spec, rope_spec],
        out_specs=out_specs,
        out_shape=out_shape,
        scratch_shapes=[pltpu.VMEM((D_MODEL, IN_WIDTH), BF16)],
        compiler_params=_params(),
        name="inproj",
    )(x, g, w_in, *rope)


def _sink_softmax(s, sink, axis):
    mx = jnp.maximum(jnp.max(s, axis=axis, keepdims=True), sink)
    p = jnp.exp(s - mx)
    den = jnp.sum(p, axis=axis, keepdims=True) + jnp.exp(sink - mx)
    return p * (1.0 / den)


def _pool_group(d, wpb_g, scale_g):
    return jnp.dot(d.astype(BF16), wpb_g, preferred_element_type=F32) * scale_g


MIXER_QB = 8


def _mixer_kernel(sink_ref, q_ref, kc_ref, kp_ref, vc_ref, vp_ref, uc_ref, up_ref, wp_ref, sc_ref,
                  o_ref, wpb, band):
    b = pl.program_id(0)
    m = pl.program_id(1)
    blk = WINDOW

    @pl.when((b == 0) & (m == 0))
    def _():
        wpb[...] = wp_ref[...].astype(BF16)
        tok = lax.broadcasted_iota(jnp.int32, (blk, 2 * blk), 0) + blk
        src = lax.broadcasted_iota(jnp.int32, (blk, 2 * blk), 1)
        for g, w in enumerate(POOL_WINDOWS):
            band[g] = jnp.where((src <= tok) & (src > tok - w), 1.0, 0.0).astype(BF16)

    half = LANES // 2
    kall = jnp.concatenate([kp_ref[...], kc_ref[...]], axis=0).astype(F32)
    vall = jnp.concatenate([vp_ref[...], vc_ref[...]], axis=0).astype(F32)
    uall = jnp.concatenate([jnp.where(m > 0, up_ref[...], jnp.zeros_like(up_ref)), uc_ref[...]], axis=0)
    lane = lax.broadcasted_iota(jnp.int32, (2 * blk, LANES), 1)
    low = lane < half
    key = lax.broadcasted_iota(jnp.int32, (4 * blk, blk), 0) & (2 * blk - 1)
    qry = lax.broadcasted_iota(jnp.int32, (4 * blk, blk), 1)
    in_window = (key > qry) & (key <= qry + blk)

    for qb in range(MIXER_QB):
        r0 = qb * blk
        kk, vv = kall[r0:r0 + 2 * blk], vall[r0:r0 + 2 * blk]
        valid = in_window & ((m > 0) | (key >= blk)) if qb == 0 else in_window
        for h in range(N_KV_HEADS):
            g0 = (h // 2) * LANES
            kg, vg = kk[:, g0:g0 + LANES], vv[:, g0:g0 + LANES]
            if h % 2 == 0:
                klo, vlo = jnp.where(low, kg, 0.0), jnp.where(low, vg, 0.0)
                khi, vhi = pltpu.roll(klo, half, 1), pltpu.roll(vlo, half, 1)
            else:
                khi, vhi = jnp.where(low, 0.0, kg), jnp.where(low, 0.0, vg)
                klo, vlo = pltpu.roll(khi, half, 1), pltpu.roll(vhi, half, 1)
            kcat = jnp.concatenate([klo, khi], axis=0).astype(BF16)
            vcat = jnp.concatenate([vlo, vhi], axis=0).astype(BF16)
            for pair in range(h * GROUP // 2, (h + 1) * GROUP // 2):
                qp = q_ref[r0:r0 + blk, pair * LANES:(pair + 1) * LANES]
                st = lax.dot_general(kcat, qp, (((1,), (1,)), ((), ())), preferred_element_type=F32)
                st = jnp.where(valid, st, NEG_INF)
                pt = [_sink_softmax(st[t * 2 * blk:(t + 1) * 2 * blk], sink_ref[2 * pair + t], 0).astype(BF16)
                      for t in range(2)]
                o = lax.dot_general(jnp.concatenate(pt, axis=0), vcat, (((0,), (0,)), ((), ())),
                                    preferred_element_type=F32)
                o_ref[r0:r0 + blk, pair * LANES:(pair + 1) * LANES] = o.astype(o_ref.dtype)

        ext = uall[r0:r0 + 2 * blk]
        pos = ((m * MIXER_QB + qb) * blk + lax.broadcasted_iota(jnp.int32, (blk, 1), 0)).astype(F32)
        for g, w in enumerate(POOL_WINDOWS):
            c0, c1 = g * POOL_GROUP_DIM, (g + 1) * POOL_GROUP_DIM
            wsum = jnp.dot(band[g], ext[:, c0:c1], preferred_element_type=F32)
            d = wsum * (1.0 / jnp.minimum(float(w), pos + 1.0)) - uc_ref[r0:r0 + blk, c0:c1].astype(F32)
            y = _pool_group(d, wpb[g], sc_ref[:, c0:c1])
            o_ref[r0:r0 + blk, ATTN_WIDTH + c0:ATTN_WIDTH + c1] = y.astype(o_ref.dtype)


def _mixer(q, k, v, u, sinks, w_pool, scale, layer):
    assert u.dtype == BF16
    qb = MIXER_QB
    nb = SEQ // WINDOW
    ns = nb // qb
    cur = lambda b, m: (b * ns + m, 0)
    prev = lambda b, m: (b * nb + jnp.maximum(m * qb - 1, 0), 0)
    n_grp = len(POOL_WINDOWS)
    return pl.pallas_call(
        _mixer_kernel,
        grid=(BATCH, ns),
        in_specs=[pl.BlockSpec(memory_space=pltpu.SMEM),
                  pl.BlockSpec((qb * WINDOW, ATTN_WIDTH), cur),
                  pl.BlockSpec((qb * WINDOW, KV_WIDTH), cur), pl.BlockSpec((WINDOW, KV_WIDTH), prev),
                  pl.BlockSpec((qb * WINDOW, KV_WIDTH), cur), pl.BlockSpec((WINDOW, KV_WIDTH), prev),
                  pl.BlockSpec((qb * WINDOW, POOL_WIDTH), cur), pl.BlockSpec((WINDOW, POOL_WIDTH), prev),
                  pl.BlockSpec((None, n_grp, POOL_GROUP_DIM, POOL_GROUP_DIM), lambda b, m: (layer, 0, 0, 0)),
                  pl.BlockSpec((1, POOL_WIDTH), lambda b, m: (0, 0))],
        out_specs=pl.BlockSpec((qb * WINDOW, D_MODEL), cur),
        out_shape=jax.ShapeDtypeStruct((BATCH * SEQ, D_MODEL), BF16),
        scratch_shapes=[pltpu.VMEM((n_grp, POOL_GROUP_DIM, POOL_GROUP_DIM), BF16),
                        pltpu.VMEM((n_grp, WINDOW, 2 * WINDOW), BF16)],
        compiler_params=_params(2),
        name="mixer",
    )(sinks, q, k, k, v, v, u, u, w_pool, scale)


SAMPLE_SB = 8


def _sample_mixer_kernel(sink_ref, q_ref, kn_ref, vn_ref, un_ref, ck_ref, cv_ref, st_ref, wp_ref, sc_ref,
                         a_ref, nk_ref, nv_ref, nu_ref, po_ref, dscr):
    step = pl.program_id(0)
    sink = sink_ref[...]
    own = (lax.broadcasted_iota(jnp.int32, (N_HEADS, KV_WIDTH), 0) // GROUP
           == lax.broadcasted_iota(jnp.int32, (N_HEADS, KV_WIDTH), 1) // HEAD_DIM)
    for j in range(SAMPLE_SB):
        nk_ref[j, 0:WINDOW - 1, :] = ck_ref[j, 1:WINDOW, :]
        nk_ref[j, WINDOW - 1:WINDOW, :] = kn_ref[j]
        nv_ref[j, 0:WINDOW - 1, :] = cv_ref[j, 1:WINDOW, :]
        nv_ref[j, WINDOW - 1:WINDOW, :] = vn_ref[j]

        q = q_ref[j]
        qe = jnp.where(own, jnp.concatenate([q] * N_KV_HEADS, axis=1), 0.0)
        s = _dot3(qe, nk_ref[j], (((1,), (1,)), ((), ())))
        r = _dot3(_sink_softmax(s, sink, 1), nv_ref[j])
        r = jnp.where(own, r, 0.0)
        o = r[:, 0:HEAD_DIM]
        for h in range(1, N_KV_HEADS):
            o = o + r[:, h * HEAD_DIM:(h + 1) * HEAD_DIM]
        a_ref[j] = o

        st = st_ref[j]
        un = un_ref[j]
        nu_ref[j, 0:POOL_STATE - 1, :] = st[1:POOL_STATE, :]
        nu_ref[j, POOL_STATE - 1:POOL_STATE, :] = un
        for g, w in enumerate(POOL_WINDOWS):
            c0, c1 = g * POOL_GROUP_DIM, (g + 1) * POOL_GROUP_DIM
            tok = un[:, c0:c1]
            wsum = tok + jnp.sum(st[POOL_STATE - (w - 1):POOL_STATE, c0:c1], axis=0, keepdims=True)
            dscr[pl.ds(step * SAMPLE_SB + j, 1), c0:c1] = wsum / float(min(w, PAST_LEN + 1)) - tok

    @pl.when(step == pl.num_programs(0) - 1)
    def _():
        for g in range(len(POOL_WINDOWS)):
            c0, c1 = g * POOL_GROUP_DIM, (g + 1) * POOL_GROUP_DIM
            po_ref[:, c0:c1] = _dot3(dscr[:, c0:c1], wp_ref[g]) * sc_ref[:, c0:c1]


def _sample_mixer(q, k, v, u, cache_k, cache_v, state, sinks, w_pool, scale, layer):
    nb, sb = DEC_BATCH, SAMPLE_SB
    per_b = lambda *shape: pl.BlockSpec((sb,) + shape, lambda b: (b,) + (0,) * len(shape))
    per_lb = lambda *shape: pl.BlockSpec((None, sb) + shape, lambda b: (layer, b) + (0,) * len(shape))
    return pl.pallas_call(
        _sample_mixer_kernel,
        grid=(nb // sb,),
        in_specs=[pl.BlockSpec((N_HEADS, 1), lambda b: (0, 0)),
                  per_b(N_HEADS, HEAD_DIM), per_b(1, KV_WIDTH), per_b(1, KV_WIDTH), per_b(1, POOL_WIDTH),
                  per_lb(WINDOW, KV_WIDTH), per_lb(WINDOW, KV_WIDTH), per_lb(POOL_STATE, POOL_WIDTH),
                  pl.BlockSpec((None, len(POOL_WINDOWS), POOL_GROUP_DIM, POOL_GROUP_DIM),
                               lambda b: (layer, 0, 0, 0)),
                  pl.BlockSpec((1, POOL_WIDTH), lambda b: (0, 0))],
        out_specs=[per_b(N_HEADS, HEAD_DIM), per_lb(WINDOW, KV_WIDTH), per_lb(WINDOW, KV_WIDTH),
                   per_lb(POOL_STATE, POOL_WIDTH), pl.BlockSpec((nb, POOL_WIDTH), lambda b: (0, 0))],
        out_shape=[jax.ShapeDtypeStruct((nb, N_HEADS, HEAD_DIM), F32),
                   jax.ShapeDtypeStruct(cache_k.shape, F32),
                   jax.ShapeDtypeStruct(cache_v.shape, F32),
                   jax.ShapeDtypeStruct(state.shape, F32),
                   jax.ShapeDtypeStruct((nb, POOL_WIDTH), F32)],
        scratch_shapes=[pltpu.VMEM((nb, POOL_WIDTH), F32)],
        input_output_aliases={5: 1, 6: 2, 7: 3},
        compiler_params=_params(),
        name="sample_mixer",
    )(sinks.reshape(N_HEADS, 1), q.reshape(nb, N_HEADS, HEAD_DIM), k.reshape(nb, 1, KV_WIDTH),
      v.reshape(nb, 1, KV_WIDTH), u.reshape(nb, 1, POOL_WIDTH), cache_k, cache_v, state, w_pool, scale)


SAMPLE_TN = 512


def _sample_inproj_kernel(x_ref, g_ref, w_ref, ra_ref, rb_ref, rc_ref, z_ref):
    j = pl.program_id(0)
    z = _dot3(_rms(x_ref[...], g_ref[...]), w_ref[...])
    ra, rb, rc = ra_ref[...], rb_ref[...], rc_ref[...]
    per_tile = SAMPLE_TN // LANES
    for c in range(per_tile):
        zc = z[:, c * LANES:(c + 1) * LANES]
        grp = j * per_tile + c
        r = _rope_mix(zc, ra, rb, rc)
        r = jnp.where(grp >= ATTN_WIDTH // LANES, r * (HEAD_DIM ** 0.5), r)
        z_ref[:, c * LANES:(c + 1) * LANES] = jnp.where(grp < (ATTN_WIDTH + KV_WIDTH) // LANES, r, zc)


def _sample_inproj(x, g, w_in, layer, rope):
    m = x.shape[0]
    const = lambda r, c: pl.BlockSpec((r, c), lambda j: (0, 0))
    return pl.pallas_call(
        _sample_inproj_kernel,
        grid=(IN_WIDTH // SAMPLE_TN,),
        in_specs=[const(m, D_MODEL), const(1, D_MODEL),
                  pl.BlockSpec((None, D_MODEL, SAMPLE_TN), lambda j: (layer, 0, j)),
                  const(m, LANES), const(m, LANES), const(m, LANES)],
        out_specs=pl.BlockSpec((m, SAMPLE_TN), lambda j: (0, j)),
        out_shape=jax.ShapeDtypeStruct((m, IN_WIDTH), F32),
        compiler_params=_params(),
        name="sample_inproj",
    )(x, g, w_in, *rope)


def _sample_outproj_kernel(mix_ref, x_ref, w_ref, o_ref):
    o_ref[...] = x_ref[...] + _dot3(mix_ref[...], w_ref[...])


def _sample_outproj(mix, x, w_out, layer):
    m = x.shape[0]
    col = pl.BlockSpec((m, SAMPLE_TN), lambda j: (0, j))
    return pl.pallas_call(
        _sample_outproj_kernel,
        grid=(D_MODEL // SAMPLE_TN,),
        in_specs=[pl.BlockSpec((m, D_MODEL), lambda j: (0, 0)), col,
                  pl.BlockSpec((None, D_MODEL, SAMPLE_TN), lambda j: (layer, 0, j))],
        out_specs=col,
        out_shape=jax.ShapeDtypeStruct((m, D_MODEL), F32),
        compiler_params=_params(),
        name="sample_outproj",
    )(mix, x, w_out)


def _sample_ffn_kernel(x_ref, g_ref, *rest, moe, final_norm):
    rest = list(rest)
    wr_ref, br_ref = (rest.pop(0), rest.pop(0)) if moe else (None, None)
    gf_ref = rest.pop(0) if final_norm else None
    wg_ref, wu_ref, wd_ref, o_ref, acc, hn_scr, cmb = rest
    e, f = pl.program_id(0), pl.program_id(1)

    @pl.when((e == 0) & (f == 0))
    def _():
        hn = _rms(x_ref[...], g_ref[...])
        hn_scr[...] = hn
        acc[...] = jnp.zeros_like(acc)
        if moe:
            cmb[...] = _top2_gates(_dot3(hn, wr_ref[...]) + br_ref[...])

    hn = hn_scr[...]
    gate = _dot3(hn, wg_ref[...])
    up = _dot3(hn, wu_ref[...])
    h = (gate * (1.0 / (1.0 + jnp.exp(-gate)))) * up
    if moe:
        lane = lax.broadcasted_iota(jnp.int32, cmb.shape, 1)
        h = h * jnp.sum(jnp.where(lane == e, cmb[...], 0.0), axis=1, keepdims=True)
    acc[...] += _dot3(h, wd_ref[...])

    @pl.when((e == pl.num_programs(0) - 1) & (f == pl.num_programs(1) - 1))
    def _():
        out = x_ref[...] + acc[...]
        if final_norm:
            out = _rms(out, gf_ref[...])
        o_ref[...] = out


def _sample_ffn(x, g, wg, wu, wd, idx, *, router=None, g_final=None):
    m = x.shape[0]
    moe = router is not None
    n_e = N_EXPERTS if moe else 1
    width = wg.shape[-1]
    const = lambda r, c: pl.BlockSpec((r, c), lambda e, f: (0, 0))
    in_specs = [const(m, D_MODEL), const(1, D_MODEL)]
    args = [x, g]
    if moe:
        in_specs += [const(D_MODEL, ROUTER_PAD), const(1, ROUTER_PAD)]
        args += list(router)
        up_spec = pl.BlockSpec((None, None, D_MODEL, SAMPLE_TN), lambda e, f: (idx, e, 0, f))
        dn_spec = pl.BlockSpec((None, None, SAMPLE_TN, D_MODEL), lambda e, f: (idx, e, f, 0))
    else:
        up_spec = pl.BlockSpec((None, D_MODEL, SAMPLE_TN), lambda e, f: (idx, 0, f))
        dn_spec = pl.BlockSpec((None, SAMPLE_TN, D_MODEL), lambda e, f: (idx, f, 0))
    if g_final is not None:
        in_specs.append(const(1, D_MODEL))
        args.append(g_final)
    return pl.pallas_call(
        functools.partial(_sample_ffn_kernel, moe=moe, final_norm=g_final is not None),
        grid=(n_e, width // SAMPLE_TN),
        in_specs=in_specs + [up_spec, up_spec, dn_spec],
        out_specs=const(m, D_MODEL),
        out_shape=jax.ShapeDtypeStruct((m, D_MODEL), F32),
        scratch_shapes=[pltpu.VMEM((m, D_MODEL), F32), pltpu.VMEM((m, D_MODEL), F32),
                        pltpu.VMEM((m, ROUTER_PAD), F32)],
        compiler_params=_params(2),
        name="sample_ffn",
    )(*args, wg, wu, wd)


def _top2(logits):
    lane = lax.broadcasted_iota(jnp.int32, logits.shape, 1)
    lg = jnp.where(lane < N_EXPERTS, logits, -jnp.inf)
    m1 = jnp.max(lg, axis=1, keepdims=True)
    i1 = jnp.min(jnp.where(lg == m1, lane, ROUTER_PAD), axis=1, keepdims=True)
    lg2 = jnp.where(lane == i1, -jnp.inf, lg)
    m2 = jnp.max(lg2, axis=1, keepdims=True)
    i2 = jnp.min(jnp.where(lg2 == m2, lane, ROUTER_PAD), axis=1, keepdims=True)
    e2 = jnp.exp(m2 - m1)
    den = 1.0 + e2
    return i1, i2, 1.0 / den, e2 / den


def _top2_gates(logits):
    i1, i2, g1, g2 = _top2(logits)
    lane = lax.broadcasted_iota(jnp.int32, logits.shape, 1)
    return jnp.where(lane == i1, g1, 0.0) + jnp.where(lane == i2, g2, 0.0)


ROUTE_I1, ROUTE_I2, ROUTE_G1, ROUTE_G2, ROUTE_P1, ROUTE_P2 = range(6)


def _outproj_kernel(mix_ref, x_ref, w_ref, g_ref, *rest, n_pieces, with_router):
    if with_router:
        wr_ref, br_ref, xo_ref, hn_ref, route_ref, cnt_ref, wbf, tri, carry = rest
    else:
        xo_ref, hn_ref, wbf = rest
    step = pl.program_id(0)
    _stash_piece(step, w_ref, wbf, n_pieces)

    if with_router:
        @pl.when(step == 0)
        def _():
            row = lax.broadcasted_iota(jnp.int32, tri.shape, 0)
            col = lax.broadcasted_iota(jnp.int32, tri.shape, 1)
            tri[...] = jnp.where(col < row, 1.0, 0.0).astype(BF16)
            carry[...] = jnp.zeros_like(carry)

    @pl.when(step >= n_pieces)
    def _():
        y = jnp.dot(mix_ref[...], wbf[...], preferred_element_type=F32)
        xo = x_ref[...] + y
        xo_ref[...] = xo
        hn = _rms(xo, g_ref[...])
        if not with_router:
            hn_ref[...] = hn.astype(BF16)
            return
        hn_ref[...] = hn
        logits = jnp.dot(hn.astype(BF16), wr_ref[...].astype(BF16), preferred_element_type=F32) + br_ref[...]
        i1, i2, g1, g2 = _top2(logits)
        lane = lax.broadcasted_iota(jnp.int32, logits.shape, 1)
        picked = jnp.where((lane == i1) | (lane == i2), 1.0, 0.0)
        ahead = jnp.dot(tri[...], picked.astype(BF16), preferred_element_type=F32) + carry[...]
        p1 = jnp.sum(jnp.where(lane == i1, ahead, 0.0), axis=1, keepdims=True)
        p2 = jnp.sum(jnp.where(lane == i2, ahead, 0.0), axis=1, keepdims=True)
        carry[...] += jnp.sum(picked, axis=0, keepdims=True)
        cnt_ref[...] = carry[...]
        rec = jnp.zeros(logits.shape, F32)
        for k, val in ((ROUTE_I1, i1.astype(F32)), (ROUTE_I2, i2.astype(F32)), (ROUTE_G1, g1), (ROUTE_G2, g2),
                       (ROUTE_P1, p1), (ROUTE_P2, p2)):
            rec = jnp.where(lane == k, val, rec)
        route_ref[...] = rec


def _outproj(mix, x, w_out, g, layer, router, *, tm):
    m = x.shape[0]
    nm = m // tm
    n_pieces = 4
    rows = _piece_rows(D_MODEL, n_pieces)
    tile = lambda i: jnp.maximum(i - n_pieces, 0)
    row_spec = lambda width: pl.BlockSpec((tm, width), lambda i: (tile(i), 0))
    const = lambda r, c: pl.BlockSpec((r, c), lambda i: (0, 0))
    in_specs = [row_spec(D_MODEL), row_spec(D_MODEL),
                pl.BlockSpec((None, rows, D_MODEL), lambda i: (layer, jnp.minimum(i, n_pieces - 1), 0)),
                const(1, D_MODEL)]
    args = [mix, x, w_out, g]
    scratch = [pltpu.VMEM((D_MODEL, D_MODEL), BF16)]
    if router is None:
        out_shape = [jax.ShapeDtypeStruct((m, D_MODEL), F32), jax.ShapeDtypeStruct((m, D_MODEL), BF16)]
        out_specs = [row_spec(D_MODEL), row_spec(D_MODEL)]
    else:
        in_specs += [const(D_MODEL, ROUTER_PAD), const(1, ROUTER_PAD)]
        args += list(router)
        out_shape = [jax.ShapeDtypeStruct((m, D_MODEL), F32), jax.ShapeDtypeStruct((m, D_MODEL), F32),
                     jax.ShapeDtypeStruct((m, ROUTER_PAD), F32), jax.ShapeDtypeStruct((1, ROUTER_PAD), F32)]
        out_specs = [row_spec(D_MODEL), row_spec(D_MODEL), row_spec(ROUTER_PAD), const(1, ROUTER_PAD)]
        scratch += [pltpu.VMEM((tm, tm), BF16), pltpu.VMEM((1, ROUTER_PAD), F32)]
    return pl.pallas_call(
        functools.partial(_outproj_kernel, n_pieces=n_pieces, with_router=router is not None),
        grid=(n_pieces + nm,),
        in_specs=in_specs,
        out_specs=out_specs,
        out_shape=out_shape,
        scratch_shapes=scratch,
        compiler_params=_params(),
        name="outproj",
    )(*args)


EXPERT_TM = 512
EXPERT_PIECES = 4
ROUTE_TM = 512
SORTED_ROWS = BATCH * SEQ * 2 + N_EXPERTS * EXPERT_TM
EXPERT_STEPS = N_EXPERTS * EXPERT_PIECES + SORTED_ROWS // EXPERT_TM
EXPERT_VMEM_LIMIT = 60 * 1024 * 1024


def _take(table, idx):
    hot = idx[..., None] == jnp.arange(table.shape[0], dtype=idx.dtype)
    return jnp.sum(jnp.where(hot, table, 0), axis=-1)


def _route_plan(route, counts):
    i32 = jnp.int32
    t, p = EXPERT_TM, EXPERT_PIECES
    cnt = counts[0, :N_EXPERTS].astype(i32)
    ntile = (cnt + t - 1) // t
    tile0 = jnp.cumsum(ntile) - ntile
    experts = route[:, ROUTE_I1:ROUTE_I2 + 1].astype(i32)
    dest = _take(tile0, experts) * t + route[:, ROUTE_P1:ROUTE_P2 + 1].astype(i32)
    n_all = SORTED_ROWS // t
    last = N_EXPERTS - 1
    ids = jnp.arange(N_EXPERTS, dtype=i32)
    blen = jnp.where(ids < last, jnp.maximum(ntile, p), ntile)
    bstart = p + jnp.cumsum(blen) - blen
    total = p + jnp.sum(blen)
    n_tiles = jnp.sum(ntile)
    s = jnp.arange(EXPERT_STEPS, dtype=i32)
    pro = s < p
    e = jnp.clip(jnp.sum((s[:, None] >= bstart[None, :]).astype(i32), axis=1) - 1, 0, last)
    k = s - _take(bstart, e)
    live = jnp.logical_not(pro) & (s < total)
    comp = live & (k < _take(ntile, e))
    stage = pro | (live & (k < p) & (e < last))
    w_expert = jnp.where(pro, 0, jnp.minimum(e + 1, last))
    w_piece = jnp.where(pro, s, jnp.where(stage, k, p - 1))
    first = _take(tile0, e)
    spare = n_tiles + s - total
    tile = jnp.where(comp, first + k,
                     jnp.where(pro, 0, jnp.where(live, first + _take(ntile, e), spare)))
    tile = jnp.minimum(tile, n_all - 1)
    act = jnp.where(comp, 1, jnp.where((s >= total) & (spare < n_all), 2, 0))
    nvalid = jnp.clip(_take(cnt, e) - (tile - first) * t, 0, t)
    steps = (w_expert, w_piece, stage.astype(i32), tile, act, nvalid, e % 2)
    dest = dest.reshape(dest.shape[0] // ROUTE_TM, 1, 2 * ROUTE_TM)
    partial = jnp.where(ntile > 0, tile0 + ntile - 1, -1)
    unused = n_tiles + jnp.arange(N_EXPERTS, dtype=i32)
    pad_rows = jnp.concatenate([partial, jnp.where(unused < n_all, unused, -1)]) * t
    return dest, pad_rows, steps


def _start_rows(n, copies_of):
    def start(t, c):
        for cp in copies_of(t):
            cp.start()
        return c

    lax.fori_loop(0, n, start, 0, unroll=8)


def _wait_rows(n, copies_of):
    def wait(t, c):
        for cp in copies_of(t):
            cp.wait()
        return c

    lax.fori_loop(0, n, wait, 0, unroll=8)


def _dispatch_kernel(dest_ref, pad_ref, hn_ref, xg_hbm, zeros, sem):
    step = pl.program_id(0)

    @pl.when(step == 0)
    def _():
        zeros[...] = jnp.zeros_like(zeros)

        def fill(j):
            row = pl.multiple_of(jnp.maximum(pad_ref[j], 0), EXPERT_TM)
            return pltpu.make_async_copy(zeros, xg_hbm.at[pl.ds(row, EXPERT_TM)], sem)

        for j in range(pad_ref.shape[0]):
            pl.when(pad_ref[j] >= 0)(lambda j=j: fill(j).start())
        for j in range(pad_ref.shape[0]):
            pl.when(pad_ref[j] >= 0)(lambda j=j: fill(j).wait())

    def copies_of(t):
        src = hn_ref.at[pl.ds(t, 1)]
        return [pltpu.make_async_copy(src, xg_hbm.at[pl.ds(dest_ref[0, 2 * t + k], 1)], sem) for k in range(2)]

    _start_rows(ROUTE_TM, copies_of)
    _wait_rows(ROUTE_TM, copies_of)


def _dispatch(hn, dest, pad_rows):
    return pl.pallas_call(
        _dispatch_kernel,
        grid=(hn.shape[0] // ROUTE_TM,),
        in_specs=[pl.BlockSpec((None, 1, 2 * ROUTE_TM), lambda s: (s, 0, 0), memory_space=pltpu.SMEM),
                  pl.BlockSpec(memory_space=pltpu.SMEM),
                  pl.BlockSpec((ROUTE_TM, D_MODEL), lambda s: (s, 0))],
        out_specs=pl.BlockSpec(memory_space=pl.ANY),
        out_shape=jax.ShapeDtypeStruct((SORTED_ROWS, D_MODEL), F32),
        scratch_shapes=[pltpu.VMEM((EXPERT_TM, D_MODEL), F32), pltpu.SemaphoreType.DMA],
        compiler_params=_params(),
        name="dispatch",
    )(dest, pad_rows, hn)


def _expert_kernel(st_we, st_wp, st_stage, st_t, st_act, st_nv, st_slot, x_ref, wg_ref, wu_ref, wd_ref, y_ref,
                   wgb, wub, wdb):
    s = pl.program_id(0)

    @pl.when(st_stage[s] == 1)
    def _():
        slot = st_we[s] & 1
        for w_ref, wbf in ((wg_ref, wgb), (wu_ref, wub), (wd_ref, wdb)):
            rows = w_ref.shape[0]
            r0 = pl.multiple_of(st_wp[s] * rows, rows)
            wbf[slot, pl.ds(r0, rows), :] = w_ref[...].astype(BF16)

    act = st_act[s]

    @pl.when(act == 1)
    def _():
        slot = st_slot[s]
        row = lax.broadcasted_iota(jnp.int32, (x_ref.shape[0], 1), 0)
        x = jnp.where(row < st_nv[s], x_ref[...], 0.0).astype(BF16)
        gate = jnp.dot(x, wgb[slot], preferred_element_type=F32)
        up = jnp.dot(x, wub[slot], preferred_element_type=F32)
        h = (gate * (1.0 / (1.0 + jnp.exp(-gate)))) * up
        y_ref[...] = jnp.dot(h.astype(BF16), wdb[slot], preferred_element_type=F32)

    @pl.when(act == 2)
    def _():
        y_ref[...] = jnp.zeros_like(y_ref)


def _experts(xg, wg, wu, wd, idx, steps):
    t, p = EXPERT_TM, EXPERT_PIECES
    rows_in = _piece_rows(D_MODEL, p)
    rows_dn = _piece_rows(D_FF_EXPERT, p)
    w_map = lambda s, we, wp, *_: (idx, we[s], wp[s], 0)
    row_spec = pl.BlockSpec((t, D_MODEL), lambda s, we, wp, sg, tl, *_: (tl[s], 0))
    up_spec = pl.BlockSpec((None, None, rows_in, D_FF_EXPERT), w_map)
    return pl.pallas_call(
        _expert_kernel,
        grid_spec=pltpu.PrefetchScalarGridSpec(
            num_scalar_prefetch=len(steps),
            grid=(EXPERT_STEPS,),
            in_specs=[row_spec, up_spec, up_spec, pl.BlockSpec((None, None, rows_dn, D_MODEL), w_map)],
            out_specs=row_spec,
            scratch_shapes=[pltpu.VMEM((2, D_MODEL, D_FF_EXPERT), BF16), pltpu.VMEM((2, D_MODEL, D_FF_EXPERT), BF16),
                            pltpu.VMEM((2, D_FF_EXPERT, D_MODEL), BF16)]),
        out_shape=jax.ShapeDtypeStruct((SORTED_ROWS, D_MODEL), F32),
        compiler_params=_params(vmem=EXPERT_VMEM_LIMIT),
        name="experts",
    )(*steps, xg, wg, wu, wd)


def _combine_kernel(dest_ref, next_ref, x_ref, route_ref, *rest, final_norm):
    if final_norm:
        gf_ref, y_hbm, o_ref, ya, yb, sem = rest
    else:
        y_hbm, o_ref, ya, yb, sem = rest
    step = pl.program_id(0)
    n = x_ref.shape[0]
    slot = step % 2

    def gather(idx_ref, slot):
        def copies_of(t):
            return [pltpu.make_async_copy(y_hbm.at[pl.ds(idx_ref[0, 2 * t + k], 1)], buf.at[slot, pl.ds(t, 1)],
                                          sem.at[slot])
                    for k, buf in enumerate((ya, yb))]
        return copies_of

    @pl.when(step == 0)
    def _():
        _start_rows(n, gather(dest_ref, 0))

    @pl.when(step + 1 < pl.num_programs(0))
    def _():
        _start_rows(n, gather(next_ref, 1 - slot))

    _wait_rows(n, gather(dest_ref, slot))
    out = (x_ref[...] + route_ref[:, ROUTE_G1:ROUTE_G1 + 1] * ya[slot]
           + route_ref[:, ROUTE_G2:ROUTE_G2 + 1] * yb[slot])
    if final_norm:
        out = _rms(out, gf_ref[...])
    o_ref[...] = out


def _combine(x, route, y, dest, g_final):
    m = x.shape[0]
    n_steps = m // ROUTE_TM
    row_spec = lambda width: pl.BlockSpec((ROUTE_TM, width), lambda s: (s, 0))
    idx_spec = lambda ahead: pl.BlockSpec((None, 1, 2 * ROUTE_TM),
                                          lambda s: (jnp.minimum(s + ahead, n_steps - 1), 0, 0),
                                          memory_space=pltpu.SMEM)
    in_specs = [idx_spec(0), idx_spec(1), row_spec(D_MODEL), row_spec(ROUTER_PAD)]
    args = [dest, dest, x, route]
    if g_final is not None:
        in_specs.append(pl.BlockSpec((1, D_MODEL), lambda s: (0, 0)))
        args.append(g_final)
    return pl.pallas_call(
        functools.partial(_combine_kernel, final_norm=g_final is not None),
        grid=(n_steps,),
        in_specs=in_specs + [pl.BlockSpec(memory_space=pl.ANY)],
        out_specs=row_spec(D_MODEL),
        out_shape=jax.ShapeDtypeStruct((m, D_MODEL), F32),
        scratch_shapes=[pltpu.VMEM((2, ROUTE_TM, D_MODEL), F32), pltpu.VMEM((2, ROUTE_TM, D_MODEL), F32),
                        pltpu.SemaphoreType.DMA((2,))],
        compiler_params=_params(),
        name="combine",
    )(*args, y)


def _ffn_kernel(hn_ref, acc_ref, *rest, pieces, final_norm):
    rest = list(rest)
    gf_ref = rest.pop(0) if final_norm else None
    wg_ref, wu_ref, wd_ref, o_ref, wgb, wub, wdb = rest
    step = pl.program_id(0)
    _stash_piece(step, wg_ref, wgb, pieces[0])
    _stash_piece(step, wu_ref, wub, pieces[0])
    _stash_piece(step, wd_ref, wdb, pieces[1])

    @pl.when(step >= max(pieces))
    def _():
        hn = hn_ref[...]
        gate = jnp.dot(hn, wgb[...], preferred_element_type=F32)
        up = jnp.dot(hn, wub[...], preferred_element_type=F32)
        h = (gate * (1.0 / (1.0 + jnp.exp(-gate)))) * up
        out = acc_ref[...] + jnp.dot(h.astype(BF16), wdb[...], preferred_element_type=F32)
        if final_norm:
            out = _rms(out, gf_ref[...])
        o_ref[...] = out


def _ffn_slab(hn, acc, wg, wu, wd, idx, slab, width, *, tm, g_final=None):
    m = hn.shape[0]
    nm = m // tm
    pieces = (8, 8)
    n_pro = max(pieces)
    rows_in = _piece_rows(D_MODEL, pieces[0])
    rows_dn = _piece_rows(width, pieces[1])
    tile = lambda i: jnp.maximum(i - n_pro, 0)
    row_spec = lambda w_: pl.BlockSpec((tm, w_), lambda i: (tile(i), 0))
    in_specs = [row_spec(D_MODEL), row_spec(D_MODEL)]
    args = [hn, acc]
    if g_final is not None:
        in_specs.append(pl.BlockSpec((1, D_MODEL), lambda i: (0, 0)))
        args.append(g_final)
    up_spec = pl.BlockSpec((None, rows_in, width), lambda i: (idx, jnp.minimum(i, pieces[0] - 1), slab))
    in_specs += [up_spec, up_spec,
                 pl.BlockSpec((None, rows_dn, D_MODEL),
                              lambda i: (idx, slab * pieces[1] + jnp.minimum(i, pieces[1] - 1), 0))]
    args += [wg, wu, wd]
    return pl.pallas_call(
        functools.partial(_ffn_kernel, pieces=pieces, final_norm=g_final is not None),
        grid=(n_pro + nm,),
        in_specs=in_specs,
        out_specs=row_spec(D_MODEL),
        out_shape=jax.ShapeDtypeStruct((m, D_MODEL), F32),
        scratch_shapes=[pltpu.VMEM((D_MODEL, width), BF16), pltpu.VMEM((D_MODEL, width), BF16),
                        pltpu.VMEM((width, D_MODEL), BF16)],
        compiler_params=_params(),
        name="ffn_slab",
    )(*args)


def _rope_tables(pos):
    half = ROPE_DIM // 2
    inv = ROPE_THETA ** (-jnp.arange(0, ROPE_DIM, 2, dtype=F32) / ROPE_DIM)
    ang = pos[:, None] * inv[None, :]
    cos, sin = jnp.cos(ang), jnp.sin(ang)
    n = pos.shape[0]
    rest = HEAD_DIM - ROPE_DIM
    a = jnp.concatenate([cos, cos, jnp.ones((n, rest), F32)], axis=1)
    b = jnp.concatenate([jnp.zeros((n, half), F32), sin, jnp.zeros((n, rest), F32)], axis=1)
    c = jnp.concatenate([-sin, jnp.zeros((n, half + rest), F32)], axis=1)
    reps = LANES // HEAD_DIM
    scale = HEAD_DIM ** -0.5
    return tuple(jnp.tile(t, (1, reps)) * scale for t in (a, b, c))


def kernel(x_prompt, x_sample, cache_k, cache_v, state_pool, norm_mix, w_in, attn_sinks, w_pool, pool_scale,
           w_out, norm_ffn, w_gate_dense, w_up_dense, w_down_dense, w_router, b_router, w_gate_exp, w_up_exp,
           w_down_exp, norm_final):
    tm = 512
    xp = x_prompt.reshape(BATCH * SEQ, D_MODEL)
    xs = x_sample.reshape(DEC_BATCH, D_MODEL)
    rope_p = _rope_tables(jnp.arange(SEQ, dtype=F32))
    rope_s = _rope_tables(jnp.full((DEC_BATCH,), PAST_LEN, dtype=F32))
    ck = cache_k.reshape(DEPTH, DEC_BATCH, WINDOW, KV_WIDTH)
    cv = cache_v.reshape(DEPTH, DEC_BATCH, WINDOW, KV_WIDTH)
    g_final = norm_final.reshape(1, D_MODEL)
    slab_w = D_FF // DENSE_SLABS
    heads = (N_KV_HEADS, HEAD_DIM)

    pool_state = state_pool
    nk_p, nv_p, nu_p = [], [], []
    for l in range(DEPTH):
        g_mix = norm_mix[l].reshape(1, D_MODEL)
        g_ffn = norm_ffn[l].reshape(1, D_MODEL)
        scale = pool_scale[l].reshape(1, POOL_WIDTH)
        moe = l % 2 == 1
        i = l // 2
        last = l == DEPTH - 1
        router = None
        if moe:
            router = (jnp.pad(w_router[i], ((0, 0), (0, ROUTER_PAD - N_EXPERTS))),
                      jnp.pad(b_router[i], (0, ROUTER_PAD - N_EXPERTS)).reshape(1, ROUTER_PAD))

        q, k, v, u, kvt, ut = _inproj(xp, g_mix, w_in, l, rope_p, tm=tm)
        mix = _mixer(q, k, v, u, attn_sinks[l], w_pool, scale, l)
        xp, hn, *routing = _outproj(mix, xp, w_out, g_ffn, l, router, tm=tm)
        nk_p.append(kvt[:, :KV_WIDTH].reshape(BATCH, WINDOW, *heads))
        nv_p.append(kvt[:, KV_WIDTH:].reshape(BATCH, WINDOW, *heads))
        nu_p.append(ut.reshape(BATCH, BF16_ROWS, POOL_WIDTH)[:, BF16_ROWS - POOL_STATE:])
        if moe:
            route, counts = routing
            dest, pad_rows, steps = _route_plan(route, counts)
            y = _experts(_dispatch(hn, dest, pad_rows), w_gate_exp, w_up_exp, w_down_exp, i, steps)
            xp = _combine(xp, route, y, dest, g_final if last else None)
        else:
            for s in range(DENSE_SLABS):
                xp = _ffn_slab(hn, xp, w_gate_dense, w_up_dense, w_down_dense, i, s, slab_w, tm=tm,
                               g_final=g_final if last and s == DENSE_SLABS - 1 else None)

        z = _sample_inproj(xs, g_mix, w_in, l, rope_s)
        a_s, ck, cv, pool_state, ps = _sample_mixer(
            z[:, :ATTN_WIDTH], z[:, ATTN_WIDTH:ATTN_WIDTH + KV_WIDTH],
            z[:, ATTN_WIDTH + KV_WIDTH:ATTN_WIDTH + 2 * KV_WIDTH], z[:, ATTN_WIDTH + 2 * KV_WIDTH:],
            ck, cv, pool_state, attn_sinks[l], w_pool, scale, l)
        mix_s = jnp.concatenate([a_s.reshape(DEC_BATCH, ATTN_WIDTH), ps], axis=1)
        xs = _sample_outproj(mix_s, xs, w_out, l)
        if moe:
            xs = _sample_ffn(xs, g_ffn, w_gate_exp, w_up_exp, w_down_exp, i, router=router,
                             g_final=g_final if last else None)
        else:
            xs = _sample_ffn(xs, g_ffn, w_gate_dense, w_up_dense, w_down_dense, i,
                             g_final=g_final if last else None)

    y_prompt = xp.reshape(BATCH, SEQ, D_MODEL)
    y_sample = xs.reshape(DEC_BATCH, 1, D_MODEL)
    return (y_prompt, y_sample, jnp.stack(nk_p), jnp.stack(nv_p), jnp.stack(nu_p),
            ck.reshape(DEPTH, DEC_BATCH, WINDOW, *heads), cv.reshape(DEPTH, DEC_BATCH, WINDOW, *heads), pool_state)
```
